```python
import jax, jax.numpy as jnp
from jax import lax
import numpy as np

D_MODEL = 1024
BATCH = 2
SEQ = 8192
DEPTH = 2

CHUNK = 64
LOOKBACK_CHUNKS = 8
BAND = (LOOKBACK_CHUNKS + 1) * CHUNK
N_MIXERS = 2
N_HEADS = 16
HEAD_DIM = D_MODEL // N_HEADS
MAX_REL = 128
N_REL = 2 * MAX_REL + 1
SSM_GROUP = 16
SSM_GROUPS = D_MODEL // SSM_GROUP
SSM_STATE = 64
DT_MIN = 1e-3
DT_MAX = 1e-1
N_EXPERT_GROUPS = 4
EXPERTS_PER_GROUP = 8
N_EXPERTS = N_EXPERT_GROUPS * EXPERTS_PER_GROUP
TOP_K = 2
D_EXPERT = D_MODEL // 4
RMS_EPS = 1e-6
NEG_BIG = -1e30

kernel_name = "interleaved_chunkattn_s5_hiermoe"


def rms_norm(x, g):
    x32 = x.astype(jnp.float32)
    y = x32 * lax.rsqrt(jnp.mean(x32 * x32, axis=-1, keepdims=True) + RMS_EPS)
    return (y * g.astype(jnp.float32)).astype(x.dtype)


def chunked_rel_attention(h, w_qkv, w_o, rel_bias):
    b, s, _ = h.shape
    nc = s // CHUNK
    q, k, v = jnp.split(h @ w_qkv, 3, axis=-1)
    q = q.reshape(b, nc, CHUNK, N_HEADS, HEAD_DIM)
    pad = ((0, 0), (LOOKBACK_CHUNKS * CHUNK, 0), (0, 0))

    def gather_band(t):
        t = jnp.pad(t, pad).reshape(b, nc + LOOKBACK_CHUNKS, CHUNK, N_HEADS, HEAD_DIM)
        return jnp.concatenate([t[:, j:j + nc] for j in range(LOOKBACK_CHUNKS + 1)], axis=2)

    kb, vb = gather_band(k), gather_band(v)
    scores = jnp.einsum('bcqhd,bckhd->bhcqk', q, kb).astype(jnp.float32) * (HEAD_DIM ** -0.5)
    q_pos = LOOKBACK_CHUNKS * CHUNK + jnp.arange(CHUNK)
    dist = q_pos[:, None] - jnp.arange(BAND)[None, :]
    rel_idx = jnp.clip(dist, -MAX_REL, MAX_REL) + MAX_REL
    bias = rel_bias[:, rel_idx].astype(jnp.float32)
    key_chunk = jnp.arange(nc)[:, None] - LOOKBACK_CHUNKS + jnp.arange(BAND)[None, :] // CHUNK
    valid = key_chunk >= 0
    scores = jnp.where(valid[None, None, :, None, :], scores + bias[None, :, None], NEG_BIG)
    p = jax.nn.softmax(scores, axis=-1).astype(vb.dtype)
    o = jnp.einsum('bhcqk,bckhd->bcqhd', p, vb).reshape(b, s, D_MODEL)
    return o @ w_o


def _complex_affine_combine(e1, e2):
    a1r, a1i, b1r, b1i = e1
    a2r, a2i, b2r, b2i = e2
    ar = a2r * a1r - a2i * a1i
    ai = a2r * a1i + a2i * a1r
    br = a2r * b1r - a2i * b1i + b2r
    bi = a2r * b1i + a2i * b1r + b2i
    return (ar, ai, br, bi)


def s5_mixer(h, lambda_re, lambda_im, log_step, b_re, b_im, c_re, c_im, d_skip, w_out):
    bsz, s, _ = h.shape
    h32 = h.astype(jnp.float32)
    u = h32.reshape(bsz, s, SSM_GROUPS, SSM_GROUP)
    lam_re = jnp.minimum(lambda_re.astype(jnp.float32), -1e-4)
    lam_im = lambda_im.astype(jnp.float32)
    step = jnp.exp(log_step.astype(jnp.float32))[:, None]
    decay = jnp.exp(lam_re * step)
    ang = lam_im * step
    abar_re = decay * jnp.cos(ang)
    abar_im = decay * jnp.sin(ang)
    denom = lam_re * lam_re + lam_im * lam_im
    num_re = abar_re - 1.0
    f_re = ((num_re * lam_re + abar_im * lam_im) / denom)[..., None]
    f_im = ((abar_im * lam_re - num_re * lam_im) / denom)[..., None]
    br = b_re.astype(jnp.float32)
    bi = b_im.astype(jnp.float32)
    bbar_re = f_re * br - f_im * bi
    bbar_im = f_re * bi + f_im * br
    bu_re = jnp.einsum('blgc,gpc->blgp', u, bbar_re)
    bu_im = jnp.einsum('blgc,gpc->blgp', u, bbar_im)
    a_re = jnp.broadcast_to(abar_re, bu_re.shape)
    a_im = jnp.broadcast_to(abar_im, bu_im.shape)
    _, _, x_re, x_im = lax.associative_scan(_complex_affine_combine, (a_re, a_im, bu_re, bu_im), axis=1)
    y = (jnp.einsum('blgp,gcp->blgc', x_re, c_re.astype(jnp.float32))
         - jnp.einsum('blgp,gcp->blgc', x_im, c_im.astype(jnp.float32)))
    y = y.reshape(bsz, s, D_MODEL) + d_skip.astype(jnp.float32) * h32
    y = jax.nn.gelu(y).astype(h.dtype)
    a, g = jnp.split(y @ w_out, 2, axis=-1)
    return a * jax.nn.sigmoid(g)


def hierarchical_moe(h, w_group_router, b_group_router, w_expert_router, b_expert_router,
                     w_gate, w_up, w_down):
    bsz, s, d = h.shape
    t = h.reshape(-1, d)
    group_logits = (t @ w_group_router).astype(jnp.float32) + b_group_router.astype(jnp.float32)
    group_prob = jax.nn.softmax(group_logits, axis=-1)
    g_prob, g_idx = lax.top_k(group_prob, 1)
    exp_logits = (jnp.einsum('td,gde->tge', t, w_expert_router).astype(jnp.float32)
                  + b_expert_router.astype(jnp.float32))
    sel_logits = jnp.take_along_axis(exp_logits, g_idx[:, :, None], axis=1)[:, 0]
    top_logits, e_idx = lax.top_k(sel_logits, TOP_K)
    e_w = jax.nn.softmax(top_logits, axis=-1) * g_prob
    global_idx = g_idx * EXPERTS_PER_GROUP + e_idx
    gates = jnp.sum(jax.nn.one_hot(global_idx, N_EXPERTS, dtype=jnp.float32) * e_w[..., None], axis=1)
    out = jnp.zeros(t.shape, jnp.float32)
    for e in range(N_EXPERTS):
        hid = jax.nn.silu(t @ w_gate[e]) * (t @ w_up[e])
        out = out + gates[:, e:e + 1] * (hid @ w_down[e]).astype(jnp.float32)
    return out.astype(h.dtype).reshape(bsz, s, d)


def setup_inputs(seed: int = 0) -> dict:
    key = jax.random.key(seed)
    ks = jax.random.split(key, 24)
    n_attn = (DEPTH + 1) // 2
    n_ssm = DEPTH // 2
    D, F = D_MODEL, D_EXPERT
    nrm = jax.random.normal
    f32 = jnp.float32
    log_step = jax.random.uniform(ks[5], (n_ssm, SSM_GROUPS), f32,
                                  float(np.log(DT_MIN)), float(np.log(DT_MAX)))
    lam_im_base = jnp.pi * jnp.arange(SSM_STATE, dtype=f32)
    return {
        "x": nrm(ks[0], (BATCH, SEQ, D), f32),
        "norm_mix": 1.0 + 0.02 * nrm(ks[1], (DEPTH, D), f32),
        "norm_ffn": 1.0 + 0.02 * nrm(ks[2], (DEPTH, D), f32),
        "norm_final": 1.0 + 0.02 * nrm(ks[3], (D,), f32),
        "attn_w_qkv": nrm(ks[4], (n_attn, D, 3 * D), f32) * D ** -0.5,
        "attn_w_o": nrm(ks[6], (n_attn, D, D), f32) * D ** -0.5,
        "attn_rel_bias": 0.1 * nrm(ks[7], (n_attn, N_HEADS, N_REL), f32),
        "ssm_lambda_re": -0.5 + 0.01 * nrm(ks[8], (n_ssm, SSM_GROUPS, SSM_STATE), f32),
        "ssm_lambda_im": lam_im_base + 0.01 * nrm(ks[9], (n_ssm, SSM_GROUPS, SSM_STATE), f32),
        "ssm_log_step": log_step,
        "ssm_b_re": nrm(ks[10], (n_ssm, SSM_GROUPS, SSM_STATE, SSM_GROUP), f32) * (2 * SSM_GROUP) ** -0.5,
        "ssm_b_im": nrm(ks[11], (n_ssm, SSM_GROUPS, SSM_STATE, SSM_GROUP), f32) * (2 * SSM_GROUP) ** -0.5,
        "ssm_c_re": nrm(ks[12], (n_ssm, SSM_GROUPS, SSM_GROUP, SSM_STATE), f32) * SSM_STATE ** -0.5,
        "ssm_c_im": nrm(ks[13], (n_ssm, SSM_GROUPS, SSM_GROUP, SSM_STATE), f32) * SSM_STATE ** -0.5,
        "ssm_d": nrm(ks[14], (n_ssm, D), f32),
        "ssm_w_out": nrm(ks[15], (n_ssm, D, 2 * D), f32) * D ** -0.5,
        "moe_w_group_router": nrm(ks[16], (DEPTH, D, N_EXPERT_GROUPS), f32) * D ** -0.5,
        "moe_b_group_router": 0.01 * nrm(ks[17], (DEPTH, N_EXPERT_GROUPS), f32),
        "moe_w_expert_router": nrm(ks[18], (DEPTH, N_EXPERT_GROUPS, D, EXPERTS_PER_GROUP), f32) * D ** -0.5,
        "moe_b_expert_router": 0.01 * nrm(ks[19], (DEPTH, N_EXPERT_GROUPS, EXPERTS_PER_GROUP), f32),
        "moe_w_gate": nrm(ks[20], (DEPTH, N_EXPERTS, D, F), f32) * D ** -0.5,
        "moe_w_up": nrm(ks[21], (DEPTH, N_EXPERTS, D, F), f32) * D ** -0.5,
        "moe_w_down": nrm(ks[22], (DEPTH, N_EXPERTS, F, D), f32) * F ** -0.5,
    }


def reference(x, norm_mix, norm_ffn, norm_final, attn_w_qkv, attn_w_o, attn_rel_bias,
              ssm_lambda_re, ssm_lambda_im, ssm_log_step, ssm_b_re, ssm_b_im, ssm_c_re, ssm_c_im,
              ssm_d, ssm_w_out, moe_w_group_router, moe_b_group_router, moe_w_expert_router,
              moe_b_expert_router, moe_w_gate, moe_w_up, moe_w_down):
    h = x
    for i in range(DEPTH):
        j = i // N_MIXERS
        hn = rms_norm(h, norm_mix[i])
        if i % N_MIXERS == 0:
            mix = chunked_rel_attention(hn, attn_w_qkv[j], attn_w_o[j], attn_rel_bias[j])
        else:
            mix = s5_mixer(hn, ssm_lambda_re[j], ssm_lambda_im[j], ssm_log_step[j],
                           ssm_b_re[j], ssm_b_im[j], ssm_c_re[j], ssm_c_im[j],
                           ssm_d[j], ssm_w_out[j])
        h = h + mix
        hn = rms_norm(h, norm_ffn[i])
        h = h + hierarchical_moe(hn, moe_w_group_router[i], moe_b_group_router[i],
                                 moe_w_expert_router[i], moe_b_expert_router[i],
                                 moe_w_gate[i], moe_w_up[i], moe_w_down[i])
    return rms_norm(h, norm_final)
```

```python
import functools

import jax
import jax.numpy as jnp
import numpy as np
from jax import lax
from jax.experimental import pallas as pl
from jax.experimental.pallas import tpu as pltpu

F32 = jnp.float32
BF16 = jnp.bfloat16

D_MODEL = 1024
CHUNK = 64
LOOKBACK_CHUNKS = 8
BAND = (LOOKBACK_CHUNKS + 1) * CHUNK
N_HEADS = 16
HEAD_DIM = D_MODEL // N_HEADS
MAX_REL = 128
SSM_GROUP = 16
SSM_GROUPS = D_MODEL // SSM_GROUP
SSM_STATE = 64
N_EXPERT_GROUPS = 4
EXPERTS_PER_GROUP = 8
N_EXPERTS = N_EXPERT_GROUPS * EXPERTS_PER_GROUP
D_EXPERT = D_MODEL // 4
RMS_EPS = 1e-6
NEG_BIG = -1e30

LANES = 128
MXU_DIM = 256
VMEM_LIMIT = 56 * 1024 * 1024

HEADS_PER_GROUP = MXU_DIM // HEAD_DIM
N_HEAD_GROUPS = N_HEADS // HEADS_PER_GROUP
ATTN_QBLOCK = LOOKBACK_CHUNKS * CHUNK
ROUTER_LANES = LANES
ROUTER_EXPERT_LANE0 = N_EXPERT_GROUPS
SSM_CHUNK = 16
SSM_LANE_GROUPS = LANES // SSM_GROUP
SSM_BLOCKS = D_MODEL // LANES
SSM_BLOCK_STATE = SSM_LANE_GROUPS * SSM_STATE


def _dot(a, b):
    return jnp.dot(a, b, preferred_element_type=F32)


def _rms(x, g):
    return x * lax.rsqrt(jnp.mean(x * x, axis=-1, keepdims=True) + RMS_EPS) * g


def _sigmoid(x):
    return 1.0 / (1.0 + jnp.exp(-x))


def _gelu_tanh(x):
    c = np.float32(np.sqrt(2.0 / np.pi))
    return 0.5 * x * (1.0 + jnp.tanh(c * (x + np.float32(0.044715) * (x * x * x))))


def _params(*sem):
    return pltpu.CompilerParams(dimension_semantics=sem, vmem_limit_bytes=VMEM_LIMIT)


def _qkv_kernel(x_ref, g_ref, w_ref, o_ref):
    xn = _rms(x_ref[...], g_ref[...]).astype(BF16)
    for c in range(3):
        acc = _dot(xn, w_ref[:, c * D_MODEL:(c + 1) * D_MODEL])
        if c == 0:
            acc = acc * np.float32(HEAD_DIM ** -0.5)
        o_ref[:, c * D_MODEL:(c + 1) * D_MODEL] = acc.astype(BF16)


def _qkv(x, g, w, tm=512):
    t = x.shape[0]
    return pl.pallas_call(
        _qkv_kernel,
        grid=(t // tm,),
        in_specs=[pl.BlockSpec((tm, D_MODEL), lambda i: (i, 0)),
                  pl.BlockSpec((1, D_MODEL), lambda i: (0, 0)),
                  pl.BlockSpec((D_MODEL, 3 * D_MODEL), lambda i: (0, 0))],
        out_specs=pl.BlockSpec((tm, 3 * D_MODEL), lambda i: (i, 0)),
        out_shape=jax.ShapeDtypeStruct((t, 3 * D_MODEL), BF16),
        compiler_params=_params("parallel"),
        name="qkv",
    )(x, g, w)


def _attn_kernel(q_ref, kp_ref, kc_ref, vp_ref, vc_ref, bias_ref, o_ref, kk, vv):
    qb = pl.program_id(2)
    kk[0:ATTN_QBLOCK, :] = kp_ref[...]
    kk[ATTN_QBLOCK:, :] = kc_ref[...]
    vv[0:ATTN_QBLOCK, :] = vp_ref[...]
    vv[ATTN_QBLOCK:, :] = vc_ref[...]
    lane_head = lax.broadcasted_iota(jnp.int32, (CHUNK, MXU_DIM), 1) // HEAD_DIM
    col = lax.broadcasted_iota(jnp.int32, (1, BAND), 1)
    for j in range(ATTN_QBLOCK // CHUNK):
        qj = q_ref[j * CHUNK:(j + 1) * CHUNK, :]
        lhs = jnp.concatenate(
            [jnp.where(lane_head == h, qj, jnp.zeros_like(qj)) for h in range(HEADS_PER_GROUP)], axis=0)
        kwin = kk[j * CHUNK:j * CHUNK + BAND, :]
        vwin = vv[j * CHUNK:j * CHUNK + BAND, :]
        s = lax.dot_general(lhs, kwin, (((1,), (1,)), ((), ())), preferred_element_type=F32)
        valid = (col + j * CHUNK >= ATTN_QBLOCK) | (qb > 0)
        s = jnp.where(valid, s + bias_ref[0], NEG_BIG)
        m = jnp.max(s, axis=-1, keepdims=True)
        p = jnp.exp(s - m)
        l = jnp.sum(p, axis=-1, keepdims=True)
        o_all = _dot(p.astype(BF16), vwin) * (1.0 / l)
        o = jnp.zeros((CHUNK, MXU_DIM), F32)
        for h in range(HEADS_PER_GROUP):
            o = o + jnp.where(lane_head == h, o_all[h * CHUNK:(h + 1) * CHUNK, :], 0.0)
        o_ref[j * CHUNK:(j + 1) * CHUNK, :] = o.astype(BF16)


def _attention(qkv, bias, batch, seq):
    nqb = seq // ATTN_QBLOCK
    kcol = D_MODEL // MXU_DIM
    blk = (ATTN_QBLOCK, MXU_DIM)
    cur = lambda off: (lambda b, g, i: (b * nqb + i, off + g))
    prev = lambda off: (lambda b, g, i: (b * nqb + jnp.maximum(i - 1, 0), off + g))
    return pl.pallas_call(
        _attn_kernel,
        grid=(batch, N_HEAD_GROUPS, nqb),
        in_specs=[pl.BlockSpec(blk, cur(0)),
                  pl.BlockSpec(blk, prev(kcol)), pl.BlockSpec(blk, cur(kcol)),
                  pl.BlockSpec(blk, prev(2 * kcol)), pl.BlockSpec(blk, cur(2 * kcol)),
                  pl.BlockSpec((1, MXU_DIM, BAND), lambda b, g, i: (g, 0, 0))],
        out_specs=pl.BlockSpec(blk, cur(0)),
        out_shape=jax.ShapeDtypeStruct((batch * seq, D_MODEL), BF16),
        scratch_shapes=[pltpu.VMEM((2 * ATTN_QBLOCK, MXU_DIM), BF16),
                        pltpu.VMEM((2 * ATTN_QBLOCK, MXU_DIM), BF16)],
        compiler_params=_params("parallel", "parallel", "arbitrary"),
        name="attn",
    )(qkv, qkv, qkv, qkv, qkv, bias)


def _rel_bias_table(rel_bias):
    q_pos = LOOKBACK_CHUNKS * CHUNK + np.arange(CHUNK)
    dist = q_pos[:, None] - np.arange(BAND)[None, :]
    rel_idx = np.clip(dist, -MAX_REL, MAX_REL) + MAX_REL
    bias = rel_bias.astype(F32)[:, rel_idx]
    return bias.reshape(N_HEAD_GROUPS, HEADS_PER_GROUP * CHUNK, BAND)


def _route(hn, wr_ref, br_ref):
    a_hi = hn.astype(BF16)
    a_lo = (hn - a_hi.astype(F32)).astype(BF16)
    w = wr_ref[...]
    w_hi = w.astype(BF16)
    w_lo = (w - w_hi.astype(F32)).astype(BF16)
    lg = _dot(a_hi, w_hi) + (_dot(a_hi, w_lo) + _dot(a_lo, w_hi)) + br_ref[...]
    lane = lax.broadcasted_iota(jnp.int32, lg.shape, 1)
    ninf = np.float32(-np.inf)
    big = np.int32(ROUTER_LANES)
    gmask = lane < N_EXPERT_GROUPS
    gl = jnp.where(gmask, lg, ninf)
    gmax = jnp.max(gl, axis=-1, keepdims=True)
    gidx = jnp.min(jnp.where(gl == gmax, lane, big), axis=-1, keepdims=True)
    gprob = 1.0 / jnp.sum(jnp.where(gmask, jnp.exp(lg - gmax), 0.0), axis=-1, keepdims=True)
    lo = ROUTER_EXPERT_LANE0 + EXPERTS_PER_GROUP * gidx
    el = jnp.where((lane >= lo) & (lane < lo + EXPERTS_PER_GROUP), lg, ninf)
    l1 = jnp.max(el, axis=-1, keepdims=True)
    i1 = jnp.min(jnp.where(el == l1, lane, big), axis=-1, keepdims=True)
    el2 = jnp.where(lane == i1, ninf, el)
    l2 = jnp.max(el2, axis=-1, keepdims=True)
    i2 = jnp.min(jnp.where(el2 == l2, lane, big), axis=-1, keepdims=True)
    t = jnp.exp(l2 - l1)
    w1 = gprob / (1.0 + t)
    w2 = w1 * t
    return jnp.where(lane == i1, w1, 0.0) + jnp.where(lane == i2, w2, 0.0)


def _post_kernel(glu, y_ref, h_ref, w_ref, g_ref, wr_ref, br_ref, hout_ref, hn_ref, gates_ref):
    y = y_ref[...].astype(BF16)
    if glu:
        mix = _dot(y, w_ref[:, :D_MODEL]) * _sigmoid(_dot(y, w_ref[:, D_MODEL:]))
    else:
        mix = _dot(y, w_ref[...])
    h = h_ref[...] + mix
    hout_ref[...] = h
    hn = _rms(h, g_ref[...])
    hn_ref[...] = hn.astype(BF16)
    gates_ref[...] = _route(hn, wr_ref, br_ref)


def _post(y, h, w, g, wr, br, glu, tm=512):
    t = h.shape[0]
    row = lambda i: (i, 0)
    fixed = lambda i: (0, 0)
    return pl.pallas_call(
        functools.partial(_post_kernel, glu),
        grid=(t // tm,),
        in_specs=[pl.BlockSpec((tm, D_MODEL), row), pl.BlockSpec((tm, D_MODEL), row),
                  pl.BlockSpec(w.shape, fixed), pl.BlockSpec((1, D_MODEL), fixed),
                  pl.BlockSpec((D_MODEL, ROUTER_LANES), fixed), pl.BlockSpec((1, ROUTER_LANES), fixed)],
        out_specs=[pl.BlockSpec((tm, D_MODEL), row), pl.BlockSpec((tm, D_MODEL), row),
                   pl.BlockSpec((tm, ROUTER_LANES), row)],
        out_shape=[jax.ShapeDtypeStruct((t, D_MODEL), F32), jax.ShapeDtypeStruct((t, D_MODEL), BF16),
                   jax.ShapeDtypeStruct((t, ROUTER_LANES), F32)],
        compiler_params=_params("parallel"),
        name="post_glu" if glu else "post_attn",
    )(y, h, w, g, wr, br)


def _moe_kernel(hn_ref, gates_ref, h_ref, wgu_ref, wd_ref, gn_ref, hout_ref, hnorm_ref, acc):
    e = pl.program_id(1)

    @pl.when(e == 0)
    def _():
        acc[...] = h_ref[...]

    gates = gates_ref[...]
    lane = lax.broadcasted_iota(jnp.int32, gates.shape, 1)
    gcol = jnp.sum(jnp.where(lane == e + ROUTER_EXPERT_LANE0, gates, 0.0), axis=-1, keepdims=True)
    au = _dot(hn_ref[...], wgu_ref[0])
    a = au[:, :D_EXPERT]
    u = au[:, D_EXPERT:]
    hid = (a * _sigmoid(a)) * u * gcol
    acc[...] += _dot(hid.astype(BF16), wd_ref[0])

    @pl.when(e == N_EXPERTS - 1)
    def _():
        h = acc[...]
        hout_ref[...] = h
        hnorm_ref[...] = _rms(h, gn_ref[...])


def _moe(hn, gates, h, wgu, wd, gnext, tm=1024):
    t = h.shape[0]
    row = lambda i, e: (i, 0)
    return pl.pallas_call(
        _moe_kernel,
        grid=(t // tm, N_EXPERTS),
        in_specs=[pl.BlockSpec((tm, D_MODEL), row), pl.BlockSpec((tm, ROUTER_LANES), row),
                  pl.BlockSpec((tm, D_MODEL), row),
                  pl.BlockSpec((1, D_MODEL, 2 * D_EXPERT), lambda i, e: (e, 0, 0)),
                  pl.BlockSpec((1, D_EXPERT, D_MODEL), lambda i, e: (e, 0, 0)),
                  pl.BlockSpec((1, D_MODEL), lambda i, e: (0, 0))],
        out_specs=[pl.BlockSpec((tm, D_MODEL), row), pl.BlockSpec((tm, D_MODEL), row)],
        out_shape=[jax.ShapeDtypeStruct((t, D_MODEL), F32), jax.ShapeDtypeStruct((t, D_MODEL), F32)],
        scratch_shapes=[pltpu.VMEM((tm, D_MODEL), F32)],
        compiler_params=_params("parallel", "arbitrary"),
        name="moe",
    )(hn, gates, h, wgu, wd, gnext)


def _ssm_kernel(nch, x_ref, r_ref, wd_ref, vd_ref, a_ref, d_ref, o_ref, xcat, st, xprev):
    nb = x_ref.shape[0]
    ns = SSM_BLOCK_STATE
    for s in range(SSM_CHUNK):
        for b in range(nb):
            xs = x_ref[b, pl.ds(s, nch, stride=SSM_CHUNK), :]
            xcat[b * nch:(b + 1) * nch, s * LANES:(s + 1) * LANES] = xs.astype(BF16)
    st[...] = _dot(xcat[...], wd_ref[0])
    ar = a_ref[0, :, 0:ns]
    ai = a_ref[0, :, ns:2 * ns]

    def step(k, carry):
        out = []
        for b in range(nb):
            xr, xi = carry[b]
            row = b * nch + k
            sr = st[pl.ds(row, 1), 0:ns]
            si = st[pl.ds(row, 1), ns:2 * ns]
            st[pl.ds(row, 1), 0:ns] = xr
            st[pl.ds(row, 1), ns:2 * ns] = xi
            out.append((ar * xr - ai * xi + sr, ar * xi + ai * xr + si))
        return tuple(out)

    zero = jnp.zeros((1, ns), F32)
    lax.fori_loop(0, nch, step, tuple((zero, zero) for _ in range(nb)))
    xprev[...] = st[...].astype(BF16)
    dskip = d_ref[...]
    for t in range(SSM_CHUNK):
        y = (_dot(xcat[:, :(t + 1) * LANES], r_ref[0, (SSM_CHUNK - 1 - t) * LANES:, :])
             + _dot(xprev[...], vd_ref[0, :, t * LANES:(t + 1) * LANES]))
        for b in range(nb):
            u = x_ref[b, pl.ds(t, nch, stride=SSM_CHUNK), :]
            o_ref[b, pl.ds(t, nch, stride=SSM_CHUNK), :] = _gelu_tanh(y[b * nch:(b + 1) * nch, :] + dskip * u)


def _ssm(x3, r, wd, vd, a16, dskip):
    nb, seq, _ = x3.shape
    nch = seq // SSM_CHUNK
    rows = nb * nch
    blk = pl.BlockSpec((nb, seq, LANES), lambda g: (0, 0, g), pipeline_mode=pl.Buffered(1))
    per = lambda shape: pl.BlockSpec((1,) + shape, lambda g: (g, 0, 0))
    return pl.pallas_call(
        functools.partial(_ssm_kernel, nch),
        grid=(SSM_BLOCKS,),
        in_specs=[blk, per((SSM_CHUNK * LANES, LANES)), per((SSM_CHUNK * LANES, 2 * SSM_BLOCK_STATE)),
                  per((2 * SSM_BLOCK_STATE, SSM_CHUNK * LANES)), per((1, 2 * SSM_BLOCK_STATE)),
                  pl.BlockSpec((1, LANES), lambda g: (0, g))],
        out_specs=blk,
        out_shape=jax.ShapeDtypeStruct(x3.shape, F32),
        scratch_shapes=[pltpu.VMEM((rows, SSM_CHUNK * LANES), BF16),
                        pltpu.VMEM((rows, 2 * SSM_BLOCK_STATE), F32),
                        pltpu.VMEM((rows, 2 * SSM_BLOCK_STATE), BF16)],
        compiler_params=_params("parallel"),
        name="ssm",
    )(x3, r, wd, vd, a16, dskip)


def _ssm_tables(lambda_re, lambda_im, log_step, b_re, b_im, c_re, c_im):
    hp = lax.Precision.HIGHEST
    lam_re = jnp.minimum(lambda_re.astype(F32), -1e-4)
    lam_im = lambda_im.astype(F32)
    step = jnp.exp(log_step.astype(F32))[:, None]
    L = SSM_CHUNK

    def powers(lags):
        lags = jnp.asarray(lags, F32)[:, None, None]
        dec = jnp.exp(lags * (lam_re * step))
        ang = lags * (lam_im * step)
        return dec * jnp.cos(ang), dec * jnp.sin(ang)

    pr_re, pr_im = powers(np.arange(L - 1, -1, -1))
    p1_re, p1_im = powers(np.arange(1, L + 1))
    abar_re, abar_im = p1_re[0], p1_im[0]
    denom = lam_re * lam_re + lam_im * lam_im
    num_re = abar_re - 1.0
    f_re = ((num_re * lam_re + abar_im * lam_im) / denom)[..., None]
    f_im = ((abar_im * lam_re - num_re * lam_im) / denom)[..., None]
    br, bi = b_re.astype(F32), b_im.astype(F32)
    bb_re = f_re * br - f_im * bi
    bb_im = f_re * bi + f_im * br
    cr, ci = c_re.astype(F32), c_im.astype(F32)
    ca_re = cr[None] * pr_re[:, :, None, :] - ci[None] * pr_im[:, :, None, :]
    ca_im = cr[None] * pr_im[:, :, None, :] + ci[None] * pr_re[:, :, None, :]
    taps = (jnp.einsum('lgop,gpi->lgio', ca_re, bb_re, precision=hp)
            - jnp.einsum('lgop,gpi->lgio', ca_im, bb_im, precision=hp))
    nb, ng = SSM_BLOCKS, SSM_LANE_GROUPS
    eye = jnp.eye(ng, dtype=F32)

    def blockdiag(m):
        lead = m.shape[:-3]
        a, b = m.shape[-2:]
        m = m.reshape(lead + (nb, ng, a, b))
        out = jnp.einsum('...ngab,gh->...ngahb', m, eye)
        return out.reshape(lead + (nb, ng * a, ng * b))

    d_blk = blockdiag(taps)
    r = jnp.transpose(d_blk, (1, 0, 2, 3)).reshape(nb, L * LANES, LANES)
    w_re = pr_re[..., None] * bb_re[None] - pr_im[..., None] * bb_im[None]
    w_im = pr_re[..., None] * bb_im[None] + pr_im[..., None] * bb_re[None]
    w_re = blockdiag(jnp.swapaxes(w_re, -1, -2))
    w_im = blockdiag(jnp.swapaxes(w_im, -1, -2))
    wd = jnp.concatenate([w_re, w_im], axis=-1)
    wd = jnp.transpose(wd, (1, 0, 2, 3)).reshape(nb, L * LANES, 2 * SSM_BLOCK_STATE)
    ca1_re = cr[None] * p1_re[:, :, None, :] - ci[None] * p1_im[:, :, None, :]
    ca1_im = cr[None] * p1_im[:, :, None, :] + ci[None] * p1_re[:, :, None, :]
    v_re = blockdiag(jnp.swapaxes(ca1_re, -1, -2))
    v_im = blockdiag(jnp.swapaxes(-ca1_im, -1, -2))
    vd = jnp.concatenate([v_re, v_im], axis=-2)
    vd = jnp.transpose(vd, (1, 2, 0, 3)).reshape(nb, 2 * SSM_BLOCK_STATE, L * LANES)
    a_l = jnp.concatenate([p1_re[L - 1].reshape(nb, 1, SSM_BLOCK_STATE),
                           p1_im[L - 1].reshape(nb, 1, SSM_BLOCK_STATE)], axis=-1)
    return r.astype(BF16), wd.astype(BF16), vd.astype(BF16), a_l


def _router_tables(w_group, b_group, w_expert, b_expert):
    w = jnp.concatenate([w_group.astype(F32),
                         jnp.transpose(w_expert.astype(F32), (1, 0, 2)).reshape(D_MODEL, N_EXPERTS)], axis=1)
    b = jnp.concatenate([b_group.astype(F32), b_expert.astype(F32).reshape(N_EXPERTS)])
    pad = ROUTER_LANES - w.shape[1]
    return jnp.pad(w, ((0, 0), (0, pad))), jnp.pad(b, (0, pad)).reshape(1, ROUTER_LANES)


def _expert_tables(w_gate, w_up, w_down):
    return jnp.concatenate([w_gate, w_up], axis=-1).astype(BF16), w_down.astype(BF16)


def kernel(x, norm_mix, norm_ffn, norm_final, attn_w_qkv, attn_w_o, attn_rel_bias, ssm_lambda_re, ssm_lambda_im, ssm_log_step, ssm_b_re, ssm_b_im, ssm_c_re, ssm_c_im, ssm_d, ssm_w_out, moe_w_group_router, moe_b_group_router, moe_w_expert_router, moe_b_expert_router, moe_w_gate, moe_w_up, moe_w_down):
    batch, seq, d = x.shape
    t = batch * seq
    row = lambda v: v.astype(F32).reshape(1, d)
    h = x.astype(F32).reshape(t, d)

    qkv = _qkv(h, row(norm_mix[0]), attn_w_qkv[0].astype(BF16))
    o = _attention(qkv, _rel_bias_table(attn_rel_bias[0]), batch, seq)
    wr, br = _router_tables(moe_w_group_router[0], moe_b_group_router[0],
                            moe_w_expert_router[0], moe_b_expert_router[0])
    h, hn, gates = _post(o, h, attn_w_o[0].astype(BF16), row(norm_ffn[0]), wr, br, glu=False)
    wgu, wdn = _expert_tables(moe_w_gate[0], moe_w_up[0], moe_w_down[0])
    h, hn32 = _moe(hn, gates, h, wgu, wdn, row(norm_mix[1]))

    r, wd, vd, a_l = _ssm_tables(ssm_lambda_re[0], ssm_lambda_im[0], ssm_log_step[0],
                                 ssm_b_re[0], ssm_b_im[0], ssm_c_re[0], ssm_c_im[0])
    y = _ssm(hn32.reshape(batch, seq, d), r, wd, vd, a_l, row(ssm_d[0]))
    wr, br = _router_tables(moe_w_group_router[1], moe_b_group_router[1],
                            moe_w_expert_router[1], moe_b_expert_router[1])
    h, hn, gates = _post(y.reshape(t, d), h, ssm_w_out[0].astype(BF16), row(norm_ffn[1]), wr, br, glu=True)
    wgu, wdn = _expert_tables(moe_w_gate[1], moe_w_up[1], moe_w_down[1])
    _, out = _moe(hn, gates, h, wgu, wdn, row(norm_final))
    return out.reshape(batch, seq, d).astype(x.dtype)
```

```python
import functools

import jax
import jax.numpy as jnp
import numpy as np
from jax import lax
from jax.experimental import pallas as pl
from jax.experimental.pallas import tpu as pltpu

F32 = jnp.float32
BF16 = jnp.bfloat16

D_MODEL = 1024
CHUNK = 64
LOOKBACK_CHUNKS = 8
BAND = (LOOKBACK_CHUNKS + 1) * CHUNK
N_HEADS = 16
HEAD_DIM = D_MODEL // N_HEADS
MAX_REL = 128
SSM_GROUP = 16
SSM_GROUPS = D_MODEL // SSM_GROUP
SSM_STATE = 64
N_EXPERT_GROUPS = 4
EXPERTS_PER_GROUP = 8
N_EXPERTS = N_EXPERT_GROUPS * EXPERTS_PER_GROUP
D_EXPERT = D_MODEL // 4
RMS_EPS = 1e-6
NEG_BIG = -1e30

LANES = 128
MXU_DIM = 256
VMEM_LIMIT = 56 * 1024 * 1024

HEADS_PER_GROUP = MXU_DIM // HEAD_DIM
N_HEAD_GROUPS = N_HEADS // HEADS_PER_GROUP
ATTN_QBLOCK = LOOKBACK_CHUNKS * CHUNK
ROUTER_LANES = LANES
ROUTER_EXPERT_LANE0 = N_EXPERT_GROUPS
SSM_CHUNK = 16
SSM_LANE_GROUPS = LANES // SSM_GROUP
SSM_BLOCKS = D_MODEL // LANES
SSM_BLOCK_STATE = SSM_LANE_GROUPS * SSM_STATE
TOKEN_ROWS = D_MODEL // LANES
MOE_TILE = 256
R_E1, R_E2, R_W1, R_W2, R_RANK1, R_RANK2 = range(6)


def _dot(a, b):
    return jnp.dot(a, b, preferred_element_type=F32)


def _rms(x, g):
    return x * lax.rsqrt(jnp.mean(x * x, axis=-1, keepdims=True) + RMS_EPS) * g


def _sigmoid(x):
    return 1.0 / (1.0 + jnp.exp(-x))


def _gelu_tanh(x):
    c = np.float32(np.sqrt(2.0 / np.pi))
    return 0.5 * x * (1.0 + jnp.tanh(c * (x + np.float32(0.044715) * (x * x * x))))


def _params(*sem):
    return pltpu.CompilerParams(dimension_semantics=sem, vmem_limit_bytes=VMEM_LIMIT)


def _qkv_kernel(x_ref, g_ref, w_ref, o_ref):
    xn = _rms(x_ref[...], g_ref[...]).astype(BF16)
    for c in range(3):
        acc = _dot(xn, w_ref[:, c * D_MODEL:(c + 1) * D_MODEL])
        if c == 0:
            acc = acc * np.float32(HEAD_DIM ** -0.5)
        o_ref[:, c * D_MODEL:(c + 1) * D_MODEL] = acc.astype(BF16)


def _qkv(x, g, w, tm=512):
    t = x.shape[0]
    return pl.pallas_call(
        _qkv_kernel,
        grid=(t // tm,),
        in_specs=[pl.BlockSpec((tm, D_MODEL), lambda i: (i, 0)),
                  pl.BlockSpec((1, D_MODEL), lambda i: (0, 0)),
                  pl.BlockSpec((D_MODEL, 3 * D_MODEL), lambda i: (0, 0))],
        out_specs=pl.BlockSpec((tm, 3 * D_MODEL), lambda i: (i, 0)),
        out_shape=jax.ShapeDtypeStruct((t, 3 * D_MODEL), BF16),
        compiler_params=_params("parallel"),
        name="qkv",
    )(x, g, w)


def _attn_kernel(q_ref, kp_ref, kc_ref, vp_ref, vc_ref, bias_ref, o_ref, kk, vv):
    qb = pl.program_id(2)
    kk[0:ATTN_QBLOCK, :] = kp_ref[...]
    kk[ATTN_QBLOCK:, :] = kc_ref[...]
    vv[0:ATTN_QBLOCK, :] = vp_ref[...]
    vv[ATTN_QBLOCK:, :] = vc_ref[...]
    lane_head = lax.broadcasted_iota(jnp.int32, (CHUNK, MXU_DIM), 1) // HEAD_DIM
    col = lax.broadcasted_iota(jnp.int32, (1, BAND), 1)
    for j in range(ATTN_QBLOCK // CHUNK):
        qj = q_ref[j * CHUNK:(j + 1) * CHUNK, :]
        lhs = jnp.concatenate(
            [jnp.where(lane_head == h, qj, jnp.zeros_like(qj)) for h in range(HEADS_PER_GROUP)], axis=0)
        kwin = kk[j * CHUNK:j * CHUNK + BAND, :]
        vwin = vv[j * CHUNK:j * CHUNK + BAND, :]
        s = lax.dot_general(lhs, kwin, (((1,), (1,)), ((), ())), preferred_element_type=F32)
        valid = (col + j * CHUNK >= ATTN_QBLOCK) | (qb > 0)
        s = jnp.where(valid, s + bias_ref[0], NEG_BIG)
        m = jnp.max(s, axis=-1, keepdims=True)
        p = jnp.exp(s - m)
        l = jnp.sum(p, axis=-1, keepdims=True)
        o_all = _dot(p.astype(BF16), vwin) * (1.0 / l)
        o = jnp.zeros((CHUNK, MXU_DIM), F32)
        for h in range(HEADS_PER_GROUP):
            o = o + jnp.where(lane_head == h, o_all[h * CHUNK:(h + 1) * CHUNK, :], 0.0)
        o_ref[j * CHUNK:(j + 1) * CHUNK, :] = o.astype(BF16)


def _attention(qkv, bias, batch, seq):
    nqb = seq // ATTN_QBLOCK
    kcol = D_MODEL // MXU_DIM
    blk = (ATTN_QBLOCK, MXU_DIM)
    cur = lambda off: (lambda b, g, i: (b * nqb + i, off + g))
    prev = lambda off: (lambda b, g, i: (b * nqb + jnp.maximum(i - 1, 0), off + g))
    return pl.pallas_call(
        _attn_kernel,
        grid=(batch, N_HEAD_GROUPS, nqb),
        in_specs=[pl.BlockSpec(blk, cur(0)),
                  pl.BlockSpec(blk, prev(kcol)), pl.BlockSpec(blk, cur(kcol)),
                  pl.BlockSpec(blk, prev(2 * kcol)), pl.BlockSpec(blk, cur(2 * kcol)),
                  pl.BlockSpec((1, MXU_DIM, BAND), lambda b, g, i: (g, 0, 0))],
        out_specs=pl.BlockSpec(blk, cur(0)),
        out_shape=jax.ShapeDtypeStruct((batch * seq, D_MODEL), BF16),
        scratch_shapes=[pltpu.VMEM((2 * ATTN_QBLOCK, MXU_DIM), BF16),
                        pltpu.VMEM((2 * ATTN_QBLOCK, MXU_DIM), BF16)],
        compiler_params=_params("parallel", "parallel", "arbitrary"),
        name="attn",
    )(qkv, qkv, qkv, qkv, qkv, bias)


def _rel_bias_table(rel_bias):
    q_pos = LOOKBACK_CHUNKS * CHUNK + np.arange(CHUNK)
    dist = q_pos[:, None] - np.arange(BAND)[None, :]
    rel_idx = np.clip(dist, -MAX_REL, MAX_REL) + MAX_REL
    bias = rel_bias.astype(F32)[:, rel_idx]
    return bias.reshape(N_HEAD_GROUPS, HEADS_PER_GROUP * CHUNK, BAND)


def _route(hn, wr_ref, br_ref, base):
    a_hi = hn.astype(BF16)
    a_lo = (hn - a_hi.astype(F32)).astype(BF16)
    w = wr_ref[...]
    w_hi = w.astype(BF16)
    w_lo = (w - w_hi.astype(F32)).astype(BF16)
    lg = _dot(a_hi, w_hi) + (_dot(a_hi, w_lo) + _dot(a_lo, w_hi)) + br_ref[...]
    rows = lg.shape[0]
    lane = lax.broadcasted_iota(jnp.int32, lg.shape, 1)
    ninf = np.float32(-np.inf)
    big = np.int32(ROUTER_LANES)
    gmask = lane < N_EXPERT_GROUPS
    gl = jnp.where(gmask, lg, ninf)
    gmax = jnp.max(gl, axis=-1, keepdims=True)
    gidx = jnp.min(jnp.where(gl == gmax, lane, big), axis=-1, keepdims=True)
    gprob = 1.0 / jnp.sum(jnp.where(gmask, jnp.exp(lg - gmax), 0.0), axis=-1, keepdims=True)
    lo = ROUTER_EXPERT_LANE0 + EXPERTS_PER_GROUP * gidx
    el = jnp.where((lane >= lo) & (lane < lo + EXPERTS_PER_GROUP), lg, ninf)
    l1 = jnp.max(el, axis=-1, keepdims=True)
    i1 = jnp.min(jnp.where(el == l1, lane, big), axis=-1, keepdims=True)
    el2 = jnp.where(lane == i1, ninf, el)
    l2 = jnp.max(el2, axis=-1, keepdims=True)
    i2 = jnp.min(jnp.where(el2 == l2, lane, big), axis=-1, keepdims=True)
    t = jnp.exp(l2 - l1)
    w1 = gprob / (1.0 + t)
    w2 = w1 * t
    o1 = lane == i1
    o2 = lane == i2
    tri = (lax.broadcasted_iota(jnp.int32, (rows, rows), 0)
           > lax.broadcasted_iota(jnp.int32, (rows, rows), 1))
    tri = jnp.where(tri, 1.0, 0.0).astype(BF16)
    o1f = jnp.where(o1, 1.0, 0.0)
    o2f = jnp.where(o2, 1.0, 0.0)
    p1 = _dot(tri, o1f.astype(BF16))
    p2 = _dot(tri, o2f.astype(BF16))
    c1 = jnp.sum(o1f, axis=0, keepdims=True)
    c2 = jnp.sum(o2f, axis=0, keepdims=True)
    rank1 = jnp.sum(jnp.where(o1, base + p1, 0.0), axis=-1, keepdims=True)
    rank2 = jnp.sum(jnp.where(o2, base + c1 + p2, 0.0), axis=-1, keepdims=True)
    rec = jnp.zeros(lg.shape, F32)
    for ln, val in ((R_E1, (i1 - ROUTER_EXPERT_LANE0).astype(F32)), (R_E2, (i2 - ROUTER_EXPERT_LANE0).astype(F32)),
                    (R_W1, w1), (R_W2, w2), (R_RANK1, rank1), (R_RANK2, rank2)):
        rec = jnp.where(lane == ln, val, rec)
    return rec, base + c1 + c2


def _to_token_tiles(ref, x):
    rows = x.shape[0]
    for j in range(TOKEN_ROWS):
        ref[pl.ds(j, rows, stride=TOKEN_ROWS), :] = x[:, j * LANES:(j + 1) * LANES]


def _from_token_tiles(ref, rows):
    return jnp.concatenate([ref[pl.ds(j, rows, stride=TOKEN_ROWS), :] for j in range(TOKEN_ROWS)], axis=1)


def _post_kernel(glu, y_ref, h_ref, w_ref, g_ref, wr_ref, br_ref, hout_ref, hntt_ref, route_ref, cnt_ref):
    @pl.when(pl.program_id(0) == 0)
    def _():
        cnt_ref[...] = jnp.zeros_like(cnt_ref)

    y = y_ref[...].astype(BF16)
    if glu:
        mix = _dot(y, w_ref[:, :D_MODEL]) * _sigmoid(_dot(y, w_ref[:, D_MODEL:]))
    else:
        mix = _dot(y, w_ref[...])
    h = h_ref[...] + mix
    hout_ref[...] = h
    hn = _rms(h, g_ref[...])
    _to_token_tiles(hntt_ref, hn)
    route_ref[...], cnt_ref[...] = _route(hn, wr_ref, br_ref, cnt_ref[...])


def _post(y, h, w, g, wr, br, glu, tm=512):
    t = h.shape[0]
    row = lambda i: (i, 0)
    fixed = lambda i: (0, 0)
    return pl.pallas_call(
        functools.partial(_post_kernel, glu),
        grid=(t // tm,),
        in_specs=[pl.BlockSpec((tm, D_MODEL), row), pl.BlockSpec((tm, D_MODEL), row),
                  pl.BlockSpec(w.shape, fixed), pl.BlockSpec((1, D_MODEL), fixed),
                  pl.BlockSpec((D_MODEL, ROUTER_LANES), fixed), pl.BlockSpec((1, ROUTER_LANES), fixed)],
        out_specs=[pl.BlockSpec((tm, D_MODEL), row), pl.BlockSpec((tm * TOKEN_ROWS, LANES), row),
                   pl.BlockSpec((tm, ROUTER_LANES), row), pl.BlockSpec((1, ROUTER_LANES), fixed)],
        out_shape=[jax.ShapeDtypeStruct((t, D_MODEL), F32), jax.ShapeDtypeStruct((t * TOKEN_ROWS, LANES), F32),
                   jax.ShapeDtypeStruct((t, ROUTER_LANES), F32), jax.ShapeDtypeStruct((1, ROUTER_LANES), F32)],
        compiler_params=_params("arbitrary"),
        name="post_glu" if glu else "post_attn",
    )(y, h, w, g, wr, br)


def _moe_tiles(t):
    return (2 * t) // MOE_TILE + N_EXPERTS


def _dispatch_plan(route, counts, t):
    e = route[:, R_E1:R_E2 + 1].astype(jnp.int32)
    rank = route[:, R_RANK1:R_RANK2 + 1].astype(jnp.int32)
    cnt = counts[0, ROUTER_EXPERT_LANE0:ROUTER_EXPERT_LANE0 + N_EXPERTS].astype(jnp.int32)
    padded = ((cnt + (MOE_TILE - 1)) // MOE_TILE) * MOE_TILE
    ends = jnp.cumsum(padded)
    off = ends - padded
    onehot = e[..., None] == jnp.arange(N_EXPERTS, dtype=jnp.int32)
    pos = jnp.sum(jnp.where(onehot, off, 0), axis=-1) + rank
    ntiles = _moe_tiles(t)
    tile_start = jnp.arange(ntiles, dtype=jnp.int32) * MOE_TILE
    tile_expert = jnp.minimum(jnp.sum((tile_start[:, None] >= ends[None, :]).astype(jnp.int32), axis=1),
                              N_EXPERTS - 1)
    n_active = (ends[-1] // MOE_TILE).reshape(1)
    return pos, tile_expert, n_active


def _token_tile(ref, i):
    return ref.at[pl.ds(pl.multiple_of(i * TOKEN_ROWS, TOKEN_ROWS), TOKEN_ROWS), :]


def _dispatch_kernel(pos_ref, hn_ref, xs_in_ref, xs_ref, sem):
    del xs_in_ref
    rows = hn_ref.shape[0] // TOKEN_ROWS

    def copy(r, k):
        return pltpu.make_async_copy(_token_tile(hn_ref, r), _token_tile(xs_ref, pos_ref[0, k, r]), sem)

    def issue(r, c):
        copy(r, 0).start()
        copy(r, 1).start()
        return c

    def drain(r, c):
        copy(r, 0).wait()
        copy(r, 1).wait()
        return c

    lax.fori_loop(0, rows, issue, 0)
    lax.fori_loop(0, rows, drain, 0)


def _pos_blocks(pos, tm):
    t = pos.shape[0]
    return jnp.transpose(pos.reshape(t // tm, tm, 2), (0, 2, 1))


def _dispatch(pos, hn_tt, t, tm=1024):
    nrows = _moe_tiles(t) * MOE_TILE * TOKEN_ROWS
    xs0 = jnp.zeros((nrows, LANES), F32)
    return pl.pallas_call(
        _dispatch_kernel,
        grid=(t // tm,),
        in_specs=[pl.BlockSpec((1, 2, tm), lambda i: (i, 0, 0), memory_space=pltpu.SMEM),
                  pl.BlockSpec((tm * TOKEN_ROWS, LANES), lambda i: (i, 0)),
                  pl.BlockSpec(memory_space=pl.ANY)],
        out_specs=pl.BlockSpec(memory_space=pl.ANY),
        out_shape=jax.ShapeDtypeStruct((nrows, LANES), F32),
        scratch_shapes=[pltpu.SemaphoreType.DMA(())],
        input_output_aliases={2: 0},
        compiler_params=_params("arbitrary"),
        name="moe_dispatch",
    )(_pos_blocks(pos, tm), hn_tt, xs0)


def _expert_kernel(te_ref, na_ref, xs_ref, wg_ref, wu_ref, wd_ref, ys_ref, wgu_s, wd_s):
    i = pl.program_id(0)
    e = te_ref[i]
    new_expert = (i == 0) | (te_ref[jnp.maximum(i - 1, 0)] != e)

    @pl.when(i < na_ref[0])
    def _():
        @pl.when(new_expert)
        def _():
            wgu_s[:, :D_EXPERT] = wg_ref[0].astype(BF16)
            wgu_s[:, D_EXPERT:] = wu_ref[0].astype(BF16)
            wd_s[...] = wd_ref[0].astype(BF16)

        x = _from_token_tiles(xs_ref, MOE_TILE).astype(BF16)
        au = _dot(x, wgu_s[...])
        a = au[:, :D_EXPERT]
        hid = (a * _sigmoid(a)) * au[:, D_EXPERT:]
        _to_token_tiles(ys_ref, _dot(hid.astype(BF16), wd_s[...]))


def _experts(tile_expert, n_active, xs, w_gate, w_up, w_down):
    ntiles = tile_expert.shape[0]
    last = lambda i, na: jnp.minimum(i, na[0] - 1)
    tile = pl.BlockSpec((MOE_TILE * TOKEN_ROWS, LANES), lambda i, te, na: (last(i, na), 0))
    wspec = lambda shape: pl.BlockSpec((1,) + shape, lambda i, te, na: (te[last(i, na)], 0, 0))
    return pl.pallas_call(
        _expert_kernel,
        grid_spec=pltpu.PrefetchScalarGridSpec(
            num_scalar_prefetch=2,
            grid=(ntiles,),
            in_specs=[tile, wspec((D_MODEL, D_EXPERT)), wspec((D_MODEL, D_EXPERT)), wspec((D_EXPERT, D_MODEL))],
            out_specs=tile,
            scratch_shapes=[pltpu.VMEM((D_MODEL, 2 * D_EXPERT), BF16), pltpu.VMEM((D_EXPERT, D_MODEL), BF16)]),
        out_shape=jax.ShapeDtypeStruct(xs.shape, F32),
        input_output_aliases={2: 0},
        compiler_params=_params("arbitrary"),
        name="moe_experts",
    )(tile_expert, n_active, xs, w_gate, w_up, w_down)


def _combine_kernel(pos_ref, h_ref, route_ref, gn_ref, ys_ref, hout_ref, hnorm_ref, buf1, buf2, sem):
    rows = h_ref.shape[0]
    bufs = (buf1, buf2)

    def copy(r, k):
        return pltpu.make_async_copy(_token_tile(ys_ref, pos_ref[0, k, r]), _token_tile(bufs[k], r), sem)

    def issue(r, c):
        copy(r, 0).start()
        copy(r, 1).start()
        return c

    def drain(r, c):
        copy(r, 0).wait()
        copy(r, 1).wait()
        return c

    lax.fori_loop(0, rows, issue, 0)
    lax.fori_loop(0, rows, drain, 0)
    route = route_ref[...]
    h = (h_ref[...] + route[:, R_W1:R_W1 + 1] * _from_token_tiles(buf1, rows)
         + route[:, R_W2:R_W2 + 1] * _from_token_tiles(buf2, rows))
    hout_ref[...] = h
    hnorm_ref[...] = _rms(h, gn_ref[...])


def _combine(pos, h, route, gnext, ys, tm=256):
    t = h.shape[0]
    row = lambda i: (i, 0)
    return pl.pallas_call(
        _combine_kernel,
        grid=(t // tm,),
        in_specs=[pl.BlockSpec((1, 2, tm), lambda i: (i, 0, 0), memory_space=pltpu.SMEM),
                  pl.BlockSpec((tm, D_MODEL), row), pl.BlockSpec((tm, ROUTER_LANES), row),
                  pl.BlockSpec((1, D_MODEL), lambda i: (0, 0)),
                  pl.BlockSpec(memory_space=pl.ANY)],
        out_specs=[pl.BlockSpec((tm, D_MODEL), row), pl.BlockSpec((tm, D_MODEL), row)],
        out_shape=[jax.ShapeDtypeStruct((t, D_MODEL), F32), jax.ShapeDtypeStruct((t, D_MODEL), F32)],
        scratch_shapes=[pltpu.VMEM((tm * TOKEN_ROWS, LANES), F32), pltpu.VMEM((tm * TOKEN_ROWS, LANES), F32),
                        pltpu.SemaphoreType.DMA(())],
        compiler_params=_params("arbitrary"),
        name="moe_combine",
    )(_pos_blocks(pos, tm), h, route, gnext, ys)


def _moe(h, hn_tt, route, counts, w_gate, w_up, w_down, gnext):
    t = h.shape[0]
    pos, tile_expert, n_active = _dispatch_plan(route, counts, t)
    xs = _dispatch(pos, hn_tt, t)
    ys = _experts(tile_expert, n_active, xs, w_gate, w_up, w_down)
    return _combine(pos, h, route, gnext, ys)


def _ssm_kernel(nch, x_ref, r_ref, wd_ref, vd_ref, a_ref, d_ref, o_ref, xcat, st, xprev):
    nb = x_ref.shape[0]
    ns = SSM_BLOCK_STATE
    for s in range(SSM_CHUNK):
        for b in range(nb):
            xs = x_ref[b, pl.ds(s, nch, stride=SSM_CHUNK), :]
            xcat[b * nch:(b + 1) * nch, s * LANES:(s + 1) * LANES] = xs.astype(BF16)
    st[...] = _dot(xcat[...], wd_ref[0])
    ar = a_ref[0, :, 0:ns]
    ai = a_ref[0, :, ns:2 * ns]

    def step(k, carry):
        out = []
        for b in range(nb):
            xr, xi = carry[b]
            row = b * nch + k
            sr = st[pl.ds(row, 1), 0:ns]
            si = st[pl.ds(row, 1), ns:2 * ns]
            st[pl.ds(row, 1), 0:ns] = xr
            st[pl.ds(row, 1), ns:2 * ns] = xi
            out.append((ar * xr - ai * xi + sr, ar * xi + ai * xr + si))
        return tuple(out)

    zero = jnp.zeros((1, ns), F32)
    lax.fori_loop(0, nch, step, tuple((zero, zero) for _ in range(nb)))
    xprev[...] = st[...].astype(BF16)
    dskip = d_ref[...]
    for t in range(SSM_CHUNK):
        y = (_dot(xcat[:, :(t + 1) * LANES], r_ref[0, (SSM_CHUNK - 1 - t) * LANES:, :])
             + _dot(xprev[...], vd_ref[0, :, t * LANES:(t + 1) * LANES]))
        for b in range(nb):
            u = x_ref[b, pl.ds(t, nch, stride=SSM_CHUNK), :]
            o_ref[b, pl.ds(t, nch, stride=SSM_CHUNK), :] = _gelu_tanh(y[b * nch:(b + 1) * nch, :] + dskip * u)


def _ssm(x3, r, wd, vd, a16, dskip):
    nb, seq, _ = x3.shape
    nch = seq // SSM_CHUNK
    rows = nb * nch
    blk = pl.BlockSpec((nb, seq, LANES), lambda g: (0, 0, g), pipeline_mode=pl.Buffered(1))
    per = lambda shape: pl.BlockSpec((1,) + shape, lambda g: (g, 0, 0))
    return pl.pallas_call(
        functools.partial(_ssm_kernel, nch),
        grid=(SSM_BLOCKS,),
        in_specs=[blk, per((SSM_CHUNK * LANES, LANES)), per((SSM_CHUNK * LANES, 2 * SSM_BLOCK_STATE)),
                  per((2 * SSM_BLOCK_STATE, SSM_CHUNK * LANES)), per((1, 2 * SSM_BLOCK_STATE)),
                  pl.BlockSpec((1, LANES), lambda g: (0, g))],
        out_specs=blk,
        out_shape=jax.ShapeDtypeStruct(x3.shape, F32),
        scratch_shapes=[pltpu.VMEM((rows, SSM_CHUNK * LANES), BF16),
                        pltpu.VMEM((rows, 2 * SSM_BLOCK_STATE), F32),
                        pltpu.VMEM((rows, 2 * SSM_BLOCK_STATE), BF16)],
        compiler_params=_params("parallel"),
        name="ssm",
    )(x3, r, wd, vd, a16, dskip)


def _ssm_tables(lambda_re, lambda_im, log_step, b_re, b_im, c_re, c_im):
    hp = lax.Precision.HIGHEST
    lam_re = jnp.minimum(lambda_re.astype(F32), -1e-4)
    lam_im = lambda_im.astype(F32)
    step = jnp.exp(log_step.astype(F32))[:, None]
    L = SSM_CHUNK

    def powers(lags):
        lags = jnp.asarray(lags, F32)[:, None, None]
        dec = jnp.exp(lags * (lam_re * step))
        ang = lags * (lam_im * step)
        return dec * jnp.cos(ang), dec * jnp.sin(ang)

    pr_re, pr_im = powers(np.arange(L - 1, -1, -1))
    p1_re, p1_im = powers(np.arange(1, L + 1))
    abar_re, abar_im = p1_re[0], p1_im[0]
    denom = lam_re * lam_re + lam_im * lam_im
    num_re = abar_re - 1.0
    f_re = ((num_re * lam_re + abar_im * lam_im) / denom)[..., None]
    f_im = ((abar_im * lam_re - num_re * lam_im) / denom)[..., None]
    br, bi = b_re.astype(F32), b_im.astype(F32)
    bb_re = f_re * br - f_im * bi
    bb_im = f_re * bi + f_im * br
    cr, ci = c_re.astype(F32), c_im.astype(F32)
    ca_re = cr[None] * pr_re[:, :, None, :] - ci[None] * pr_im[:, :, None, :]
    ca_im = cr[None] * pr_im[:, :, None, :] + ci[None] * pr_re[:, :, None, :]
    taps = (jnp.einsum('lgop,gpi->lgio', ca_re, bb_re, precision=hp)
            - jnp.einsum('lgop,gpi->lgio', ca_im, bb_im, precision=hp))
    nb, ng = SSM_BLOCKS, SSM_LANE_GROUPS
    eye = jnp.eye(ng, dtype=F32)

    def blockdiag(m):
        lead = m.shape[:-3]
        a, b = m.shape[-2:]
        m = m.reshape(lead + (nb, ng, a, b))
        out = jnp.einsum('...ngab,gh->...ngahb', m, eye)
        return out.reshape(lead + (nb, ng * a, ng * b))

    d_blk = blockdiag(taps)
    r = jnp.transpose(d_blk, (1, 0, 2, 3)).reshape(nb, L * LANES, LANES)
    w_re = pr_re[..., None] * bb_re[None] - pr_im[..., None] * bb_im[None]
    w_im = pr_re[..., None] * bb_im[None] + pr_im[..., None] * bb_re[None]
    w_re = blockdiag(jnp.swapaxes(w_re, -1, -2))
    w_im = blockdiag(jnp.swapaxes(w_im, -1, -2))
    wd = jnp.concatenate([w_re, w_im], axis=-1)
    wd = jnp.transpose(wd, (1, 0, 2, 3)).reshape(nb, L * LANES, 2 * SSM_BLOCK_STATE)
    ca1_re = cr[None] * p1_re[:, :, None, :] - ci[None] * p1_im[:, :, None, :]
    ca1_im = cr[None] * p1_im[:, :, None, :] + ci[None] * p1_re[:, :, None, :]
    v_re = blockdiag(jnp.swapaxes(ca1_re, -1, -2))
    v_im = blockdiag(jnp.swapaxes(-ca1_im, -1, -2))
    vd = jnp.concatenate([v_re, v_im], axis=-2)
    vd = jnp.transpose(vd, (1, 2, 0, 3)).reshape(nb, 2 * SSM_BLOCK_STATE, L * LANES)
    a_l = jnp.concatenate([p1_re[L - 1].reshape(nb, 1, SSM_BLOCK_STATE),
                           p1_im[L - 1].reshape(nb, 1, SSM_BLOCK_STATE)], axis=-1)
    return r.astype(BF16), wd.astype(BF16), vd.astype(BF16), a_l


def _router_tables(w_group, b_group, w_expert, b_expert):
    w = jnp.concatenate([w_group.astype(F32),
                         jnp.transpose(w_expert.astype(F32), (1, 0, 2)).reshape(D_MODEL, N_EXPERTS)], axis=1)
    b = jnp.concatenate([b_group.astype(F32), b_expert.astype(F32).reshape(N_EXPERTS)])
    pad = ROUTER_LANES - w.shape[1]
    return jnp.pad(w, ((0, 0), (0, pad))), jnp.pad(b, (0, pad)).reshape(1, ROUTER_LANES)


def kernel(x, norm_mix, norm_ffn, norm_final, attn_w_qkv, attn_w_o, attn_rel_bias, ssm_lambda_re, ssm_lambda_im, ssm_log_step, ssm_b_re, ssm_b_im, ssm_c_re, ssm_c_im, ssm_d, ssm_w_out, moe_w_group_router, moe_b_group_router, moe_w_expert_router, moe_b_expert_router, moe_w_gate, moe_w_up, moe_w_down):
    batch, seq, d = x.shape
    t = batch * seq
    row = lambda v: v.astype(F32).reshape(1, d)
    h = x.astype(F32).reshape(t, d)

    qkv = _qkv(h, row(norm_mix[0]), attn_w_qkv[0].astype(BF16))
    o = _attention(qkv, _rel_bias_table(attn_rel_bias[0]), batch, seq)
    wr, br = _router_tables(moe_w_group_router[0], moe_b_group_router[0],
                            moe_w_expert_router[0], moe_b_expert_router[0])
    h, hn_tt, route, counts = _post(o, h, attn_w_o[0].astype(BF16), row(norm_ffn[0]), wr, br, glu=False)
    h, hn32 = _moe(h, hn_tt, route, counts, moe_w_gate[0], moe_w_up[0], moe_w_down[0], row(norm_mix[1]))

    r, wd, vd, a_l = _ssm_tables(ssm_lambda_re[0], ssm_lambda_im[0], ssm_log_step[0],
                                 ssm_b_re[0], ssm_b_im[0], ssm_c_re[0], ssm_c_im[0])
    y = _ssm(hn32.reshape(batch, seq, d), r, wd, vd, a_l, row(ssm_d[0]))
    wr, br = _router_tables(moe_w_group_router[1], moe_b_group_router[1],
                            moe_w_expert_router[1], moe_b_expert_router[1])
    h, hn_tt, route, counts = _post(y.reshape(t, d), h, ssm_w_out[0].astype(BF16), row(norm_ffn[1]), wr, br, glu=True)
    _, out = _moe(h, hn_tt, route, counts, moe_w_gate[1], moe_w_up[1], moe_w_down[1], row(norm_final))
    return out.reshape(batch, seq, d).astype(x.dtype)
```

```python
import functools

import jax
import jax.numpy as jnp
import numpy as np
from jax import lax
from jax.experimental import pallas as pl
from jax.experimental.pallas import tpu as pltpu

F32 = jnp.float32
BF16 = jnp.bfloat16

D_MODEL = 1024
CHUNK = 64
LOOKBACK_CHUNKS = 8
BAND = (LOOKBACK_CHUNKS + 1) * CHUNK
N_HEADS = 16
HEAD_DIM = D_MODEL // N_HEADS
MAX_REL = 128
SSM_GROUP = 16
SSM_GROUPS = D_MODEL // SSM_GROUP
SSM_STATE = 64
N_EXPERT_GROUPS = 4
EXPERTS_PER_GROUP = 8
N_EXPERTS = N_EXPERT_GROUPS * EXPERTS_PER_GROUP
D_EXPERT = D_MODEL // 4
RMS_EPS = 1e-6
NEG_BIG = -1e30

LANES = 128
SUBLANES = 8
MXU_DIM = 256
VMEM_LIMIT = 56 * 1024 * 1024

HEADS_PER_GROUP = MXU_DIM // HEAD_DIM
N_HEAD_GROUPS = N_HEADS // HEADS_PER_GROUP
ATTN_QBLOCK = LOOKBACK_CHUNKS * CHUNK
REL_PAD = -(-(2 * MAX_REL + 1) // LANES) * LANES
BIAS_WIDTH = -(-(BAND + CHUNK - 1) // LANES) * LANES
ROUTER_LANES = LANES
ROUTER_EXPERT_LANE0 = N_EXPERT_GROUPS
SSM_CHUNK = 16
SSM_LANE_GROUPS = LANES // SSM_GROUP
SSM_BLOCKS = D_MODEL // LANES
SSM_BLOCK_STATE = SSM_LANE_GROUPS * SSM_STATE
TOKEN_ROWS = D_MODEL // LANES
MOE_TILE = 256
R_E1, R_E2, R_W1, R_W2, R_RANK1, R_RANK2 = range(6)


def _dot(a, b):
    return jnp.dot(a, b, preferred_element_type=F32)


def _rms(x, g):
    return x * lax.rsqrt(jnp.mean(x * x, axis=-1, keepdims=True) + RMS_EPS) * g


def _sigmoid(x):
    return 1.0 / (1.0 + jnp.exp(-x))


def _gelu_tanh(x):
    c = np.float32(np.sqrt(2.0 / np.pi))
    return 0.5 * x * (1.0 + jnp.tanh(c * (x + np.float32(0.044715) * (x * x * x))))


def _params(*sem):
    return pltpu.CompilerParams(dimension_semantics=sem, vmem_limit_bytes=VMEM_LIMIT)


def _resident(shape):
    return pl.BlockSpec(shape, lambda *_: (0,) * len(shape), pipeline_mode=pl.Buffered(1))


def _cast_once(w_ref, wb):
    @pl.when(pl.program_id(0) == 0)
    def _():
        wb[...] = w_ref[...].astype(BF16)


def _qkv_kernel(x_ref, g_ref, w_ref, o_ref, wb):
    _cast_once(w_ref, wb)
    xn = _rms(x_ref[...], g_ref[...]).astype(BF16)
    for c in range(3):
        acc = _dot(xn, wb[:, c * D_MODEL:(c + 1) * D_MODEL])
        if c == 0:
            acc = acc * np.float32(HEAD_DIM ** -0.5)
        o_ref[:, c * D_MODEL:(c + 1) * D_MODEL] = acc.astype(BF16)


def _qkv(x, g, w, tm=512):
    t = x.shape[0]
    return pl.pallas_call(
        _qkv_kernel,
        grid=(t // tm,),
        in_specs=[pl.BlockSpec((tm, D_MODEL), lambda i: (i, 0)),
                  pl.BlockSpec((1, D_MODEL), lambda i: (0, 0)),
                  _resident(w.shape)],
        out_specs=pl.BlockSpec((tm, 3 * D_MODEL), lambda i: (i, 0)),
        out_shape=jax.ShapeDtypeStruct((t, 3 * D_MODEL), BF16),
        scratch_shapes=[pltpu.VMEM(w.shape, BF16)],
        compiler_params=_params("arbitrary"),
        name="qkv",
    )(x, g, w)


def _expand_rel_bias(rel_ref, bias):
    rel = rel_ref[0]
    r1 = rel.astype(BF16)
    r2 = (rel - r1.astype(F32)).astype(BF16)
    r3 = (rel - r1.astype(F32) - r2.astype(F32)).astype(BF16)
    j = lax.broadcasted_iota(jnp.int32, (REL_PAD, BIAS_WIDTH), 1)
    src = jnp.clip(BAND - 1 - j, -MAX_REL, MAX_REL) + MAX_REL
    sel = jnp.where(lax.broadcasted_iota(jnp.int32, (REL_PAD, BIAS_WIDTH), 0) == src, 1.0, 0.0).astype(BF16)
    u = (_dot(r1, sel) + _dot(r2, sel)) + _dot(r3, sel)
    for h in range(HEADS_PER_GROUP):
        rows = jnp.broadcast_to(u[h:h + 1, :], (CHUNK, BIAS_WIDTH))
        rows = pltpu.roll(rows, BIAS_WIDTH - (CHUNK - 1), 1, stride=1, stride_axis=0)
        bias[h * CHUNK:(h + 1) * CHUNK, :] = rows[:, :BAND]


def _attn_kernel(q_ref, kp_ref, kc_ref, vp_ref, vc_ref, rel_ref, o_ref, kk, vv, bias):
    qb = pl.program_id(2)

    @pl.when(qb == 0)
    def _():
        _expand_rel_bias(rel_ref, bias)

    kk[0:ATTN_QBLOCK, :] = kp_ref[...]
    kk[ATTN_QBLOCK:, :] = kc_ref[...]
    vv[0:ATTN_QBLOCK, :] = vp_ref[...]
    vv[ATTN_QBLOCK:, :] = vc_ref[...]
    lane_head = lax.broadcasted_iota(jnp.int32, (CHUNK, MXU_DIM), 1) // HEAD_DIM
    col = lax.broadcasted_iota(jnp.int32, (1, BAND), 1)
    for j in range(ATTN_QBLOCK // CHUNK):
        qj = q_ref[j * CHUNK:(j + 1) * CHUNK, :]
        lhs = jnp.concatenate(
            [jnp.where(lane_head == h, qj, jnp.zeros_like(qj)) for h in range(HEADS_PER_GROUP)], axis=0)
        kwin = kk[j * CHUNK:j * CHUNK + BAND, :]
        vwin = vv[j * CHUNK:j * CHUNK + BAND, :]
        s = lax.dot_general(lhs, kwin, (((1,), (1,)), ((), ())), preferred_element_type=F32)
        valid = (col + j * CHUNK >= ATTN_QBLOCK) | (qb > 0)
        s = jnp.where(valid, s + bias[...], NEG_BIG)
        m = jnp.max(s, axis=-1, keepdims=True)
        p = jnp.exp(s - m)
        l = jnp.sum(p, axis=-1, keepdims=True)
        o_all = _dot(p.astype(BF16), vwin) * (1.0 / l)
        o = jnp.zeros((CHUNK, MXU_DIM), F32)
        for h in range(HEADS_PER_GROUP):
            o = o + jnp.where(lane_head == h, o_all[h * CHUNK:(h + 1) * CHUNK, :], 0.0)
        o_ref[j * CHUNK:(j + 1) * CHUNK, :] = o.astype(BF16)


def _attention(qkv, rel, batch, seq):
    nqb = seq // ATTN_QBLOCK
    kcol = D_MODEL // MXU_DIM
    blk = (ATTN_QBLOCK, MXU_DIM)
    cur = lambda off: (lambda b, g, i: (b * nqb + i, off + g))
    prev = lambda off: (lambda b, g, i: (b * nqb + jnp.maximum(i - 1, 0), off + g))
    return pl.pallas_call(
        _attn_kernel,
        grid=(batch, N_HEAD_GROUPS, nqb),
        in_specs=[pl.BlockSpec(blk, cur(0)),
                  pl.BlockSpec(blk, prev(kcol)), pl.BlockSpec(blk, cur(kcol)),
                  pl.BlockSpec(blk, prev(2 * kcol)), pl.BlockSpec(blk, cur(2 * kcol)),
                  pl.BlockSpec((1, SUBLANES, REL_PAD), lambda b, g, i: (g, 0, 0))],
        out_specs=pl.BlockSpec(blk, cur(0)),
        out_shape=jax.ShapeDtypeStruct((batch * seq, D_MODEL), BF16),
        scratch_shapes=[pltpu.VMEM((2 * ATTN_QBLOCK, MXU_DIM), BF16),
                        pltpu.VMEM((2 * ATTN_QBLOCK, MXU_DIM), BF16),
                        pltpu.VMEM((HEADS_PER_GROUP * CHUNK, BAND), F32)],
        compiler_params=_params("parallel", "parallel", "arbitrary"),
        name="attn",
    )(qkv, qkv, qkv, qkv, qkv, rel)


def _rel_bias_rows(rel_bias):
    rel = rel_bias.astype(F32).reshape(N_HEAD_GROUPS, HEADS_PER_GROUP, 2 * MAX_REL + 1)
    return jnp.pad(rel, ((0, 0), (0, SUBLANES - HEADS_PER_GROUP), (0, REL_PAD - (2 * MAX_REL + 1))))


def _route(hn, wr_ref, br_ref, base):
    a_hi = hn.astype(BF16)
    a_lo = (hn - a_hi.astype(F32)).astype(BF16)
    w = wr_ref[...]
    w_hi = w.astype(BF16)
    w_lo = (w - w_hi.astype(F32)).astype(BF16)
    lg = _dot(a_hi, w_hi) + (_dot(a_hi, w_lo) + _dot(a_lo, w_hi)) + br_ref[...]
    rows = lg.shape[0]
    lane = lax.broadcasted_iota(jnp.int32, lg.shape, 1)
    ninf = np.float32(-np.inf)
    big = np.int32(ROUTER_LANES)
    gmask = lane < N_EXPERT_GROUPS
    gl = jnp.where(gmask, lg, ninf)
    gmax = jnp.max(gl, axis=-1, keepdims=True)
    gidx = jnp.min(jnp.where(gl == gmax, lane, big), axis=-1, keepdims=True)
    gprob = 1.0 / jnp.sum(jnp.where(gmask, jnp.exp(lg - gmax), 0.0), axis=-1, keepdims=True)
    lo = ROUTER_EXPERT_LANE0 + EXPERTS_PER_GROUP * gidx
    el = jnp.where((lane >= lo) & (lane < lo + EXPERTS_PER_GROUP), lg, ninf)
    l1 = jnp.max(el, axis=-1, keepdims=True)
    i1 = jnp.min(jnp.where(el == l1, lane, big), axis=-1, keepdims=True)
    el2 = jnp.where(lane == i1, ninf, el)
    l2 = jnp.max(el2, axis=-1, keepdims=True)
    i2 = jnp.min(jnp.where(el2 == l2, lane, big), axis=-1, keepdims=True)
    t = jnp.exp(l2 - l1)
    w1 = gprob / (1.0 + t)
    w2 = w1 * t
    o1 = lane == i1
    o2 = lane == i2
    tri = (lax.broadcasted_iota(jnp.int32, (rows, rows), 0)
           > lax.broadcasted_iota(jnp.int32, (rows, rows), 1))
    tri = jnp.where(tri, 1.0, 0.0).astype(BF16)
    o1f = jnp.where(o1, 1.0, 0.0)
    o2f = jnp.where(o2, 1.0, 0.0)
    p1 = _dot(tri, o1f.astype(BF16))
    p2 = _dot(tri, o2f.astype(BF16))
    c1 = jnp.sum(o1f, axis=0, keepdims=True)
    c2 = jnp.sum(o2f, axis=0, keepdims=True)
    rank1 = jnp.sum(jnp.where(o1, base + p1, 0.0), axis=-1, keepdims=True)
    rank2 = jnp.sum(jnp.where(o2, base + c1 + p2, 0.0), axis=-1, keepdims=True)
    rec = jnp.zeros(lg.shape, F32)
    for ln, val in ((R_E1, (i1 - ROUTER_EXPERT_LANE0).astype(F32)), (R_E2, (i2 - ROUTER_EXPERT_LANE0).astype(F32)),
                    (R_W1, w1), (R_W2, w2), (R_RANK1, rank1), (R_RANK2, rank2)):
        rec = jnp.where(lane == ln, val, rec)
    return rec, base + c1 + c2


def _to_token_tiles(ref, x):
    rows = x.shape[0]
    for j in range(TOKEN_ROWS):
        ref[pl.ds(j, rows, stride=TOKEN_ROWS), :] = x[:, j * LANES:(j + 1) * LANES]


def _from_token_tiles(ref, rows):
    return jnp.concatenate([ref[pl.ds(j, rows, stride=TOKEN_ROWS), :] for j in range(TOKEN_ROWS)], axis=1)


def _post_kernel(glu, y_ref, h_ref, w_ref, g_ref, wr_ref, br_ref, hout_ref, hntt_ref, route_ref, cnt_ref, wb):
    @pl.when(pl.program_id(0) == 0)
    def _():
        cnt_ref[...] = jnp.zeros_like(cnt_ref)

    _cast_once(w_ref, wb)
    y = y_ref[...].astype(BF16)
    if glu:
        mix = _dot(y, wb[:, :D_MODEL]) * _sigmoid(_dot(y, wb[:, D_MODEL:]))
    else:
        mix = _dot(y, wb[...])
    h = h_ref[...] + mix
    hout_ref[...] = h
    hn = _rms(h, g_ref[...])
    _to_token_tiles(hntt_ref, hn)
    route_ref[...], cnt_ref[...] = _route(hn, wr_ref, br_ref, cnt_ref[...])


def _post(y, h, w, g, wr, br, glu, tm=512):
    t = h.shape[0]
    row = lambda i: (i, 0)
    fixed = lambda i: (0, 0)
    return pl.pallas_call(
        functools.partial(_post_kernel, glu),
        grid=(t // tm,),
        in_specs=[pl.BlockSpec((tm, D_MODEL), row), pl.BlockSpec((tm, D_MODEL), row),
                  _resident(w.shape), pl.BlockSpec((1, D_MODEL), fixed),
                  pl.BlockSpec((D_MODEL, ROUTER_LANES), fixed), pl.BlockSpec((1, ROUTER_LANES), fixed)],
        out_specs=[pl.BlockSpec((tm, D_MODEL), row), pl.BlockSpec((tm * TOKEN_ROWS, LANES), row),
                   pl.BlockSpec((tm, ROUTER_LANES), row), pl.BlockSpec((1, ROUTER_LANES), fixed)],
        out_shape=[jax.ShapeDtypeStruct((t, D_MODEL), F32), jax.ShapeDtypeStruct((t * TOKEN_ROWS, LANES), F32),
                   jax.ShapeDtypeStruct((t, ROUTER_LANES), F32), jax.ShapeDtypeStruct((1, ROUTER_LANES), F32)],
        scratch_shapes=[pltpu.VMEM(w.shape, BF16)],
        compiler_params=_params("arbitrary"),
        name="post_glu" if glu else "post_attn",
    )(y, h, w, g, wr, br)


def _moe_tiles(t):
    return (2 * t) // MOE_TILE + N_EXPERTS


def _dispatch_plan(route, counts, t):
    e = route[:, R_E1:R_E2 + 1].astype(jnp.int32)
    rank = route[:, R_RANK1:R_RANK2 + 1].astype(jnp.int32)
    cnt = counts[0, ROUTER_EXPERT_LANE0:ROUTER_EXPERT_LANE0 + N_EXPERTS].astype(jnp.int32)
    padded = ((cnt + (MOE_TILE - 1)) // MOE_TILE) * MOE_TILE
    ends = jnp.cumsum(padded)
    off = ends - padded
    onehot = e[..., None] == jnp.arange(N_EXPERTS, dtype=jnp.int32)
    pos = jnp.sum(jnp.where(onehot, off, 0), axis=-1) + rank
    ntiles = _moe_tiles(t)
    tile_start = jnp.arange(ntiles, dtype=jnp.int32) * MOE_TILE
    tile_expert = jnp.minimum(jnp.sum((tile_start[:, None] >= ends[None, :]).astype(jnp.int32), axis=1),
                              N_EXPERTS - 1)
    n_active = (ends[-1] // MOE_TILE).reshape(1)
    plan = jnp.concatenate([off + cnt, padded - cnt, n_active])
    return pos, tile_expert, n_active, plan


def _token_tile(ref, i):
    return ref.at[pl.ds(pl.multiple_of(i * TOKEN_ROWS, TOKEN_ROWS), TOKEN_ROWS), :]


def _zero_fill(plan_ref, xs_ref, zbuf, zsem, ntiles):
    zbuf[...] = jnp.zeros_like(zbuf)
    half = MOE_TILE // 2
    sizes = [1 << b for b in reversed(range(half.bit_length()))]

    def chunk(first, size):
        return pltpu.make_async_copy(zbuf.at[pl.ds(0, size * TOKEN_ROWS), :],
                                     xs_ref.at[pl.ds(pl.multiple_of(first * TOKEN_ROWS, TOKEN_ROWS),
                                                     size * TOKEN_ROWS), :], zsem)

    def pads(wait):
        def body(e, c):
            first = plan_ref[e]
            n = plan_ref[N_EXPERTS + e]
            for size in sizes:
                hit = (n & size) != 0

                @pl.when(hit)
                def _():
                    cp = chunk(first, size)
                    cp.wait() if wait else cp.start()

                first = first + jnp.where(hit, size, 0)
            return c
        lax.fori_loop(0, N_EXPERTS, body, 0)

    def tail(wait):
        def body(i, c):
            for k in range(2):
                cp = chunk(i * MOE_TILE + k * half, half)
                cp.wait() if wait else cp.start()
            return c
        lax.fori_loop(plan_ref[2 * N_EXPERTS], ntiles, body, 0)

    pads(False)
    tail(False)
    pads(True)
    tail(True)


def _dispatch_kernel(ntiles, plan_ref, pos_ref, hn_ref, xs_ref, sem, zbuf, zsem):
    rows = hn_ref.shape[0] // TOKEN_ROWS

    @pl.when(pl.program_id(0) == 0)
    def _():
        _zero_fill(plan_ref, xs_ref, zbuf, zsem, ntiles)

    def copy(r, k):
        return pltpu.make_async_copy(_token_tile(hn_ref, r), _token_tile(xs_ref, pos_ref[0, k, r]), sem)

    def issue(r, c):
        copy(r, 0).start(priority=0)
        copy(r, 1).start(priority=1)
        return c

    def drain(r, c):
        copy(r, 0).wait()
        copy(r, 1).wait()
        return c

    lax.fori_loop(0, rows, issue, 0)
    lax.fori_loop(0, rows, drain, 0)


def _pos_blocks(pos, tm):
    t = pos.shape[0]
    return jnp.transpose(pos.reshape(t // tm, tm, 2), (0, 2, 1))


def _dispatch(plan, pos, hn_tt, t, tm=1024):
    ntiles = _moe_tiles(t)
    nrows = ntiles * MOE_TILE * TOKEN_ROWS
    return pl.pallas_call(
        functools.partial(_dispatch_kernel, ntiles),
        grid_spec=pltpu.PrefetchScalarGridSpec(
            num_scalar_prefetch=1,
            grid=(t // tm,),
            in_specs=[pl.BlockSpec((1, 2, tm), lambda i, plan: (i, 0, 0), memory_space=pltpu.SMEM),
                      pl.BlockSpec((tm * TOKEN_ROWS, LANES), lambda i, plan: (i, 0))],
            out_specs=pl.BlockSpec(memory_space=pl.ANY),
            scratch_shapes=[pltpu.SemaphoreType.DMA(()),
                            pltpu.VMEM((MOE_TILE // 2 * TOKEN_ROWS, LANES), F32),
                            pltpu.SemaphoreType.DMA(())]),
        out_shape=jax.ShapeDtypeStruct((nrows, LANES), F32),
        compiler_params=_params("arbitrary"),
        name="moe_dispatch",
    )(plan, _pos_blocks(pos, tm), hn_tt)


def _expert_kernel(te_ref, na_ref, xs_ref, wg_ref, wu_ref, wd_ref, ys_ref, wgu_s, wd_s):
    i = pl.program_id(0)
    e = te_ref[i]
    new_expert = (i == 0) | (te_ref[jnp.maximum(i - 1, 0)] != e)

    @pl.when(i < na_ref[0])
    def _():
        @pl.when(new_expert)
        def _():
            wgu_s[:, :D_EXPERT] = wg_ref[0].astype(BF16)
            wgu_s[:, D_EXPERT:] = wu_ref[0].astype(BF16)
            wd_s[...] = wd_ref[0].astype(BF16)

        x = _from_token_tiles(xs_ref, MOE_TILE).astype(BF16)
        au = _dot(x, wgu_s[...])
        a = au[:, :D_EXPERT]
        hid = (a * _sigmoid(a)) * au[:, D_EXPERT:]
        _to_token_tiles(ys_ref, _dot(hid.astype(BF16), wd_s[...]))


def _experts(tile_expert, n_active, xs, w_gate, w_up, w_down):
    ntiles = tile_expert.shape[0]
    last = lambda i, na: jnp.minimum(i, na[0] - 1)
    tile = pl.BlockSpec((MOE_TILE * TOKEN_ROWS, LANES), lambda i, te, na: (last(i, na), 0))
    wspec = lambda shape: pl.BlockSpec((1,) + shape, lambda i, te, na: (te[last(i, na)], 0, 0))
    return pl.pallas_call(
        _expert_kernel,
        grid_spec=pltpu.PrefetchScalarGridSpec(
            num_scalar_prefetch=2,
            grid=(ntiles,),
            in_specs=[tile, wspec((D_MODEL, D_EXPERT)), wspec((D_MODEL, D_EXPERT)), wspec((D_EXPERT, D_MODEL))],
            out_specs=tile,
            scratch_shapes=[pltpu.VMEM((D_MODEL, 2 * D_EXPERT), BF16), pltpu.VMEM((D_EXPERT, D_MODEL), BF16)]),
        out_shape=jax.ShapeDtypeStruct(xs.shape, F32),
        input_output_aliases={2: 0},
        compiler_params=_params("arbitrary"),
        name="moe_experts",
    )(tile_expert, n_active, xs, w_gate, w_up, w_down)


def _combine_kernel(pos_ref, h_ref, route_ref, gn_ref, ys_ref, hout_ref, hnorm_ref, buf1, buf2, sem):
    rows = h_ref.shape[0]
    bufs = (buf1, buf2)

    def copy(r, k):
        return pltpu.make_async_copy(_token_tile(ys_ref, pos_ref[0, k, r]), _token_tile(bufs[k], r), sem)

    def issue(r, c):
        copy(r, 0).start(priority=0)
        copy(r, 1).start(priority=1)
        return c

    def drain(r, c):
        copy(r, 0).wait()
        copy(r, 1).wait()
        return c

    lax.fori_loop(0, rows, issue, 0)
    lax.fori_loop(0, rows, drain, 0)
    route = route_ref[...]
    h = (h_ref[...] + route[:, R_W1:R_W1 + 1] * _from_token_tiles(buf1, rows)
         + route[:, R_W2:R_W2 + 1] * _from_token_tiles(buf2, rows))
    hout_ref[...] = h
    hnorm_ref[...] = _rms(h, gn_ref[...])


def _combine(pos, h, route, gnext, ys, tm=256):
    t = h.shape[0]
    row = lambda i: (i, 0)
    return pl.pallas_call(
        _combine_kernel,
        grid=(t // tm,),
        in_specs=[pl.BlockSpec((1, 2, tm), lambda i: (i, 0, 0), memory_space=pltpu.SMEM),
                  pl.BlockSpec((tm, D_MODEL), row), pl.BlockSpec((tm, ROUTER_LANES), row),
                  pl.BlockSpec((1, D_MODEL), lambda i: (0, 0)),
                  pl.BlockSpec(memory_space=pl.ANY)],
        out_specs=[pl.BlockSpec((tm, D_MODEL), row), pl.BlockSpec((tm, D_MODEL), row)],
        out_shape=[jax.ShapeDtypeStruct((t, D_MODEL), F32), jax.ShapeDtypeStruct((t, D_MODEL), F32)],
        scratch_shapes=[pltpu.VMEM((tm * TOKEN_ROWS, LANES), F32), pltpu.VMEM((tm * TOKEN_ROWS, LANES), F32),
                        pltpu.SemaphoreType.DMA(())],
        compiler_params=_params("arbitrary"),
        name="moe_combine",
    )(_pos_blocks(pos, tm), h, route, gnext, ys)


def _moe(h, hn_tt, route, counts, w_gate, w_up, w_down, gnext):
    t = h.shape[0]
    pos, tile_expert, n_active, plan = _dispatch_plan(route, counts, t)
    xs = _dispatch(plan, pos, hn_tt, t)
    ys = _experts(tile_expert, n_active, xs, w_gate, w_up, w_down)
    return _combine(pos, h, route, gnext, ys)


def _dot3(a, b):
    a_hi = a.astype(BF16)
    a_lo = (a - a_hi.astype(F32)).astype(BF16)
    b_hi = b.astype(BF16)
    b_lo = (b - b_hi.astype(F32)).astype(BF16)
    return _dot(a_hi, b_hi) + (_dot(a_hi, b_lo) + _dot(a_lo, b_hi))


def _ssm_powers(lam_re, lam_im, log_step, lags):
    lam_re = jnp.minimum(lam_re, np.float32(-1e-4))
    step = jnp.exp(log_step)
    dec = jnp.exp(lags * (lam_re * step))
    ang = lags * (lam_im * step)
    return dec * jnp.cos(ang), dec * jnp.sin(ang)


def _ssm_operators(lamr_ref, lamc_ref, bre_ref, bim_ref, cre_ref, cim_ref, r_s, wd_s, vd_s):
    L, ns = SSM_CHUNK, SSM_BLOCK_STATE
    lam = lamr_ref[0]
    lre, lim, lst = lam[0:1], lam[1:2], lam[2:3]
    lag_rev = (L - 1 - lax.broadcasted_iota(jnp.int32, (L, 1), 0)).astype(F32)
    pr_re, pr_im = _ssm_powers(lre, lim, lst, lag_rev)
    ab_re, ab_im = _ssm_powers(lre, lim, lst, np.float32(1.0))
    lam_re = jnp.minimum(lre, np.float32(-1e-4))
    denom = lam_re * lam_re + lim * lim
    num_re = ab_re - 1.0
    f_re = (num_re * lam_re + ab_im * lim) / denom
    f_im = (ab_im * lam_re - num_re * lim) / denom
    bre, bim = bre_ref[0], bim_ref[0]
    bb_re = f_re * bre - f_im * bim
    bb_im = f_re * bim + f_im * bre
    cre, cim = cre_ref[0], cim_ref[0]
    for s in range(L):
        a_re, a_im = pr_re[s:s + 1], pr_im[s:s + 1]
        w_re = a_re * bb_re - a_im * bb_im
        w_im = a_re * bb_im + a_im * bb_re
        rows = slice(s * LANES, (s + 1) * LANES)
        wd_s[rows, 0:ns] = w_re.astype(BF16)
        wd_s[rows, ns:2 * ns] = w_im.astype(BF16)
        r_s[rows, :] = (_dot3(w_re, cre) - _dot3(w_im, cim)).astype(BF16)
    lamc = lamc_ref[0]
    lag1 = (1 + lax.broadcasted_iota(jnp.int32, (1, L), 1)).astype(F32)
    p1_re, p1_im = _ssm_powers(lamc[:, 0:1], lamc[:, 1:2], lamc[:, 2:3], lag1)
    for t in range(L):
        a_re, a_im = p1_re[:, t:t + 1], p1_im[:, t:t + 1]
        cols = slice(t * LANES, (t + 1) * LANES)
        vd_s[0:ns, cols] = (cre * a_re - cim * a_im).astype(BF16)
        vd_s[ns:2 * ns, cols] = (-(cre * a_im + cim * a_re)).astype(BF16)
    return _ssm_powers(lre, lim, lst, np.float32(L))


def _ssm_kernel(nch, x_ref, lamr_ref, lamc_ref, bre_ref, bim_ref, cre_ref, cim_ref, d_ref, o_ref,
                r_s, wd_s, vd_s, xcat, st, xprev):
    nb = x_ref.shape[0]
    ns = SSM_BLOCK_STATE
    ar, ai = _ssm_operators(lamr_ref, lamc_ref, bre_ref, bim_ref, cre_ref, cim_ref, r_s, wd_s, vd_s)
    for s in range(SSM_CHUNK):
        for b in range(nb):
            xs = x_ref[b, pl.ds(s, nch, stride=SSM_CHUNK), :]
            xcat[b * nch:(b + 1) * nch, s * LANES:(s + 1) * LANES] = xs.astype(BF16)
    st[...] = _dot(xcat[...], wd_s[...])

    def step(k, carry):
        out = []
        for b in range(nb):
            xr, xi = carry[b]
            row = b * nch + k
            sr = st[pl.ds(row, 1), 0:ns]
            si = st[pl.ds(row, 1), ns:2 * ns]
            st[pl.ds(row, 1), 0:ns] = xr
            st[pl.ds(row, 1), ns:2 * ns] = xi
            out.append((ar * xr - ai * xi + sr, ar * xi + ai * xr + si))
        return tuple(out)

    zero = jnp.zeros((1, ns), F32)
    lax.fori_loop(0, nch, step, tuple((zero, zero) for _ in range(nb)))
    xprev[...] = st[...].astype(BF16)
    dskip = d_ref[...]
    for t in range(SSM_CHUNK):
        y = (_dot(xcat[:, :(t + 1) * LANES], r_s[(SSM_CHUNK - 1 - t) * LANES:, :])
             + _dot(xprev[...], vd_s[:, t * LANES:(t + 1) * LANES]))
        for b in range(nb):
            u = x_ref[b, pl.ds(t, nch, stride=SSM_CHUNK), :]
            o_ref[b, pl.ds(t, nch, stride=SSM_CHUNK), :] = _gelu_tanh(y[b * nch:(b + 1) * nch, :] + dskip * u)


def _ssm(x3, params, dskip):
    nb, seq, _ = x3.shape
    nch = seq // SSM_CHUNK
    rows = nb * nch
    ns = SSM_BLOCK_STATE
    blk = pl.BlockSpec((nb, seq, LANES), lambda g: (0, 0, g), pipeline_mode=pl.Buffered(1))
    per = lambda shape: pl.BlockSpec((1,) + shape, lambda g: (g, 0, 0))
    return pl.pallas_call(
        functools.partial(_ssm_kernel, nch),
        grid=(SSM_BLOCKS,),
        in_specs=[blk, per((3, ns)), per((ns, 3)), per((LANES, ns)), per((LANES, ns)),
                  per((ns, LANES)), per((ns, LANES)), pl.BlockSpec((1, LANES), lambda g: (0, g))],
        out_specs=blk,
        out_shape=jax.ShapeDtypeStruct(x3.shape, F32),
        scratch_shapes=[pltpu.VMEM((SSM_CHUNK * LANES, LANES), BF16),
                        pltpu.VMEM((SSM_CHUNK * LANES, 2 * ns), BF16),
                        pltpu.VMEM((2 * ns, SSM_CHUNK * LANES), BF16),
                        pltpu.VMEM((rows, SSM_CHUNK * LANES), BF16),
                        pltpu.VMEM((rows, 2 * ns), F32),
                        pltpu.VMEM((rows, 2 * ns), BF16)],
        compiler_params=_params("parallel"),
        name="ssm",
    )(x3, *params, dskip)


def _ssm_params(lambda_re, lambda_im, log_step, b_re, b_im, c_re, c_im):
    nb, ng, ns = SSM_BLOCKS, SSM_LANE_GROUPS, SSM_BLOCK_STATE
    lam = jnp.stack([lambda_re.astype(F32), lambda_im.astype(F32),
                     jnp.broadcast_to(log_step.astype(F32)[:, None], lambda_re.shape)])
    lam_rows = jnp.transpose(lam.reshape(3, nb, ns), (1, 0, 2))
    lam_cols = jnp.transpose(lam_rows, (0, 2, 1))
    eye = jnp.eye(ng, dtype=F32)

    def blockdiag(m):
        a, b = m.shape[-2:]
        out = jnp.einsum('ngab,gh->ngahb', m.astype(F32).reshape(nb, ng, a, b), eye)
        return out.reshape(nb, ng * a, ng * b)

    bt = lambda m: blockdiag(jnp.swapaxes(m, -1, -2))
    return lam_rows, lam_cols, bt(b_re), bt(b_im), bt(c_re), bt(c_im)


def _router_tables(w_group, b_group, w_expert, b_expert):
    w = jnp.concatenate([w_group.astype(F32),
                         jnp.transpose(w_expert.astype(F32), (1, 0, 2)).reshape(D_MODEL, N_EXPERTS)], axis=1)
    b = jnp.concatenate([b_group.astype(F32), b_expert.astype(F32).reshape(N_EXPERTS)])
    pad = ROUTER_LANES - w.shape[1]
    return jnp.pad(w, ((0, 0), (0, pad))), jnp.pad(b, (0, pad)).reshape(1, ROUTER_LANES)


def kernel(x, norm_mix, norm_ffn, norm_final, attn_w_qkv, attn_w_o, attn_rel_bias, ssm_lambda_re, ssm_lambda_im, ssm_log_step, ssm_b_re, ssm_b_im, ssm_c_re, ssm_c_im, ssm_d, ssm_w_out, moe_w_group_router, moe_b_group_router, moe_w_expert_router, moe_b_expert_router, moe_w_gate, moe_w_up, moe_w_down):
    batch, seq, d = x.shape
    t = batch * seq
    row = lambda v: v.astype(F32).reshape(1, d)
    h = x.astype(F32).reshape(t, d)

    qkv = _qkv(h, row(norm_mix[0]), attn_w_qkv[0].astype(F32))
    o = _attention(qkv, _rel_bias_rows(attn_rel_bias[0]), batch, seq)
    wr, br = _router_tables(moe_w_group_router[0], moe_b_group_router[0],
                            moe_w_expert_router[0], moe_b_expert_router[0])
    h, hn_tt, route, counts = _post(o, h, attn_w_o[0].astype(F32), row(norm_ffn[0]), wr, br, glu=False)
    h, hn32 = _moe(h, hn_tt, route, counts, moe_w_gate[0], moe_w_up[0], moe_w_down[0], row(norm_mix[1]))

    ssm_params = _ssm_params(ssm_lambda_re[0], ssm_lambda_im[0], ssm_log_step[0],
                             ssm_b_re[0], ssm_b_im[0], ssm_c_re[0], ssm_c_im[0])
    y = _ssm(hn32.reshape(batch, seq, d), ssm_params, row(ssm_d[0]))
    wr, br = _router_tables(moe_w_group_router[1], moe_b_group_router[1],
                            moe_w_expert_router[1], moe_b_expert_router[1])
    h, hn_tt, route, counts = _post(y.reshape(t, d), h, ssm_w_out[0].astype(F32), row(norm_ffn[1]), wr, br, glu=True)
    _, out = _moe(h, hn_tt, route, counts, moe_w_gate[1], moe_w_up[1], moe_w_down[1], row(norm_final))
    return out.reshape(batch, seq, d).astype(x.dtype)
```

```python
import functools

import jax
import jax.numpy as jnp
import numpy as np
from jax import lax
from jax.experimental import pallas as pl
from jax.experimental.pallas import tpu as pltpu

F32 = jnp.float32
BF16 = jnp.bfloat16

D_MODEL = 1024
CHUNK = 64
LOOKBACK_CHUNKS = 8
BAND = (LOOKBACK_CHUNKS + 1) * CHUNK
N_HEADS = 16
HEAD_DIM = D_MODEL // N_HEADS
MAX_REL = 128
SSM_GROUP = 16
SSM_GROUPS = D_MODEL // SSM_GROUP
SSM_STATE = 64
N_EXPERT_GROUPS = 4
EXPERTS_PER_GROUP = 8
N_EXPERTS = N_EXPERT_GROUPS * EXPERTS_PER_GROUP
D_EXPERT = D_MODEL // 4
RMS_EPS = 1e-6
NEG_BIG = -1e30

LANES = 128
SUBLANES = 8
MXU_DIM = 256
VMEM_LIMIT = 56 * 1024 * 1024

HEADS_PER_GROUP = MXU_DIM // HEAD_DIM
N_HEAD_GROUPS = N_HEADS // HEADS_PER_GROUP
ATTN_QBLOCK = LOOKBACK_CHUNKS * CHUNK
REL_PAD = -(-(2 * MAX_REL + 1) // LANES) * LANES
BIAS_WIDTH = -(-(BAND + CHUNK - 1) // LANES) * LANES
ROUTER_LANES = LANES
ROUTER_EXPERT_LANE0 = N_EXPERT_GROUPS
SSM_CHUNK = 16
SSM_LANE_GROUPS = LANES // SSM_GROUP
SSM_BLOCKS = D_MODEL // LANES
SSM_BLOCK_STATE = SSM_LANE_GROUPS * SSM_STATE
TOKEN_ROWS = D_MODEL // LANES
MOE_TILE = 256
R_GROUP, R_RANK = range(2)


def _dot(a, b):
    return jnp.dot(a, b, preferred_element_type=F32)


def _rms(x, g):
    return x * lax.rsqrt(jnp.mean(x * x, axis=-1, keepdims=True) + RMS_EPS) * g


def _sigmoid(x):
    return 1.0 / (1.0 + jnp.exp(-x))


def _gelu_tanh(x):
    c = np.float32(np.sqrt(2.0 / np.pi))
    return 0.5 * x * (1.0 + jnp.tanh(c * (x + np.float32(0.044715) * (x * x * x))))


def _params(*sem):
    return pltpu.CompilerParams(dimension_semantics=sem, vmem_limit_bytes=VMEM_LIMIT)


def _resident(shape):
    return pl.BlockSpec(shape, lambda *_: (0,) * len(shape), pipeline_mode=pl.Buffered(1))


def _cast_once(w_ref, wb):
    @pl.when(pl.program_id(0) == 0)
    def _():
        wb[...] = w_ref[...].astype(BF16)


def _qkv_kernel(x_ref, g_ref, w_ref, o_ref, wb):
    _cast_once(w_ref, wb)
    xn = _rms(x_ref[...], g_ref[...]).astype(BF16)
    for c in range(3):
        acc = _dot(xn, wb[:, c * D_MODEL:(c + 1) * D_MODEL])
        if c == 0:
            acc = acc * np.float32(HEAD_DIM ** -0.5)
        o_ref[:, c * D_MODEL:(c + 1) * D_MODEL] = acc.astype(BF16)


def _qkv(x, g, w, tm=512):
    t = x.shape[0]
    return pl.pallas_call(
        _qkv_kernel,
        grid=(t // tm,),
        in_specs=[pl.BlockSpec((tm, D_MODEL), lambda i: (i, 0)),
                  pl.BlockSpec((1, D_MODEL), lambda i: (0, 0)),
                  _resident(w.shape)],
        out_specs=pl.BlockSpec((tm, 3 * D_MODEL), lambda i: (i, 0)),
        out_shape=jax.ShapeDtypeStruct((t, 3 * D_MODEL), BF16),
        scratch_shapes=[pltpu.VMEM(w.shape, BF16)],
        compiler_params=_params("arbitrary"),
        name="qkv",
    )(x, g, w)


def _expand_rel_bias(rel_ref, bias):
    rel = rel_ref[0]
    r1 = rel.astype(BF16)
    r2 = (rel - r1.astype(F32)).astype(BF16)
    r3 = (rel - r1.astype(F32) - r2.astype(F32)).astype(BF16)
    j = lax.broadcasted_iota(jnp.int32, (REL_PAD, BIAS_WIDTH), 1)
    src = jnp.clip(BAND - 1 - j, -MAX_REL, MAX_REL) + MAX_REL
    sel = jnp.where(lax.broadcasted_iota(jnp.int32, (REL_PAD, BIAS_WIDTH), 0) == src, 1.0, 0.0).astype(BF16)
    u = (_dot(r1, sel) + _dot(r2, sel)) + _dot(r3, sel)
    for h in range(HEADS_PER_GROUP):
        rows = jnp.broadcast_to(u[h:h + 1, :], (CHUNK, BIAS_WIDTH))
        rows = pltpu.roll(rows, BIAS_WIDTH - (CHUNK - 1), 1, stride=1, stride_axis=0)
        bias[h * CHUNK:(h + 1) * CHUNK, :] = rows[:, :BAND]


def _attn_kernel(q_ref, kp_ref, kc_ref, vp_ref, vc_ref, rel_ref, o_ref, kk, vv, bias):
    qb = pl.program_id(2)

    @pl.when(qb == 0)
    def _():
        _expand_rel_bias(rel_ref, bias)

    kk[0:ATTN_QBLOCK, :] = kp_ref[...]
    kk[ATTN_QBLOCK:, :] = kc_ref[...]
    vv[0:ATTN_QBLOCK, :] = vp_ref[...]
    vv[ATTN_QBLOCK:, :] = vc_ref[...]
    lane_head = lax.broadcasted_iota(jnp.int32, (CHUNK, MXU_DIM), 1) // HEAD_DIM
    col = lax.broadcasted_iota(jnp.int32, (1, BAND), 1)
    for j in range(ATTN_QBLOCK // CHUNK):
        qj = q_ref[j * CHUNK:(j + 1) * CHUNK, :]
        lhs = jnp.concatenate(
            [jnp.where(lane_head == h, qj, jnp.zeros_like(qj)) for h in range(HEADS_PER_GROUP)], axis=0)
        kwin = kk[j * CHUNK:j * CHUNK + BAND, :]
        vwin = vv[j * CHUNK:j * CHUNK + BAND, :]
        s = lax.dot_general(lhs, kwin, (((1,), (1,)), ((), ())), preferred_element_type=F32)
        valid = (col + j * CHUNK >= ATTN_QBLOCK) | (qb > 0)
        s = jnp.where(valid, s + bias[...], NEG_BIG)
        m = jnp.max(s, axis=-1, keepdims=True)
        p = jnp.exp(s - m)
        l = jnp.sum(p, axis=-1, keepdims=True)
        o_all = _dot(p.astype(BF16), vwin) * (1.0 / l)
        o = jnp.zeros((CHUNK, MXU_DIM), F32)
        for h in range(HEADS_PER_GROUP):
            o = o + jnp.where(lane_head == h, o_all[h * CHUNK:(h + 1) * CHUNK, :], 0.0)
        o_ref[j * CHUNK:(j + 1) * CHUNK, :] = o.astype(BF16)


def _attention(qkv, rel, batch, seq):
    nqb = seq // ATTN_QBLOCK
    kcol = D_MODEL // MXU_DIM
    blk = (ATTN_QBLOCK, MXU_DIM)
    cur = lambda off: (lambda b, g, i: (b * nqb + i, off + g))
    prev = lambda off: (lambda b, g, i: (b * nqb + jnp.maximum(i - 1, 0), off + g))
    return pl.pallas_call(
        _attn_kernel,
        grid=(batch, N_HEAD_GROUPS, nqb),
        in_specs=[pl.BlockSpec(blk, cur(0)),
                  pl.BlockSpec(blk, prev(kcol)), pl.BlockSpec(blk, cur(kcol)),
                  pl.BlockSpec(blk, prev(2 * kcol)), pl.BlockSpec(blk, cur(2 * kcol)),
                  pl.BlockSpec((1, SUBLANES, REL_PAD), lambda b, g, i: (g, 0, 0))],
        out_specs=pl.BlockSpec(blk, cur(0)),
        out_shape=jax.ShapeDtypeStruct((batch * seq, D_MODEL), BF16),
        scratch_shapes=[pltpu.VMEM((2 * ATTN_QBLOCK, MXU_DIM), BF16),
                        pltpu.VMEM((2 * ATTN_QBLOCK, MXU_DIM), BF16),
                        pltpu.VMEM((HEADS_PER_GROUP * CHUNK, BAND), F32)],
        compiler_params=_params("parallel", "parallel", "arbitrary"),
        name="attn",
    )(qkv, qkv, qkv, qkv, qkv, rel)


def _rel_bias_rows(rel_bias):
    rel = rel_bias.astype(F32).reshape(N_HEAD_GROUPS, HEADS_PER_GROUP, 2 * MAX_REL + 1)
    return jnp.pad(rel, ((0, 0), (0, SUBLANES - HEADS_PER_GROUP), (0, REL_PAD - (2 * MAX_REL + 1))))


def _router_logits(hn, wr_ref, br_ref):
    a_hi = hn.astype(BF16)
    a_lo = (hn - a_hi.astype(F32)).astype(BF16)
    w = wr_ref[...]
    w_hi = w.astype(BF16)
    w_lo = (w - w_hi.astype(F32)).astype(BF16)
    return _dot(a_hi, w_hi) + (_dot(a_hi, w_lo) + _dot(a_lo, w_hi)) + br_ref[...]


def _route_group(lg, base):
    rows = lg.shape[0]
    lane = lax.broadcasted_iota(jnp.int32, lg.shape, 1)
    gl = jnp.where(lane < N_EXPERT_GROUPS, lg, np.float32(-np.inf))
    gmax = jnp.max(gl, axis=-1, keepdims=True)
    gidx = jnp.min(jnp.where(gl == gmax, lane, np.int32(ROUTER_LANES)), axis=-1, keepdims=True)
    og = lane == gidx
    tri = (lax.broadcasted_iota(jnp.int32, (rows, rows), 0)
           > lax.broadcasted_iota(jnp.int32, (rows, rows), 1))
    tri = jnp.where(tri, 1.0, 0.0).astype(BF16)
    ogf = jnp.where(og, 1.0, 0.0)
    rank = jnp.sum(jnp.where(og, base + _dot(tri, ogf.astype(BF16)), 0.0), axis=-1, keepdims=True)
    rec = jnp.where(lane == R_GROUP, gidx.astype(F32), jnp.where(lane == R_RANK, rank, 0.0))
    return rec, base + jnp.sum(ogf, axis=0, keepdims=True)


def _route_experts(lg, g):
    lane = lax.broadcasted_iota(jnp.int32, lg.shape, 1)
    ninf = np.float32(-np.inf)
    big = np.int32(ROUTER_LANES)
    gmask = lane < N_EXPERT_GROUPS
    gmax = jnp.max(jnp.where(gmask, lg, ninf), axis=-1, keepdims=True)
    ge = jnp.where(gmask, jnp.exp(lg - gmax), 0.0)
    gprob = jnp.sum(jnp.where(lane == g, ge, 0.0), axis=-1, keepdims=True) / jnp.sum(ge, axis=-1, keepdims=True)
    lo = ROUTER_EXPERT_LANE0 + EXPERTS_PER_GROUP * g
    el = jnp.where((lane >= lo) & (lane < lo + EXPERTS_PER_GROUP), lg, ninf)
    l1 = jnp.max(el, axis=-1, keepdims=True)
    i1 = jnp.min(jnp.where(el == l1, lane, big), axis=-1, keepdims=True)
    el2 = jnp.where(lane == i1, ninf, el)
    l2 = jnp.max(el2, axis=-1, keepdims=True)
    i2 = jnp.min(jnp.where(el2 == l2, lane, big), axis=-1, keepdims=True)
    t = jnp.exp(l2 - l1)
    w1 = gprob / (1.0 + t)
    return i1, w1, i2, w1 * t


def _to_token_tiles(ref, x):
    rows = x.shape[0]
    for j in range(TOKEN_ROWS):
        ref[pl.ds(j, rows, stride=TOKEN_ROWS), :] = x[:, j * LANES:(j + 1) * LANES]


def _from_token_tiles(ref, rows):
    return jnp.concatenate([ref[pl.ds(j, rows, stride=TOKEN_ROWS), :] for j in range(TOKEN_ROWS)], axis=1)


def _post_kernel(glu, y_ref, h_ref, w_ref, g_ref, wr_ref, br_ref, hout_ref, hntt_ref, route_ref, cnt_ref, wb):
    @pl.when(pl.program_id(0) == 0)
    def _():
        cnt_ref[...] = jnp.zeros_like(cnt_ref)

    _cast_once(w_ref, wb)
    y = y_ref[...].astype(BF16)
    if glu:
        mix = _dot(y, wb[:, :D_MODEL]) * _sigmoid(_dot(y, wb[:, D_MODEL:]))
    else:
        mix = _dot(y, wb[...])
    h = h_ref[...] + mix
    hout_ref[...] = h
    hn = _rms(h, g_ref[...])
    _to_token_tiles(hntt_ref, hn)
    route_ref[...], cnt_ref[...] = _route_group(_router_logits(hn, wr_ref, br_ref), cnt_ref[...])


def _post(y, h, w, g, wr, br, glu, tm=512):
    t = h.shape[0]
    row = lambda i: (i, 0)
    fixed = lambda i: (0, 0)
    return pl.pallas_call(
        functools.partial(_post_kernel, glu),
        grid=(t // tm,),
        in_specs=[pl.BlockSpec((tm, D_MODEL), row), pl.BlockSpec((tm, D_MODEL), row),
                  _resident(w.shape), pl.BlockSpec((1, D_MODEL), fixed),
                  pl.BlockSpec((D_MODEL, ROUTER_LANES), fixed), pl.BlockSpec((1, ROUTER_LANES), fixed)],
        out_specs=[pl.BlockSpec((tm, D_MODEL), row), pl.BlockSpec((tm * TOKEN_ROWS, LANES), row),
                   pl.BlockSpec((tm, ROUTER_LANES), row), pl.BlockSpec((1, ROUTER_LANES), fixed)],
        out_shape=[jax.ShapeDtypeStruct((t, D_MODEL), F32), jax.ShapeDtypeStruct((t * TOKEN_ROWS, LANES), F32),
                   jax.ShapeDtypeStruct((t, ROUTER_LANES), F32), jax.ShapeDtypeStruct((1, ROUTER_LANES), F32)],
        scratch_shapes=[pltpu.VMEM(w.shape, BF16)],
        compiler_params=_params("arbitrary"),
        name="post_glu" if glu else "post_attn",
    )(y, h, w, g, wr, br)


def _moe_tiles(t):
    return t // MOE_TILE + N_EXPERT_GROUPS


def _dispatch_plan(route, counts, t):
    g = route[:, R_GROUP].astype(jnp.int32)
    rank = route[:, R_RANK].astype(jnp.int32)
    cnt = counts[0, :N_EXPERT_GROUPS].astype(jnp.int32)
    padded = ((cnt + (MOE_TILE - 1)) // MOE_TILE) * MOE_TILE
    ends = jnp.cumsum(padded)
    off = ends - padded
    onehot = g[:, None] == jnp.arange(N_EXPERT_GROUPS, dtype=jnp.int32)
    pos = jnp.sum(jnp.where(onehot, off, 0), axis=-1) + rank
    ntiles = _moe_tiles(t)
    tile_start = jnp.arange(ntiles, dtype=jnp.int32) * MOE_TILE
    tile_group = jnp.minimum(jnp.sum((tile_start[:, None] >= ends[None, :]).astype(jnp.int32), axis=1),
                             N_EXPERT_GROUPS - 1)
    n_active = (ends[-1] // MOE_TILE).reshape(1)
    plan = jnp.concatenate([off + cnt, padded - cnt, n_active])
    return pos, tile_group, n_active, plan


def _token_tile(ref, i):
    return ref.at[pl.ds(pl.multiple_of(i * TOKEN_ROWS, TOKEN_ROWS), TOKEN_ROWS), :]


def _zero_fill(plan_ref, xs_ref, zbuf, zsem, ntiles):
    nseg = N_EXPERT_GROUPS
    zbuf[...] = jnp.zeros_like(zbuf)
    half = MOE_TILE // 2
    sizes = [1 << b for b in reversed(range(half.bit_length()))]

    def chunk(first, size):
        return pltpu.make_async_copy(zbuf.at[pl.ds(0, size * TOKEN_ROWS), :],
                                     xs_ref.at[pl.ds(pl.multiple_of(first * TOKEN_ROWS, TOKEN_ROWS),
                                                     size * TOKEN_ROWS), :], zsem)

    def pads(wait):
        def body(e, c):
            first = plan_ref[e]
            n = plan_ref[nseg + e]
            for size in sizes:
                hit = (n & size) != 0

                @pl.when(hit)
                def _():
                    cp = chunk(first, size)
                    cp.wait() if wait else cp.start()

                first = first + jnp.where(hit, size, 0)
            return c
        lax.fori_loop(0, nseg, body, 0)

    def tail(wait):
        def body(i, c):
            for k in range(2):
                cp = chunk(i * MOE_TILE + k * half, half)
                cp.wait() if wait else cp.start()
            return c
        lax.fori_loop(plan_ref[2 * nseg], ntiles, body, 0)

    pads(False)
    tail(False)
    pads(True)
    tail(True)


def _dispatch_kernel(ntiles, plan_ref, pos_ref, hn_ref, xs_ref, sem, zbuf, zsem):
    rows = hn_ref.shape[0] // TOKEN_ROWS

    @pl.when(pl.program_id(0) == 0)
    def _():
        _zero_fill(plan_ref, xs_ref, zbuf, zsem, ntiles)

    def copy(r):
        return pltpu.make_async_copy(_token_tile(hn_ref, r), _token_tile(xs_ref, pos_ref[0, 0, r]), sem)

    def issue(r, c):
        copy(2 * r).start(priority=0)
        copy(2 * r + 1).start(priority=1)
        return c

    def drain(r, c):
        copy(r).wait()
        return c

    lax.fori_loop(0, rows // 2, issue, 0)
    lax.fori_loop(0, rows, drain, 0)


def _pos_blocks(pos, tm):
    return pos.reshape(pos.shape[0] // tm, 1, tm)


def _dispatch(plan, pos, hn_tt, t, tm=1024):
    ntiles = _moe_tiles(t)
    nrows = ntiles * MOE_TILE * TOKEN_ROWS
    return pl.pallas_call(
        functools.partial(_dispatch_kernel, ntiles),
        grid_spec=pltpu.PrefetchScalarGridSpec(
            num_scalar_prefetch=1,
            grid=(t // tm,),
            in_specs=[pl.BlockSpec((1, 1, tm), lambda i, plan: (i, 0, 0), memory_space=pltpu.SMEM),
                      pl.BlockSpec((tm * TOKEN_ROWS, LANES), lambda i, plan: (i, 0))],
            out_specs=pl.BlockSpec(memory_space=pl.ANY),
            scratch_shapes=[pltpu.SemaphoreType.DMA(()),
                            pltpu.VMEM((MOE_TILE // 2 * TOKEN_ROWS, LANES), F32),
                            pltpu.SemaphoreType.DMA(())]),
        out_shape=jax.ShapeDtypeStruct((nrows, LANES), F32),
        compiler_params=_params("arbitrary"),
        name="moe_dispatch",
    )(plan, _pos_blocks(pos, tm), hn_tt)


def _expert_kernel(tg_ref, na_ref, xs_ref, wr_ref, br_ref, wg_ref, wu_ref, wd_ref, ys_ref, wgu_s, wd_s):
    i = pl.program_id(0)
    g = tg_ref[i]
    new_group = (i == 0) | (tg_ref[jnp.maximum(i - 1, 0)] != g)
    width = EXPERTS_PER_GROUP * D_EXPERT

    @pl.when(i < na_ref[0])
    def _():
        @pl.when(new_group)
        def _():
            for e in range(EXPERTS_PER_GROUP):
                cols = slice(e * D_EXPERT, (e + 1) * D_EXPERT)
                wgu_s[:, cols] = wg_ref[0, 0, e].astype(BF16)
                wgu_s[:, width + e * D_EXPERT:width + (e + 1) * D_EXPERT] = wu_ref[0, 0, e].astype(BF16)
                wd_s[cols, :] = wd_ref[0, 0, e].astype(BF16)

        x = _from_token_tiles(xs_ref, MOE_TILE)
        i1, w1, i2, w2 = _route_experts(_router_logits(x, wr_ref, br_ref), g)
        au = _dot(x.astype(BF16), wgu_s[...])
        first = ROUTER_EXPERT_LANE0 + g * EXPERTS_PER_GROUP
        hid = []
        for e in range(EXPERTS_PER_GROUP):
            a = au[:, e * D_EXPERT:(e + 1) * D_EXPERT]
            u = au[:, width + e * D_EXPERT:width + (e + 1) * D_EXPERT]
            gate = jnp.where(i1 == first + e, w1, 0.0) + jnp.where(i2 == first + e, w2, 0.0)
            hid.append(((a * _sigmoid(a)) * u * gate).astype(BF16))
        _to_token_tiles(ys_ref, _dot(jnp.concatenate(hid, axis=1), wd_s[...]))


def _experts(layer, tile_group, n_active, xs, wr, br, w_gate, w_up, w_down):
    ntiles = tile_group.shape[0]
    last = lambda i, na: jnp.minimum(i, na[0] - 1)
    tile = pl.BlockSpec((MOE_TILE * TOKEN_ROWS, LANES), lambda i, tg, na: (last(i, na), 0))
    grouped = lambda w: w.reshape((w.shape[0], N_EXPERT_GROUPS, EXPERTS_PER_GROUP) + w.shape[2:])
    wspec = lambda w: pl.BlockSpec((1, 1, EXPERTS_PER_GROUP) + w.shape[2:],
                                   lambda i, tg, na: (layer, tg[last(i, na)], 0, 0, 0),
                                   pipeline_mode=pl.Buffered(1))
    width = EXPERTS_PER_GROUP * D_EXPERT
    return pl.pallas_call(
        _expert_kernel,
        grid_spec=pltpu.PrefetchScalarGridSpec(
            num_scalar_prefetch=2,
            grid=(ntiles,),
            in_specs=[tile, pl.BlockSpec(wr.shape, lambda i, tg, na: (0, 0)),
                      pl.BlockSpec(br.shape, lambda i, tg, na: (0, 0)),
                      wspec(w_gate), wspec(w_up), wspec(w_down)],
            out_specs=tile,
            scratch_shapes=[pltpu.VMEM((D_MODEL, 2 * width), BF16), pltpu.VMEM((width, D_MODEL), BF16)]),
        out_shape=jax.ShapeDtypeStruct(xs.shape, F32),
        input_output_aliases={2: 0},
        compiler_params=_params("arbitrary"),
        name="moe_experts",
    )(tile_group, n_active, xs, wr, br, grouped(w_gate), grouped(w_up), grouped(w_down))


def _combine_kernel(pos_ref, h_ref, gn_ref, ys_ref, hout_ref, hnorm_ref, buf, sem):
    rows = h_ref.shape[0]

    def copy(r):
        return pltpu.make_async_copy(_token_tile(ys_ref, pos_ref[0, 0, r]), _token_tile(buf, r), sem)

    def issue(r, c):
        copy(2 * r).start(priority=0)
        copy(2 * r + 1).start(priority=1)
        return c

    def drain(r, c):
        copy(r).wait()
        return c

    lax.fori_loop(0, rows // 2, issue, 0)
    lax.fori_loop(0, rows, drain, 0)
    h = h_ref[...] + _from_token_tiles(buf, rows)
    hout_ref[...] = h
    hnorm_ref[...] = _rms(h, gn_ref[...])


def _combine(pos, h, gnext, ys, tm=512):
    t = h.shape[0]
    row = lambda i: (i, 0)
    return pl.pallas_call(
        _combine_kernel,
        grid=(t // tm,),
        in_specs=[pl.BlockSpec((1, 1, tm), lambda i: (i, 0, 0), memory_space=pltpu.SMEM),
                  pl.BlockSpec((tm, D_MODEL), row),
                  pl.BlockSpec((1, D_MODEL), lambda i: (0, 0)),
                  pl.BlockSpec(memory_space=pl.ANY)],
        out_specs=[pl.BlockSpec((tm, D_MODEL), row), pl.BlockSpec((tm, D_MODEL), row)],
        out_shape=[jax.ShapeDtypeStruct((t, D_MODEL), F32), jax.ShapeDtypeStruct((t, D_MODEL), F32)],
        scratch_shapes=[pltpu.VMEM((tm * TOKEN_ROWS, LANES), F32), pltpu.SemaphoreType.DMA(())],
        compiler_params=_params("arbitrary"),
        name="moe_combine",
    )(_pos_blocks(pos, tm), h, gnext, ys)


def _moe(layer, h, hn_tt, route, counts, wr, br, w_gate, w_up, w_down, gnext):
    t = h.shape[0]
    pos, tile_group, n_active, plan = _dispatch_plan(route, counts, t)
    xs = _dispatch(plan, pos, hn_tt, t)
    ys = _experts(layer, tile_group, n_active, xs, wr, br, w_gate, w_up, w_down)
    return _combine(pos, h, gnext, ys)


def _dot3(a, b):
    a_hi = a.astype(BF16)
    a_lo = (a - a_hi.astype(F32)).astype(BF16)
    b_hi = b.astype(BF16)
    b_lo = (b - b_hi.astype(F32)).astype(BF16)
    return _dot(a_hi, b_hi) + (_dot(a_hi, b_lo) + _dot(a_lo, b_hi))


def _ssm_powers(lam_re, lam_im, log_step, lags):
    lam_re = jnp.minimum(lam_re, np.float32(-1e-4))
    step = jnp.exp(log_step)
    dec = jnp.exp(lags * (lam_re * step))
    ang = lags * (lam_im * step)
    return dec * jnp.cos(ang), dec * jnp.sin(ang)


def _ssm_operators(lamr_ref, lamc_ref, bre_ref, bim_ref, cre_ref, cim_ref, r_s, wd_s, vd_s):
    L, ns = SSM_CHUNK, SSM_BLOCK_STATE
    lam = lamr_ref[0]
    lre, lim, lst = lam[0:1], lam[1:2], lam[2:3]
    lag_rev = (L - 1 - lax.broadcasted_iota(jnp.int32, (L, 1), 0)).astype(F32)
    pr_re, pr_im = _ssm_powers(lre, lim, lst, lag_rev)
    ab_re, ab_im = _ssm_powers(lre, lim, lst, np.float32(1.0))
    lam_re = jnp.minimum(lre, np.float32(-1e-4))
    denom = lam_re * lam_re + lim * lim
    num_re = ab_re - 1.0
    f_re = (num_re * lam_re + ab_im * lim) / denom
    f_im = (ab_im * lam_re - num_re * lim) / denom
    bre, bim = bre_ref[0], bim_ref[0]
    bb_re = f_re * bre - f_im * bim
    bb_im = f_re * bim + f_im * bre
    cre, cim = cre_ref[0], cim_ref[0]
    for s in range(L):
        a_re, a_im = pr_re[s:s + 1], pr_im[s:s + 1]
        w_re = a_re * bb_re - a_im * bb_im
        w_im = a_re * bb_im + a_im * bb_re
        rows = slice(s * LANES, (s + 1) * LANES)
        wd_s[rows, 0:ns] = w_re.astype(BF16)
        wd_s[rows, ns:2 * ns] = w_im.astype(BF16)
        r_s[rows, :] = (_dot3(w_re, cre) - _dot3(w_im, cim)).astype(BF16)
    r_s[L * LANES:, :] = jnp.zeros((LANES, LANES), BF16)
    lamc = lamc_ref[0]
    lag1 = (1 + lax.broadcasted_iota(jnp.int32, (1, L), 1)).astype(F32)
    p1_re, p1_im = _ssm_powers(lamc[:, 0:1], lamc[:, 1:2], lamc[:, 2:3], lag1)
    for t in range(L):
        a_re, a_im = p1_re[:, t:t + 1], p1_im[:, t:t + 1]
        cols = slice(t * LANES, (t + 1) * LANES)
        vd_s[0:ns, cols] = (cre * a_re - cim * a_im).astype(BF16)
        vd_s[ns:2 * ns, cols] = (-(cre * a_im + cim * a_re)).astype(BF16)
    return _ssm_powers(lre, lim, lst, np.float32(L))


def _ssm_kernel(nch, x_ref, lamr_ref, lamc_ref, bre_ref, bim_ref, cre_ref, cim_ref, d_ref, o_ref,
                r_s, wd_s, vd_s, xcat, st, xprev):
    nb = x_ref.shape[0]
    ns = SSM_BLOCK_STATE
    ar, ai = _ssm_operators(lamr_ref, lamc_ref, bre_ref, bim_ref, cre_ref, cim_ref, r_s, wd_s, vd_s)
    for s in range(SSM_CHUNK):
        for b in range(nb):
            xs = x_ref[b, pl.ds(s, nch, stride=SSM_CHUNK), :]
            xcat[b * nch:(b + 1) * nch, s * LANES:(s + 1) * LANES] = xs.astype(BF16)
    st[...] = _dot(xcat[...], wd_s[...])

    def step(k, carry):
        out = []
        for b in range(nb):
            xr, xi = carry[b]
            row = b * nch + k
            sr = st[pl.ds(row, 1), 0:ns]
            si = st[pl.ds(row, 1), ns:2 * ns]
            st[pl.ds(row, 1), 0:ns] = xr
            st[pl.ds(row, 1), ns:2 * ns] = xi
            out.append((ar * xr - ai * xi + sr, ar * xi + ai * xr + si))
        return tuple(out)

    zero = jnp.zeros((1, ns), F32)
    lax.fori_loop(0, nch, step, tuple((zero, zero) for _ in range(nb)))
    xprev[...] = st[...].astype(BF16)
    dskip = d_ref[...]
    L = SSM_CHUNK
    for t in range(0, L, 2):
        taps = jnp.concatenate([r_s[(L - 1 - t) * LANES:, :], r_s[(L - 2 - t) * LANES:L * LANES, :]], axis=1)
        y2 = (_dot(xcat[:, :(t + 2) * LANES], taps)
              + _dot(xprev[...], vd_s[:, t * LANES:(t + 2) * LANES]))
        for k in range(2):
            y = y2[:, k * LANES:(k + 1) * LANES]
            for b in range(nb):
                u = x_ref[b, pl.ds(t + k, nch, stride=L), :]
                o_ref[b, pl.ds(t + k, nch, stride=L), :] = _gelu_tanh(y[b * nch:(b + 1) * nch, :] + dskip * u)


def _ssm(x3, params, dskip):
    nb, seq, _ = x3.shape
    nch = seq // SSM_CHUNK
    rows = nb * nch
    ns = SSM_BLOCK_STATE
    blk = pl.BlockSpec((nb, seq, LANES), lambda g: (0, 0, g), pipeline_mode=pl.Buffered(1))
    per = lambda shape: pl.BlockSpec((1,) + shape, lambda g: (g, 0, 0))
    return pl.pallas_call(
        functools.partial(_ssm_kernel, nch),
        grid=(SSM_BLOCKS,),
        in_specs=[blk, per((3, ns)), per((ns, 3)), per((LANES, ns)), per((LANES, ns)),
                  per((ns, LANES)), per((ns, LANES)), pl.BlockSpec((1, LANES), lambda g: (0, g))],
        out_specs=blk,
        out_shape=jax.ShapeDtypeStruct(x3.shape, F32),
        scratch_shapes=[pltpu.VMEM(((SSM_CHUNK + 1) * LANES, LANES), BF16),
                        pltpu.VMEM((SSM_CHUNK * LANES, 2 * ns), BF16),
                        pltpu.VMEM((2 * ns, SSM_CHUNK * LANES), BF16),
                        pltpu.VMEM((rows, SSM_CHUNK * LANES), BF16),
                        pltpu.VMEM((rows, 2 * ns), F32),
                        pltpu.VMEM((rows, 2 * ns), BF16)],
        compiler_params=_params("parallel"),
        name="ssm",
    )(x3, *params, dskip)


def _ssm_params(lambda_re, lambda_im, log_step, b_re, b_im, c_re, c_im):
    nb, ng, ns = SSM_BLOCKS, SSM_LANE_GROUPS, SSM_BLOCK_STATE
    lam = jnp.stack([lambda_re.astype(F32), lambda_im.astype(F32),
                     jnp.broadcast_to(log_step.astype(F32)[:, None], lambda_re.shape)])
    lam_rows = jnp.transpose(lam.reshape(3, nb, ns), (1, 0, 2))
    lam_cols = jnp.transpose(lam_rows, (0, 2, 1))
    eye = jnp.eye(ng, dtype=F32)

    def blockdiag(m):
        a, b = m.shape[-2:]
        out = jnp.einsum('ngab,gh->ngahb', m.astype(F32).reshape(nb, ng, a, b), eye)
        return out.reshape(nb, ng * a, ng * b)

    bt = lambda m: blockdiag(jnp.swapaxes(m, -1, -2))
    return lam_rows, lam_cols, bt(b_re), bt(b_im), bt(c_re), bt(c_im)


def _router_tables(w_group, b_group, w_expert, b_expert):
    w = jnp.concatenate([w_group.astype(F32),
                         jnp.transpose(w_expert.astype(F32), (1, 0, 2)).reshape(D_MODEL, N_EXPERTS)], axis=1)
    b = jnp.concatenate([b_group.astype(F32), b_expert.astype(F32).reshape(N_EXPERTS)])
    pad = ROUTER_LANES - w.shape[1]
    return jnp.pad(w, ((0, 0), (0, pad))), jnp.pad(b, (0, pad)).reshape(1, ROUTER_LANES)


def kernel(x, norm_mix, norm_ffn, norm_final, attn_w_qkv, attn_w_o, attn_rel_bias, ssm_lambda_re, ssm_lambda_im, ssm_log_step, ssm_b_re, ssm_b_im, ssm_c_re, ssm_c_im, ssm_d, ssm_w_out, moe_w_group_router, moe_b_group_router, moe_w_expert_router, moe_b_expert_router, moe_w_gate, moe_w_up, moe_w_down):
    batch, seq, d = x.shape
    t = batch * seq
    row = lambda v: v.astype(F32).reshape(1, d)
    h = x.astype(F32).reshape(t, d)

    qkv = _qkv(h, row(norm_mix[0]), attn_w_qkv[0].astype(F32))
    o = _attention(qkv, _rel_bias_rows(attn_rel_bias[0]), batch, seq)
    wr, br = _router_tables(moe_w_group_router[0], moe_b_group_router[0],
                            moe_w_expert_router[0], moe_b_expert_router[0])
    h, hn_tt, route, counts = _post(o, h, attn_w_o[0].astype(F32), row(norm_ffn[0]), wr, br, glu=False)
    h, hn32 = _moe(0, h, hn_tt, route, counts, wr, br, moe_w_gate, moe_w_up, moe_w_down, row(norm_mix[1]))

    ssm_params = _ssm_params(ssm_lambda_re[0], ssm_lambda_im[0], ssm_log_step[0],
                             ssm_b_re[0], ssm_b_im[0], ssm_c_re[0], ssm_c_im[0])
    y = _ssm(hn32.reshape(batch, seq, d), ssm_params, row(ssm_d[0]))
    wr, br = _router_tables(moe_w_group_router[1], moe_b_group_router[1],
                            moe_w_expert_router[1], moe_b_expert_router[1])
    h, hn_tt, route, counts = _post(y.reshape(t, d), h, ssm_w_out[0].astype(F32), row(norm_ffn[1]), wr, br, glu=True)
    _, out = _moe(1, h, hn_tt, route, counts, wr, br, moe_w_gate, moe_w_up, moe_w_down, row(norm_final))
    return out.reshape(batch, seq, d).astype(x.dtype)
```

```python
import functools

import jax
import jax.numpy as jnp
import numpy as np
from jax import lax
from jax.experimental import pallas as pl
from jax.experimental.pallas import tpu as pltpu

F32 = jnp.float32
BF16 = jnp.bfloat16

D_MODEL = 1024
CHUNK = 64
LOOKBACK_CHUNKS = 8
BAND = (LOOKBACK_CHUNKS + 1) * CHUNK
N_HEADS = 16
HEAD_DIM = D_MODEL // N_HEADS
MAX_REL = 128
SSM_GROUP = 16
SSM_GROUPS = D_MODEL // SSM_GROUP
SSM_STATE = 64
N_EXPERT_GROUPS = 4
EXPERTS_PER_GROUP = 8
N_EXPERTS = N_EXPERT_GROUPS * EXPERTS_PER_GROUP
D_EXPERT = D_MODEL // 4
RMS_EPS = 1e-6
NEG_BIG = -1e30

LANES = 128
SUBLANES = 8
MXU_DIM = 256
VMEM_LIMIT = 56 * 1024 * 1024

HEADS_PER_GROUP = MXU_DIM // HEAD_DIM
N_HEAD_GROUPS = N_HEADS // HEADS_PER_GROUP
ATTN_QBLOCK = LOOKBACK_CHUNKS * CHUNK
REL_PAD = -(-(2 * MAX_REL + 1) // LANES) * LANES
BIAS_WIDTH = -(-(BAND + CHUNK - 1) // LANES) * LANES
ROUTER_LANES = LANES
ROUTER_EXPERT_LANE0 = N_EXPERT_GROUPS
SSM_CHUNK = 16
SSM_LANE_GROUPS = LANES // SSM_GROUP
SSM_BLOCKS = D_MODEL // LANES
SSM_BLOCK_STATE = SSM_LANE_GROUPS * SSM_STATE
TOKEN_ROWS = D_MODEL // LANES
MOE_TILE = 256
R_GROUP, R_RANK = range(2)


def _dot(a, b):
    return jnp.dot(a, b, preferred_element_type=F32)


def _rms(x, g):
    return x * lax.rsqrt(jnp.mean(x * x, axis=-1, keepdims=True) + RMS_EPS) * g


def _sigmoid(x):
    return 1.0 / (1.0 + jnp.exp(-x))


def _gelu_tanh(x):
    c = np.float32(np.sqrt(2.0 / np.pi))
    return 0.5 * x * (1.0 + jnp.tanh(c * (x + np.float32(0.044715) * (x * x * x))))


def _params(*sem):
    return pltpu.CompilerParams(dimension_semantics=sem, vmem_limit_bytes=VMEM_LIMIT)


def _resident(shape):
    return pl.BlockSpec(shape, lambda *_: (0,) * len(shape), pipeline_mode=pl.Buffered(1))


def _cast_once(w_ref, wb):
    @pl.when(pl.program_id(0) == 0)
    def _():
        wb[...] = w_ref[...].astype(BF16)


def _qkv_kernel(x_ref, g_ref, w_ref, o_ref, wb):
    _cast_once(w_ref, wb)
    xn = _rms(x_ref[...], g_ref[...]).astype(BF16)
    for c in range(3):
        acc = _dot(xn, wb[:, c * D_MODEL:(c + 1) * D_MODEL])
        if c == 0:
            acc = acc * np.float32(HEAD_DIM ** -0.5)
        o_ref[:, c * D_MODEL:(c + 1) * D_MODEL] = acc.astype(BF16)


def _qkv(x, g, w, tm=512):
    t = x.shape[0]
    return pl.pallas_call(
        _qkv_kernel,
        grid=(t // tm,),
        in_specs=[pl.BlockSpec((tm, D_MODEL), lambda i: (i, 0)),
                  pl.BlockSpec((1, D_MODEL), lambda i: (0, 0)),
                  _resident(w.shape)],
        out_specs=pl.BlockSpec((tm, 3 * D_MODEL), lambda i: (i, 0)),
        out_shape=jax.ShapeDtypeStruct((t, 3 * D_MODEL), BF16),
        scratch_shapes=[pltpu.VMEM(w.shape, BF16)],
        compiler_params=_params("arbitrary"),
        name="qkv",
    )(x, g, w)


def _expand_rel_bias(rel_ref, bias):
    rel = rel_ref[0]
    r1 = rel.astype(BF16)
    r2 = (rel - r1.astype(F32)).astype(BF16)
    r3 = (rel - r1.astype(F32) - r2.astype(F32)).astype(BF16)
    j = lax.broadcasted_iota(jnp.int32, (REL_PAD, BIAS_WIDTH), 1)
    src = jnp.clip(BAND - 1 - j, -MAX_REL, MAX_REL) + MAX_REL
    sel = jnp.where(lax.broadcasted_iota(jnp.int32, (REL_PAD, BIAS_WIDTH), 0) == src, 1.0, 0.0).astype(BF16)
    u = (_dot(r1, sel) + _dot(r2, sel)) + _dot(r3, sel)
    for h in range(HEADS_PER_GROUP):
        rows = jnp.broadcast_to(u[h:h + 1, :], (CHUNK, BIAS_WIDTH))
        rows = pltpu.roll(rows, BIAS_WIDTH - (CHUNK - 1), 1, stride=1, stride_axis=0)
        bias[h * CHUNK:(h + 1) * CHUNK, :] = rows[:, :BAND]


def _attn_kernel(q_ref, kp_ref, kc_ref, vp_ref, vc_ref, rel_ref, o_ref, kk, vv, bias):
    qb = pl.program_id(2)

    @pl.when(qb == 0)
    def _():
        _expand_rel_bias(rel_ref, bias)

    kk[0:ATTN_QBLOCK, :] = kp_ref[...]
    kk[ATTN_QBLOCK:, :] = kc_ref[...]
    vv[0:ATTN_QBLOCK, :] = vp_ref[...]
    vv[ATTN_QBLOCK:, :] = vc_ref[...]
    lane_head = lax.broadcasted_iota(jnp.int32, (CHUNK, MXU_DIM), 1) // HEAD_DIM
    col = lax.broadcasted_iota(jnp.int32, (1, BAND), 1)

    def chunks(first_block):
        for j in range(ATTN_QBLOCK // CHUNK):
            qj = q_ref[j * CHUNK:(j + 1) * CHUNK, :]
            lhs = jnp.concatenate(
                [jnp.where(lane_head == h, qj, jnp.zeros_like(qj)) for h in range(HEADS_PER_GROUP)], axis=0)
            kwin = kk[j * CHUNK:j * CHUNK + BAND, :]
            vwin = vv[j * CHUNK:j * CHUNK + BAND, :]
            s = lax.dot_general(lhs, kwin, (((1,), (1,)), ((), ())), preferred_element_type=F32) + bias[...]
            if first_block:
                s = jnp.where(col + j * CHUNK >= ATTN_QBLOCK, s, NEG_BIG)
            m = jnp.max(s, axis=-1, keepdims=True)
            p = jnp.exp(s - m)
            l = jnp.sum(p, axis=-1, keepdims=True)
            o_all = _dot(p.astype(BF16), vwin) * (1.0 / l)
            o = jnp.zeros((CHUNK, MXU_DIM), F32)
            for h in range(HEADS_PER_GROUP):
                o = o + jnp.where(lane_head == h, o_all[h * CHUNK:(h + 1) * CHUNK, :], 0.0)
            o_ref[j * CHUNK:(j + 1) * CHUNK, :] = o.astype(BF16)

    @pl.when(qb == 0)
    def _():
        chunks(True)

    @pl.when(qb > 0)
    def _():
        chunks(False)


def _attention(qkv, rel, batch, seq):
    nqb = seq // ATTN_QBLOCK
    kcol = D_MODEL // MXU_DIM
    blk = (ATTN_QBLOCK, MXU_DIM)
    cur = lambda off: (lambda b, g, i: (b * nqb + i, off + g))
    prev = lambda off: (lambda b, g, i: (b * nqb + jnp.maximum(i - 1, 0), off + g))
    return pl.pallas_call(
        _attn_kernel,
        grid=(batch, N_HEAD_GROUPS, nqb),
        in_specs=[pl.BlockSpec(blk, cur(0)),
                  pl.BlockSpec(blk, prev(kcol)), pl.BlockSpec(blk, cur(kcol)),
                  pl.BlockSpec(blk, prev(2 * kcol)), pl.BlockSpec(blk, cur(2 * kcol)),
                  pl.BlockSpec((1, SUBLANES, REL_PAD), lambda b, g, i: (g, 0, 0))],
        out_specs=pl.BlockSpec(blk, cur(0)),
        out_shape=jax.ShapeDtypeStruct((batch * seq, D_MODEL), BF16),
        scratch_shapes=[pltpu.VMEM((2 * ATTN_QBLOCK, MXU_DIM), BF16),
                        pltpu.VMEM((2 * ATTN_QBLOCK, MXU_DIM), BF16),
                        pltpu.VMEM((HEADS_PER_GROUP * CHUNK, BAND), F32)],
        compiler_params=_params("parallel", "parallel", "arbitrary"),
        name="attn",
    )(qkv, qkv, qkv, qkv, qkv, rel)


def _rel_bias_rows(rel_bias):
    rel = rel_bias.astype(F32).reshape(N_HEAD_GROUPS, HEADS_PER_GROUP, 2 * MAX_REL + 1)
    return jnp.pad(rel, ((0, 0), (0, SUBLANES - HEADS_PER_GROUP), (0, REL_PAD - (2 * MAX_REL + 1))))


def _router_logits(hn, wr_ref, br_ref):
    a_hi = hn.astype(BF16)
    a_lo = (hn - a_hi.astype(F32)).astype(BF16)
    w = wr_ref[...]
    w_hi = w.astype(BF16)
    w_lo = (w - w_hi.astype(F32)).astype(BF16)
    return _dot(a_hi, w_hi) + (_dot(a_hi, w_lo) + _dot(a_lo, w_hi)) + br_ref[...]


def _route_group(lg, base):
    rows = lg.shape[0]
    lane = lax.broadcasted_iota(jnp.int32, lg.shape, 1)
    gl = jnp.where(lane < N_EXPERT_GROUPS, lg, np.float32(-np.inf))
    gmax = jnp.max(gl, axis=-1, keepdims=True)
    gidx = jnp.min(jnp.where(gl == gmax, lane, np.int32(ROUTER_LANES)), axis=-1, keepdims=True)
    og = lane == gidx
    tri = (lax.broadcasted_iota(jnp.int32, (rows, rows), 0)
           > lax.broadcasted_iota(jnp.int32, (rows, rows), 1))
    tri = jnp.where(tri, 1.0, 0.0).astype(BF16)
    ogf = jnp.where(og, 1.0, 0.0)
    rank = jnp.sum(jnp.where(og, base + _dot(tri, ogf.astype(BF16)), 0.0), axis=-1, keepdims=True)
    rec = jnp.where(lane == R_GROUP, gidx.astype(F32), jnp.where(lane == R_RANK, rank, 0.0))
    return rec, base + jnp.sum(ogf, axis=0, keepdims=True)


def _route_experts(lg, g):
    lane = lax.broadcasted_iota(jnp.int32, lg.shape, 1)
    ninf = np.float32(-np.inf)
    big = np.int32(ROUTER_LANES)
    gmask = lane < N_EXPERT_GROUPS
    gmax = jnp.max(jnp.where(gmask, lg, ninf), axis=-1, keepdims=True)
    ge = jnp.where(gmask, jnp.exp(lg - gmax), 0.0)
    gprob = jnp.sum(jnp.where(lane == g, ge, 0.0), axis=-1, keepdims=True) / jnp.sum(ge, axis=-1, keepdims=True)
    lo = ROUTER_EXPERT_LANE0 + EXPERTS_PER_GROUP * g
    el = jnp.where((lane >= lo) & (lane < lo + EXPERTS_PER_GROUP), lg, ninf)
    l1 = jnp.max(el, axis=-1, keepdims=True)
    i1 = jnp.min(jnp.where(el == l1, lane, big), axis=-1, keepdims=True)
    el2 = jnp.where(lane == i1, ninf, el)
    l2 = jnp.max(el2, axis=-1, keepdims=True)
    i2 = jnp.min(jnp.where(el2 == l2, lane, big), axis=-1, keepdims=True)
    t = jnp.exp(l2 - l1)
    w1 = gprob / (1.0 + t)
    return i1, w1, i2, w1 * t


def _to_token_tiles(ref, x):
    rows = x.shape[0]
    for j in range(TOKEN_ROWS):
        ref[pl.ds(j, rows, stride=TOKEN_ROWS), :] = x[:, j * LANES:(j + 1) * LANES]


def _from_token_tiles(ref, rows):
    return jnp.concatenate([ref[pl.ds(j, rows, stride=TOKEN_ROWS), :] for j in range(TOKEN_ROWS)], axis=1)


def _post_kernel(glu, y_ref, h_ref, w_ref, g_ref, wr_ref, br_ref, hout_ref, hntt_ref, route_ref, cnt_ref, wb):
    @pl.when(pl.program_id(0) == 0)
    def _():
        cnt_ref[...] = jnp.zeros_like(cnt_ref)

    _cast_once(w_ref, wb)
    y = y_ref[...].astype(BF16)
    if glu:
        mix = _dot(y, wb[:, :D_MODEL]) * _sigmoid(_dot(y, wb[:, D_MODEL:]))
    else:
        mix = _dot(y, wb[...])
    h = h_ref[...] + mix
    hout_ref[...] = h
    hn = _rms(h, g_ref[...])
    _to_token_tiles(hntt_ref, hn)
    route_ref[...], cnt_ref[...] = _route_group(_router_logits(hn, wr_ref, br_ref), cnt_ref[...])


def _post(y, h, w, g, wr, br, glu, tm=512):
    t = h.shape[0]
    row = lambda i: (i, 0)
    fixed = lambda i: (0, 0)
    return pl.pallas_call(
        functools.partial(_post_kernel, glu),
        grid=(t // tm,),
        in_specs=[pl.BlockSpec((tm, D_MODEL), row), pl.BlockSpec((tm, D_MODEL), row),
                  _resident(w.shape), pl.BlockSpec((1, D_MODEL), fixed),
                  pl.BlockSpec((D_MODEL, ROUTER_LANES), fixed), pl.BlockSpec((1, ROUTER_LANES), fixed)],
        out_specs=[pl.BlockSpec((tm, D_MODEL), row), pl.BlockSpec((tm * TOKEN_ROWS, LANES), row),
                   pl.BlockSpec((tm, ROUTER_LANES), row), pl.BlockSpec((1, ROUTER_LANES), fixed)],
        out_shape=[jax.ShapeDtypeStruct((t, D_MODEL), F32), jax.ShapeDtypeStruct((t * TOKEN_ROWS, LANES), F32),
                   jax.ShapeDtypeStruct((t, ROUTER_LANES), F32), jax.ShapeDtypeStruct((1, ROUTER_LANES), F32)],
        scratch_shapes=[pltpu.VMEM(w.shape, BF16)],
        compiler_params=_params("arbitrary"),
        name="post_glu" if glu else "post_attn",
    )(y, h, w, g, wr, br)


def _moe_tiles(t):
    return t // MOE_TILE + N_EXPERT_GROUPS


def _dispatch_plan(route, counts, t):
    g = route[:, R_GROUP].astype(jnp.int32)
    rank = route[:, R_RANK].astype(jnp.int32)
    cnt = counts[0, :N_EXPERT_GROUPS].astype(jnp.int32)
    padded = ((cnt + (MOE_TILE - 1)) // MOE_TILE) * MOE_TILE
    ends = jnp.cumsum(padded)
    off = ends - padded
    onehot = g[:, None] == jnp.arange(N_EXPERT_GROUPS, dtype=jnp.int32)
    pos = jnp.sum(jnp.where(onehot, off, 0), axis=-1) + rank
    ntiles = _moe_tiles(t)
    tile_start = jnp.arange(ntiles, dtype=jnp.int32) * MOE_TILE
    tile_group = jnp.minimum(jnp.sum((tile_start[:, None] >= ends[None, :]).astype(jnp.int32), axis=1),
                             N_EXPERT_GROUPS - 1)
    n_active = (ends[-1] // MOE_TILE).reshape(1)
    plan = jnp.concatenate([off + cnt, padded - cnt, n_active])
    return pos, tile_group, n_active, plan


def _token_tile(ref, i):
    return ref.at[pl.ds(pl.multiple_of(i * TOKEN_ROWS, TOKEN_ROWS), TOKEN_ROWS), :]


def _zero_fill(plan_ref, xs_ref, zbuf, zsem, ntiles):
    nseg = N_EXPERT_GROUPS
    zbuf[...] = jnp.zeros_like(zbuf)
    half = MOE_TILE // 2
    sizes = [1 << b for b in reversed(range(half.bit_length()))]

    def chunk(first, size):
        return pltpu.make_async_copy(zbuf.at[pl.ds(0, size * TOKEN_ROWS), :],
                                     xs_ref.at[pl.ds(pl.multiple_of(first * TOKEN_ROWS, TOKEN_ROWS),
                                                     size * TOKEN_ROWS), :], zsem)

    def pads(wait):
        def body(e, c):
            first = plan_ref[e]
            n = plan_ref[nseg + e]
            for size in sizes:
                hit = (n & size) != 0

                @pl.when(hit)
                def _():
                    cp = chunk(first, size)
                    cp.wait() if wait else cp.start()

                first = first + jnp.where(hit, size, 0)
            return c
        lax.fori_loop(0, nseg, body, 0)

    def tail(wait):
        def body(i, c):
            for k in range(2):
                cp = chunk(i * MOE_TILE + k * half, half)
                cp.wait() if wait else cp.start()
            return c
        lax.fori_loop(plan_ref[2 * nseg], ntiles, body, 0)

    pads(False)
    tail(False)
    pads(True)
    tail(True)


def _dispatch_kernel(ntiles, plan_ref, pos_ref, hn_ref, xs_ref, sem, zbuf, zsem):
    rows = hn_ref.shape[0] // TOKEN_ROWS

    @pl.when(pl.program_id(0) == 0)
    def _():
        _zero_fill(plan_ref, xs_ref, zbuf, zsem, ntiles)

    def copy(r):
        return pltpu.make_async_copy(_token_tile(hn_ref, r), _token_tile(xs_ref, pos_ref[0, 0, r]), sem)

    def issue(r, c):
        copy(2 * r).start(priority=0)
        copy(2 * r + 1).start(priority=1)
        return c

    lax.fori_loop(0, rows // 2, issue, 0, unroll=4)
    pltpu.make_async_copy(hn_ref, xs_ref.at[pl.ds(0, rows * TOKEN_ROWS), :], sem).wait()


def _pos_blocks(pos, tm):
    return pos.reshape(pos.shape[0] // tm, 1, tm)


def _dispatch(plan, pos, hn_tt, t, tm=1024):
    ntiles = _moe_tiles(t)
    nrows = ntiles * MOE_TILE * TOKEN_ROWS
    return pl.pallas_call(
        functools.partial(_dispatch_kernel, ntiles),
        grid_spec=pltpu.PrefetchScalarGridSpec(
            num_scalar_prefetch=1,
            grid=(t // tm,),
            in_specs=[pl.BlockSpec((1, 1, tm), lambda i, plan: (i, 0, 0), memory_space=pltpu.SMEM),
                      pl.BlockSpec((tm * TOKEN_ROWS, LANES), lambda i, plan: (i, 0))],
            out_specs=pl.BlockSpec(memory_space=pl.ANY),
            scratch_shapes=[pltpu.SemaphoreType.DMA(()),
                            pltpu.VMEM((MOE_TILE // 2 * TOKEN_ROWS, LANES), F32),
                            pltpu.SemaphoreType.DMA(())]),
        out_shape=jax.ShapeDtypeStruct((nrows, LANES), F32),
        compiler_params=_params("arbitrary"),
        name="moe_dispatch",
    )(plan, _pos_blocks(pos, tm), hn_tt)


def _expert_kernel(tg_ref, na_ref, xs_ref, wr_ref, br_ref, wg_ref, wu_ref, wd_ref, ys_ref, wgu_s, wd_s):
    i = pl.program_id(0)
    g = tg_ref[i]
    new_group = (i == 0) | (tg_ref[jnp.maximum(i - 1, 0)] != g)
    width = EXPERTS_PER_GROUP * D_EXPERT

    @pl.when(i < na_ref[0])
    def _():
        @pl.when(new_group)
        def _():
            for e in range(EXPERTS_PER_GROUP):
                cols = slice(e * D_EXPERT, (e + 1) * D_EXPERT)
                wgu_s[:, cols] = wg_ref[0, 0, e].astype(BF16)
                wgu_s[:, width + e * D_EXPERT:width + (e + 1) * D_EXPERT] = wu_ref[0, 0, e].astype(BF16)
                wd_s[cols, :] = wd_ref[0, 0, e].astype(BF16)

        x = _from_token_tiles(xs_ref, MOE_TILE)
        i1, w1, i2, w2 = _route_experts(_router_logits(x, wr_ref, br_ref), g)
        au = _dot(x.astype(BF16), wgu_s[...])
        first = ROUTER_EXPERT_LANE0 + g * EXPERTS_PER_GROUP
        hid = []
        for e in range(EXPERTS_PER_GROUP):
            a = au[:, e * D_EXPERT:(e + 1) * D_EXPERT]
            u = au[:, width + e * D_EXPERT:width + (e + 1) * D_EXPERT]
            gate = jnp.where(i1 == first + e, w1, 0.0) + jnp.where(i2 == first + e, w2, 0.0)
            hid.append(((a * _sigmoid(a)) * u * gate).astype(BF16))
        _to_token_tiles(ys_ref, _dot(jnp.concatenate(hid, axis=1), wd_s[...]))


def _experts(layer, tile_group, n_active, xs, wr, br, w_gate, w_up, w_down):
    ntiles = tile_group.shape[0]
    last = lambda i, na: jnp.minimum(i, na[0] - 1)
    tile = pl.BlockSpec((MOE_TILE * TOKEN_ROWS, LANES), lambda i, tg, na: (last(i, na), 0))
    grouped = lambda w: w.reshape((w.shape[0], N_EXPERT_GROUPS, EXPERTS_PER_GROUP) + w.shape[2:])
    wspec = lambda w: pl.BlockSpec((1, 1, EXPERTS_PER_GROUP) + w.shape[2:],
                                   lambda i, tg, na: (layer, tg[last(i, na)], 0, 0, 0),
                                   pipeline_mode=pl.Buffered(1))
    width = EXPERTS_PER_GROUP * D_EXPERT
    return pl.pallas_call(
        _expert_kernel,
        grid_spec=pltpu.PrefetchScalarGridSpec(
            num_scalar_prefetch=2,
            grid=(ntiles,),
            in_specs=[tile, pl.BlockSpec(wr.shape, lambda i, tg, na: (0, 0)),
                      pl.BlockSpec(br.shape, lambda i, tg, na: (0, 0)),
                      wspec(w_gate), wspec(w_up), wspec(w_down)],
            out_specs=tile,
            scratch_shapes=[pltpu.VMEM((D_MODEL, 2 * width), BF16), pltpu.VMEM((width, D_MODEL), BF16)]),
        out_shape=jax.ShapeDtypeStruct(xs.shape, F32),
        input_output_aliases={2: 0},
        compiler_params=_params("arbitrary"),
        name="moe_experts",
    )(tile_group, n_active, xs, wr, br, grouped(w_gate), grouped(w_up), grouped(w_down))


def _combine_kernel(pos_ref, h_ref, gn_ref, ys_ref, hout_ref, hnorm_ref, buf, sem):
    rows = h_ref.shape[0]

    def copy(r):
        return pltpu.make_async_copy(_token_tile(ys_ref, pos_ref[0, 0, r]), _token_tile(buf, r), sem)

    def issue(r, c):
        copy(2 * r).start(priority=0)
        copy(2 * r + 1).start(priority=1)
        return c

    lax.fori_loop(0, rows // 2, issue, 0, unroll=4)
    pltpu.make_async_copy(ys_ref.at[pl.ds(0, rows * TOKEN_ROWS), :], buf, sem).wait()
    h = h_ref[...] + _from_token_tiles(buf, rows)
    hout_ref[...] = h
    hnorm_ref[...] = _rms(h, gn_ref[...])


def _combine(pos, h, gnext, ys, tm=512):
    t = h.shape[0]
    row = lambda i: (i, 0)
    return pl.pallas_call(
        _combine_kernel,
        grid=(t // tm,),
        in_specs=[pl.BlockSpec((1, 1, tm), lambda i: (i, 0, 0), memory_space=pltpu.SMEM),
                  pl.BlockSpec((tm, D_MODEL), row),
                  pl.BlockSpec((1, D_MODEL), lambda i: (0, 0)),
                  pl.BlockSpec(memory_space=pl.ANY)],
        out_specs=[pl.BlockSpec((tm, D_MODEL), row), pl.BlockSpec((tm, D_MODEL), row)],
        out_shape=[jax.ShapeDtypeStruct((t, D_MODEL), F32), jax.ShapeDtypeStruct((t, D_MODEL), F32)],
        scratch_shapes=[pltpu.VMEM((tm * TOKEN_ROWS, LANES), F32), pltpu.SemaphoreType.DMA(())],
        compiler_params=_params("arbitrary"),
        name="moe_combine",
    )(_pos_blocks(pos, tm), h, gnext, ys)


def _moe(layer, h, hn_tt, route, counts, wr, br, w_gate, w_up, w_down, gnext):
    t = h.shape[0]
    pos, tile_group, n_active, plan = _dispatch_plan(route, counts, t)
    xs = _dispatch(plan, pos, hn_tt, t)
    ys = _experts(layer, tile_group, n_active, xs, wr, br, w_gate, w_up, w_down)
    return _combine(pos, h, gnext, ys)


def _dot3(a, b):
    a_hi = a.astype(BF16)
    a_lo = (a - a_hi.astype(F32)).astype(BF16)
    b_hi = b.astype(BF16)
    b_lo = (b - b_hi.astype(F32)).astype(BF16)
    return _dot(a_hi, b_hi) + (_dot(a_hi, b_lo) + _dot(a_lo, b_hi))


def _ssm_powers(lam_re, lam_im, log_step, lags):
    lam_re = jnp.minimum(lam_re, np.float32(-1e-4))
    step = jnp.exp(log_step)
    dec = jnp.exp(lags * (lam_re * step))
    ang = lags * (lam_im * step)
    return dec * jnp.cos(ang), dec * jnp.sin(ang)


def _ssm_operators(lamr_ref, lamc_ref, bre_ref, bim_ref, cre_ref, cim_ref, r_s, wd_s, vd_s):
    L, ns = SSM_CHUNK, SSM_BLOCK_STATE
    lam = lamr_ref[0]
    lre, lim, lst = lam[0:1], lam[1:2], lam[2:3]
    lag_rev = (L - 1 - lax.broadcasted_iota(jnp.int32, (L, 1), 0)).astype(F32)
    pr_re, pr_im = _ssm_powers(lre, lim, lst, lag_rev)
    ab_re, ab_im = _ssm_powers(lre, lim, lst, np.float32(1.0))
    lam_re = jnp.minimum(lre, np.float32(-1e-4))
    denom = lam_re * lam_re + lim * lim
    num_re = ab_re - 1.0
    f_re = (num_re * lam_re + ab_im * lim) / denom
    f_im = (ab_im * lam_re - num_re * lim) / denom
    bre, bim = bre_ref[0], bim_ref[0]
    bb_re = f_re * bre - f_im * bim
    bb_im = f_re * bim + f_im * bre
    cre, cim = cre_ref[0], cim_ref[0]
    for s in range(L):
        a_re, a_im = pr_re[s:s + 1], pr_im[s:s + 1]
        w_re = a_re * bb_re - a_im * bb_im
        w_im = a_re * bb_im + a_im * bb_re
        rows = slice(s * LANES, (s + 1) * LANES)
        wd_s[rows, 0:ns] = w_re.astype(BF16)
        wd_s[rows, ns:2 * ns] = w_im.astype(BF16)
        r_s[rows, :] = (_dot3(w_re, cre) - _dot3(w_im, cim)).astype(BF16)
    r_s[L * LANES:, :] = jnp.zeros((LANES, LANES), BF16)
    lamc = lamc_ref[0]
    lag1 = (1 + lax.broadcasted_iota(jnp.int32, (1, L), 1)).astype(F32)
    p1_re, p1_im = _ssm_powers(lamc[:, 0:1], lamc[:, 1:2], lamc[:, 2:3], lag1)
    for t in range(L):
        a_re, a_im = p1_re[:, t:t + 1], p1_im[:, t:t + 1]
        cols = slice(t * LANES, (t + 1) * LANES)
        vd_s[0:ns, cols] = (cre * a_re - cim * a_im).astype(BF16)
        vd_s[ns:2 * ns, cols] = (-(cre * a_im + cim * a_re)).astype(BF16)
    return _ssm_powers(lre, lim, lst, np.float32(L))


def _ssm_kernel(nch, x_ref, lamr_ref, lamc_ref, bre_ref, bim_ref, cre_ref, cim_ref, d_ref, o_ref,
                r_s, wd_s, vd_s, xcat, st, xprev):
    nb = x_ref.shape[0]
    ns = SSM_BLOCK_STATE
    ar, ai = _ssm_operators(lamr_ref, lamc_ref, bre_ref, bim_ref, cre_ref, cim_ref, r_s, wd_s, vd_s)
    for s in range(SSM_CHUNK):
        for b in range(nb):
            xs = x_ref[b, pl.ds(s, nch, stride=SSM_CHUNK), :]
            xcat[b * nch:(b + 1) * nch, s * LANES:(s + 1) * LANES] = xs.astype(BF16)
    st[...] = _dot(xcat[...], wd_s[...])

    def step(k, carry):
        out = []
        for b in range(nb):
            xr, xi = carry[b]
            row = b * nch + k
            sr = st[pl.ds(row, 1), 0:ns]
            si = st[pl.ds(row, 1), ns:2 * ns]
            st[pl.ds(row, 1), 0:ns] = xr
            st[pl.ds(row, 1), ns:2 * ns] = xi
            out.append((ar * xr - ai * xi + sr, ar * xi + ai * xr + si))
        return tuple(out)

    zero = jnp.zeros((1, ns), F32)
    lax.fori_loop(0, nch, step, tuple((zero, zero) for _ in range(nb)))
    xprev[...] = st[...].astype(BF16)
    dskip = d_ref[...]
    L = SSM_CHUNK
    for t in range(0, L, 2):
        taps = jnp.concatenate([r_s[(L - 1 - t) * LANES:, :], r_s[(L - 2 - t) * LANES:L * LANES, :]], axis=1)
        y2 = (_dot(xcat[:, :(t + 2) * LANES], taps)
              + _dot(xprev[...], vd_s[:, t * LANES:(t + 2) * LANES]))
        for k in range(2):
            y = y2[:, k * LANES:(k + 1) * LANES]
            for b in range(nb):
                u = x_ref[b, pl.ds(t + k, nch, stride=L), :]
                o_ref[b, pl.ds(t + k, nch, stride=L), :] = _gelu_tanh(y[b * nch:(b + 1) * nch, :] + dskip * u)


def _ssm(x3, params, dskip):
    nb, seq, _ = x3.shape
    nch = seq // SSM_CHUNK
    rows = nb * nch
    ns = SSM_BLOCK_STATE
    blk = pl.BlockSpec((nb, seq, LANES), lambda g: (0, 0, g))
    per = lambda shape: pl.BlockSpec((1,) + shape, lambda g: (g, 0, 0))
    return pl.pallas_call(
        functools.partial(_ssm_kernel, nch),
        grid=(SSM_BLOCKS,),
        in_specs=[blk, per((3, ns)), per((ns, 3)), per((LANES, ns)), per((LANES, ns)),
                  per((ns, LANES)), per((ns, LANES)), pl.BlockSpec((1, LANES), lambda g: (0, g))],
        out_specs=blk,
        out_shape=jax.ShapeDtypeStruct(x3.shape, F32),
        scratch_shapes=[pltpu.VMEM(((SSM_CHUNK + 1) * LANES, LANES), BF16),
                        pltpu.VMEM((SSM_CHUNK * LANES, 2 * ns), BF16),
                        pltpu.VMEM((2 * ns, SSM_CHUNK * LANES), BF16),
                        pltpu.VMEM((rows, SSM_CHUNK * LANES), BF16),
                        pltpu.VMEM((rows, 2 * ns), F32),
                        pltpu.VMEM((rows, 2 * ns), BF16)],
        compiler_params=_params("parallel"),
        name="ssm",
    )(x3, *params, dskip)


def _ssm_params(lambda_re, lambda_im, log_step, b_re, b_im, c_re, c_im):
    nb, ng, ns = SSM_BLOCKS, SSM_LANE_GROUPS, SSM_BLOCK_STATE
    lam = jnp.stack([lambda_re.astype(F32), lambda_im.astype(F32),
                     jnp.broadcast_to(log_step.astype(F32)[:, None], lambda_re.shape)])
    lam_rows = jnp.transpose(lam.reshape(3, nb, ns), (1, 0, 2))
    lam_cols = jnp.transpose(lam_rows, (0, 2, 1))
    eye = jnp.eye(ng, dtype=F32)

    def blockdiag(m):
        a, b = m.shape[-2:]
        out = jnp.einsum('ngab,gh->ngahb', m.astype(F32).reshape(nb, ng, a, b), eye)
        return out.reshape(nb, ng * a, ng * b)

    bt = lambda m: blockdiag(jnp.swapaxes(m, -1, -2))
    return lam_rows, lam_cols, bt(b_re), bt(b_im), bt(c_re), bt(c_im)


def _router_tables(w_group, b_group, w_expert, b_expert):
    w = jnp.concatenate([w_group.astype(F32),
                         jnp.transpose(w_expert.astype(F32), (1, 0, 2)).reshape(D_MODEL, N_EXPERTS)], axis=1)
    b = jnp.concatenate([b_group.astype(F32), b_expert.astype(F32).reshape(N_EXPERTS)])
    pad = ROUTER_LANES - w.shape[1]
    return jnp.pad(w, ((0, 0), (0, pad))), jnp.pad(b, (0, pad)).reshape(1, ROUTER_LANES)


def kernel(x, norm_mix, norm_ffn, norm_final, attn_w_qkv, attn_w_o, attn_rel_bias, ssm_lambda_re, ssm_lambda_im, ssm_log_step, ssm_b_re, ssm_b_im, ssm_c_re, ssm_c_im, ssm_d, ssm_w_out, moe_w_group_router, moe_b_group_router, moe_w_expert_router, moe_b_expert_router, moe_w_gate, moe_w_up, moe_w_down):
    batch, seq, d = x.shape
    t = batch * seq
    row = lambda v: v.astype(F32).reshape(1, d)
    h = x.astype(F32).reshape(t, d)

    qkv = _qkv(h, row(norm_mix[0]), attn_w_qkv[0].astype(F32))
    o = _attention(qkv, _rel_bias_rows(attn_rel_bias[0]), batch, seq)
    wr, br = _router_tables(moe_w_group_router[0], moe_b_group_router[0],
                            moe_w_expert_router[0], moe_b_expert_router[0])
    h, hn_tt, route, counts = _post(o, h, attn_w_o[0].astype(F32), row(norm_ffn[0]), wr, br, glu=False)
    h, hn32 = _moe(0, h, hn_tt, route, counts, wr, br, moe_w_gate, moe_w_up, moe_w_down, row(norm_mix[1]))

    ssm_params = _ssm_params(ssm_lambda_re[0], ssm_lambda_im[0], ssm_log_step[0],
                             ssm_b_re[0], ssm_b_im[0], ssm_c_re[0], ssm_c_im[0])
    y = _ssm(hn32.reshape(batch, seq, d), ssm_params, row(ssm_d[0]))
    wr, br = _router_tables(moe_w_group_router[1], moe_b_group_router[1],
                            moe_w_expert_router[1], moe_b_expert_router[1])
    h, hn_tt, route, counts = _post(y.reshape(t, d), h, ssm_w_out[0].astype(F32), row(norm_ffn[1]), wr, br, glu=True)
    _, out = _moe(1, h, hn_tt, route, counts, wr, br, moe_w_gate, moe_w_up, moe_w_down, row(norm_final))
    return out.reshape(batch, seq, d).astype(x.dtype)
```

```python
import functools

import jax
import jax.numpy as jnp
import numpy as np
from jax import lax
from jax.experimental import pallas as pl
from jax.experimental.pallas import tpu as pltpu

F32 = jnp.float32
BF16 = jnp.bfloat16

D_MODEL = 1024
CHUNK = 64
LOOKBACK_CHUNKS = 8
BAND = (LOOKBACK_CHUNKS + 1) * CHUNK
N_HEADS = 16
HEAD_DIM = D_MODEL // N_HEADS
MAX_REL = 128
SSM_GROUP = 16
SSM_GROUPS = D_MODEL // SSM_GROUP
SSM_STATE = 64
N_EXPERT_GROUPS = 4
EXPERTS_PER_GROUP = 8
N_EXPERTS = N_EXPERT_GROUPS * EXPERTS_PER_GROUP
D_EXPERT = D_MODEL // 4
RMS_EPS = 1e-6
NEG_BIG = -1e30

LANES = 128
SUBLANES = 8
MXU_DIM = 256
VMEM_LIMIT = 56 * 1024 * 1024

HEADS_PER_GROUP = MXU_DIM // HEAD_DIM
N_HEAD_GROUPS = N_HEADS // HEADS_PER_GROUP
ATTN_QBLOCK = LOOKBACK_CHUNKS * CHUNK
REL_PAD = -(-(2 * MAX_REL + 1) // LANES) * LANES
BIAS_WIDTH = -(-(BAND + CHUNK - 1) // LANES) * LANES
ROUTER_LANES = LANES
ROUTER_EXPERT_LANE0 = N_EXPERT_GROUPS
SSM_CHUNK = 16
SSM_LANE_GROUPS = LANES // SSM_GROUP
SSM_BLOCKS = D_MODEL // LANES
SSM_BLOCK_STATE = SSM_LANE_GROUPS * SSM_STATE
TOKEN_ROWS = D_MODEL // LANES
MOE_TILE = 256
R_GROUP, R_RANK = range(2)


def _dot(a, b):
    return jnp.dot(a, b, preferred_element_type=F32)


def _rms(x, g):
    return x * lax.rsqrt(jnp.mean(x * x, axis=-1, keepdims=True) + RMS_EPS) * g


def _sigmoid(x):
    return 1.0 / (1.0 + jnp.exp(-x))


def _gelu_tanh(x):
    c = np.float32(np.sqrt(2.0 / np.pi))
    return 0.5 * x * (1.0 + jnp.tanh(c * (x + np.float32(0.044715) * (x * x * x))))


def _params(*sem):
    return pltpu.CompilerParams(dimension_semantics=sem, vmem_limit_bytes=VMEM_LIMIT)


def _resident(shape):
    return pl.BlockSpec(shape, lambda *_: (0,) * len(shape), pipeline_mode=pl.Buffered(1))


def _cast_once(w_ref, wb):
    @pl.when(pl.program_id(0) == 0)
    def _():
        wb[...] = w_ref[...].astype(BF16)


def _qkv_kernel(x_ref, g_ref, w_ref, o_ref, wb):
    _cast_once(w_ref, wb)
    xn = _rms(x_ref[...], g_ref[...]).astype(BF16)
    for c in range(3):
        acc = _dot(xn, wb[:, c * D_MODEL:(c + 1) * D_MODEL])
        if c == 0:
            acc = acc * np.float32(HEAD_DIM ** -0.5)
        o_ref[:, c * D_MODEL:(c + 1) * D_MODEL] = acc.astype(BF16)


def _qkv(x, g, w, tm=512):
    t = x.shape[0]
    return pl.pallas_call(
        _qkv_kernel,
        grid=(t // tm,),
        in_specs=[pl.BlockSpec((tm, D_MODEL), lambda i: (i, 0)),
                  pl.BlockSpec((1, D_MODEL), lambda i: (0, 0)),
                  _resident(w.shape)],
        out_specs=pl.BlockSpec((tm, 3 * D_MODEL), lambda i: (i, 0)),
        out_shape=jax.ShapeDtypeStruct((t, 3 * D_MODEL), BF16),
        scratch_shapes=[pltpu.VMEM(w.shape, BF16)],
        compiler_params=_params("arbitrary"),
        name="qkv",
    )(x, g, w)


def _expand_rel_bias(rel_ref, bias):
    rel = rel_ref[0]
    r1 = rel.astype(BF16)
    r2 = (rel - r1.astype(F32)).astype(BF16)
    r3 = (rel - r1.astype(F32) - r2.astype(F32)).astype(BF16)
    j = lax.broadcasted_iota(jnp.int32, (REL_PAD, BIAS_WIDTH), 1)
    src = jnp.clip(BAND - 1 - j, -MAX_REL, MAX_REL) + MAX_REL
    sel = jnp.where(lax.broadcasted_iota(jnp.int32, (REL_PAD, BIAS_WIDTH), 0) == src, 1.0, 0.0).astype(BF16)
    u = (_dot(r1, sel) + _dot(r2, sel)) + _dot(r3, sel)
    for h in range(HEADS_PER_GROUP):
        rows = jnp.broadcast_to(u[h:h + 1, :], (CHUNK, BIAS_WIDTH))
        rows = pltpu.roll(rows, BIAS_WIDTH - (CHUNK - 1), 1, stride=1, stride_axis=0)
        bias[h * CHUNK:(h + 1) * CHUNK, :] = rows[:, :BAND]


def _attn_kernel(q_ref, kp_ref, kc_ref, vp_ref, vc_ref, rel_ref, o_ref, kk, vv, bias):
    qb = pl.program_id(2)

    @pl.when(qb == 0)
    def _():
        _expand_rel_bias(rel_ref, bias)

    kk[0:ATTN_QBLOCK, :] = kp_ref[...]
    kk[ATTN_QBLOCK:, :] = kc_ref[...]
    vv[0:ATTN_QBLOCK, :] = vp_ref[...]
    vv[ATTN_QBLOCK:, :] = vc_ref[...]
    lane_head = lax.broadcasted_iota(jnp.int32, (CHUNK, MXU_DIM), 1) // HEAD_DIM
    col = lax.broadcasted_iota(jnp.int32, (1, BAND), 1)

    def chunks(first_block):
        for j in range(ATTN_QBLOCK // CHUNK):
            qj = q_ref[j * CHUNK:(j + 1) * CHUNK, :]
            lhs = jnp.concatenate(
                [jnp.where(lane_head == h, qj, jnp.zeros_like(qj)) for h in range(HEADS_PER_GROUP)], axis=0)
            kwin = kk[j * CHUNK:j * CHUNK + BAND, :]
            vwin = vv[j * CHUNK:j * CHUNK + BAND, :]
            s = lax.dot_general(lhs, kwin, (((1,), (1,)), ((), ())), preferred_element_type=F32) + bias[...]
            if first_block:
                s = jnp.where(col + j * CHUNK >= ATTN_QBLOCK, s, NEG_BIG)
            m = jnp.max(s, axis=-1, keepdims=True)
            p = jnp.exp(s - m)
            l = jnp.sum(p, axis=-1, keepdims=True)
            o_all = _dot(p.astype(BF16), vwin) * (1.0 / l)
            o = jnp.zeros((CHUNK, MXU_DIM), F32)
            for h in range(HEADS_PER_GROUP):
                o = o + jnp.where(lane_head == h, o_all[h * CHUNK:(h + 1) * CHUNK, :], 0.0)
            o_ref[j * CHUNK:(j + 1) * CHUNK, :] = o.astype(BF16)

    @pl.when(qb == 0)
    def _():
        chunks(True)

    @pl.when(qb > 0)
    def _():
        chunks(False)


def _attention(qkv, rel, batch, seq):
    nqb = seq // ATTN_QBLOCK
    kcol = D_MODEL // MXU_DIM
    blk = (ATTN_QBLOCK, MXU_DIM)
    cur = lambda off: (lambda b, g, i: (b * nqb + i, off + g))
    prev = lambda off: (lambda b, g, i: (b * nqb + jnp.maximum(i - 1, 0), off + g))
    return pl.pallas_call(
        _attn_kernel,
        grid=(batch, N_HEAD_GROUPS, nqb),
        in_specs=[pl.BlockSpec(blk, cur(0)),
                  pl.BlockSpec(blk, prev(kcol)), pl.BlockSpec(blk, cur(kcol)),
                  pl.BlockSpec(blk, prev(2 * kcol)), pl.BlockSpec(blk, cur(2 * kcol)),
                  pl.BlockSpec((1, SUBLANES, REL_PAD), lambda b, g, i: (g, 0, 0))],
        out_specs=pl.BlockSpec(blk, cur(0)),
        out_shape=jax.ShapeDtypeStruct((batch * seq, D_MODEL), BF16),
        scratch_shapes=[pltpu.VMEM((2 * ATTN_QBLOCK, MXU_DIM), BF16),
                        pltpu.VMEM((2 * ATTN_QBLOCK, MXU_DIM), BF16),
                        pltpu.VMEM((HEADS_PER_GROUP * CHUNK, BAND), F32)],
        compiler_params=_params("parallel", "parallel", "arbitrary"),
        name="attn",
    )(qkv, qkv, qkv, qkv, qkv, rel)


def _rel_bias_rows(rel_bias):
    rel = rel_bias.astype(F32).reshape(N_HEAD_GROUPS, HEADS_PER_GROUP, 2 * MAX_REL + 1)
    return jnp.pad(rel, ((0, 0), (0, SUBLANES - HEADS_PER_GROUP), (0, REL_PAD - (2 * MAX_REL + 1))))


def _router_logits(hn, wr_ref, br_ref):
    a_hi = hn.astype(BF16)
    a_lo = (hn - a_hi.astype(F32)).astype(BF16)
    w = wr_ref[...]
    w_hi = w.astype(BF16)
    w_lo = (w - w_hi.astype(F32)).astype(BF16)
    return _dot(a_hi, w_hi) + (_dot(a_hi, w_lo) + _dot(a_lo, w_hi)) + br_ref[...]


def _route_group(lg, base):
    rows = lg.shape[0]
    lane = lax.broadcasted_iota(jnp.int32, lg.shape, 1)
    gl = jnp.where(lane < N_EXPERT_GROUPS, lg, np.float32(-np.inf))
    gmax = jnp.max(gl, axis=-1, keepdims=True)
    gidx = jnp.min(jnp.where(gl == gmax, lane, np.int32(ROUTER_LANES)), axis=-1, keepdims=True)
    og = lane == gidx
    tri = (lax.broadcasted_iota(jnp.int32, (rows, rows), 0)
           > lax.broadcasted_iota(jnp.int32, (rows, rows), 1))
    tri = jnp.where(tri, 1.0, 0.0).astype(BF16)
    ogf = jnp.where(og, 1.0, 0.0)
    rank = jnp.sum(jnp.where(og, base + _dot(tri, ogf.astype(BF16)), 0.0), axis=-1, keepdims=True)
    rec = jnp.where(lane == R_GROUP, gidx.astype(F32), jnp.where(lane == R_RANK, rank, 0.0))
    return rec, base + jnp.sum(ogf, axis=0, keepdims=True)


def _route_experts(lg, g):
    lane = lax.broadcasted_iota(jnp.int32, lg.shape, 1)
    ninf = np.float32(-np.inf)
    big = np.int32(ROUTER_LANES)
    gmask = lane < N_EXPERT_GROUPS
    gmax = jnp.max(jnp.where(gmask, lg, ninf), axis=-1, keepdims=True)
    ge = jnp.where(gmask, jnp.exp(lg - gmax), 0.0)
    gprob = jnp.sum(jnp.where(lane == g, ge, 0.0), axis=-1, keepdims=True) / jnp.sum(ge, axis=-1, keepdims=True)
    lo = ROUTER_EXPERT_LANE0 + EXPERTS_PER_GROUP * g
    el = jnp.where((lane >= lo) & (lane < lo + EXPERTS_PER_GROUP), lg, ninf)
    l1 = jnp.max(el, axis=-1, keepdims=True)
    i1 = jnp.min(jnp.where(el == l1, lane, big), axis=-1, keepdims=True)
    el2 = jnp.where(lane == i1, ninf, el)
    l2 = jnp.max(el2, axis=-1, keepdims=True)
    i2 = jnp.min(jnp.where(el2 == l2, lane, big), axis=-1, keepdims=True)
    t = jnp.exp(l2 - l1)
    w1 = gprob / (1.0 + t)
    return i1, w1, i2, w1 * t


def _to_token_tiles(ref, x):
    rows = x.shape[0]
    for j in range(TOKEN_ROWS):
        ref[pl.ds(j, rows, stride=TOKEN_ROWS), :] = x[:, j * LANES:(j + 1) * LANES]


def _from_token_tiles(ref, rows):
    return jnp.concatenate([ref[pl.ds(j, rows, stride=TOKEN_ROWS), :] for j in range(TOKEN_ROWS)], axis=1)


def _post_kernel(glu, y_ref, h_ref, w_ref, g_ref, wr_ref, br_ref, hout_ref, hntt_ref, route_ref, cnt_ref, wb):
    @pl.when(pl.program_id(0) == 0)
    def _():
        cnt_ref[...] = jnp.zeros_like(cnt_ref)

    _cast_once(w_ref, wb)
    y = y_ref[...].astype(BF16)
    if glu:
        mix = _dot(y, wb[:, :D_MODEL]) * _sigmoid(_dot(y, wb[:, D_MODEL:]))
    else:
        mix = _dot(y, wb[...])
    h = h_ref[...] + mix
    hout_ref[...] = h
    hn = _rms(h, g_ref[...])
    _to_token_tiles(hntt_ref, hn)
    route_ref[...], cnt_ref[...] = _route_group(_router_logits(hn, wr_ref, br_ref), cnt_ref[...])


def _post(y, h, w, g, wr, br, glu, tm=512):
    t = h.shape[0]
    row = lambda i: (i, 0)
    fixed = lambda i: (0, 0)
    return pl.pallas_call(
        functools.partial(_post_kernel, glu),
        grid=(t // tm,),
        in_specs=[pl.BlockSpec((tm, D_MODEL), row), pl.BlockSpec((tm, D_MODEL), row),
                  _resident(w.shape), pl.BlockSpec((1, D_MODEL), fixed),
                  pl.BlockSpec((D_MODEL, ROUTER_LANES), fixed), pl.BlockSpec((1, ROUTER_LANES), fixed)],
        out_specs=[pl.BlockSpec((tm, D_MODEL), row), pl.BlockSpec((tm * TOKEN_ROWS, LANES), row),
                   pl.BlockSpec((tm, ROUTER_LANES), row), pl.BlockSpec((1, ROUTER_LANES), fixed)],
        out_shape=[jax.ShapeDtypeStruct((t, D_MODEL), F32), jax.ShapeDtypeStruct((t * TOKEN_ROWS, LANES), F32),
                   jax.ShapeDtypeStruct((t, ROUTER_LANES), F32), jax.ShapeDtypeStruct((1, ROUTER_LANES), F32)],
        scratch_shapes=[pltpu.VMEM(w.shape, BF16)],
        compiler_params=_params("arbitrary"),
        name="post_glu" if glu else "post_attn",
    )(y, h, w, g, wr, br)


def _moe_tiles(t):
    return t // MOE_TILE + N_EXPERT_GROUPS


def _dispatch_plan(route, counts, t):
    g = route[:, R_GROUP].astype(jnp.int32)
    rank = route[:, R_RANK].astype(jnp.int32)
    cnt = counts[0, :N_EXPERT_GROUPS].astype(jnp.int32)
    padded = ((cnt + (MOE_TILE - 1)) // MOE_TILE) * MOE_TILE
    ends = jnp.cumsum(padded)
    off = ends - padded
    onehot = g[:, None] == jnp.arange(N_EXPERT_GROUPS, dtype=jnp.int32)
    pos = jnp.sum(jnp.where(onehot, off, 0), axis=-1) + rank
    ntiles = _moe_tiles(t)
    tile_start = jnp.arange(ntiles, dtype=jnp.int32) * MOE_TILE
    tile_group = jnp.minimum(jnp.sum((tile_start[:, None] >= ends[None, :]).astype(jnp.int32), axis=1),
                             N_EXPERT_GROUPS - 1)
    n_active = (ends[-1] // MOE_TILE).reshape(1)
    plan = jnp.concatenate([off + cnt, padded - cnt, n_active])
    return pos, tile_group, n_active, plan


def _token_tile(ref, i):
    return ref.at[pl.ds(pl.multiple_of(i * TOKEN_ROWS, TOKEN_ROWS), TOKEN_ROWS), :]


def _zero_fill(plan_ref, xs_ref, zbuf, zsem, ntiles):
    nseg = N_EXPERT_GROUPS
    zbuf[...] = jnp.zeros_like(zbuf)
    half = MOE_TILE // 2
    sizes = [1 << b for b in reversed(range(half.bit_length()))]

    def chunk(first, size):
        return pltpu.make_async_copy(zbuf.at[pl.ds(0, size * TOKEN_ROWS), :],
                                     xs_ref.at[pl.ds(pl.multiple_of(first * TOKEN_ROWS, TOKEN_ROWS),
                                                     size * TOKEN_ROWS), :], zsem)

    def pads(wait):
        def body(e, c):
            first = plan_ref[e]
            n = plan_ref[nseg + e]
            for size in sizes:
                hit = (n & size) != 0

                @pl.when(hit)
                def _():
                    cp = chunk(first, size)
                    cp.wait() if wait else cp.start()

                first = first + jnp.where(hit, size, 0)
            return c
        lax.fori_loop(0, nseg, body, 0)

    def tail(wait):
        def body(i, c):
            for k in range(2):
                cp = chunk(i * MOE_TILE + k * half, half)
                cp.wait() if wait else cp.start()
            return c
        lax.fori_loop(plan_ref[2 * nseg], ntiles, body, 0)

    pads(False)
    tail(False)
    pads(True)
    tail(True)


def _dispatch_kernel(ntiles, plan_ref, pos_ref, hn_ref, xs_ref, sem, zbuf, zsem):
    rows = hn_ref.shape[0] // TOKEN_ROWS

    @pl.when(pl.program_id(0) == 0)
    def _():
        _zero_fill(plan_ref, xs_ref, zbuf, zsem, ntiles)

    def copy(r):
        return pltpu.make_async_copy(_token_tile(hn_ref, r), _token_tile(xs_ref, pos_ref[0, 0, r]), sem)

    def issue(r, c):
        copy(2 * r).start(priority=0)
        copy(2 * r + 1).start(priority=1)
        return c

    lax.fori_loop(0, rows // 2, issue, 0, unroll=4)
    pltpu.make_async_copy(hn_ref, xs_ref.at[pl.ds(0, rows * TOKEN_ROWS), :], sem).wait()


def _pos_blocks(pos, tm):
    return pos.reshape(pos.shape[0] // tm, 1, tm)


def _dispatch(plan, pos, hn_tt, t, tm=1024):
    ntiles = _moe_tiles(t)
    nrows = ntiles * MOE_TILE * TOKEN_ROWS
    return pl.pallas_call(
        functools.partial(_dispatch_kernel, ntiles),
        grid_spec=pltpu.PrefetchScalarGridSpec(
            num_scalar_prefetch=1,
            grid=(t // tm,),
            in_specs=[pl.BlockSpec((1, 1, tm), lambda i, plan: (i, 0, 0), memory_space=pltpu.SMEM),
                      pl.BlockSpec((tm * TOKEN_ROWS, LANES), lambda i, plan: (i, 0))],
            out_specs=pl.BlockSpec(memory_space=pl.ANY),
            scratch_shapes=[pltpu.SemaphoreType.DMA(()),
                            pltpu.VMEM((MOE_TILE // 2 * TOKEN_ROWS, LANES), F32),
                            pltpu.SemaphoreType.DMA(())]),
        out_shape=jax.ShapeDtypeStruct((nrows, LANES), F32),
        compiler_params=_params("arbitrary"),
        name="moe_dispatch",
    )(plan, _pos_blocks(pos, tm), hn_tt)


def _expert_kernel(tg_ref, na_ref, xs_ref, wr_ref, br_ref, wg_ref, wu_ref, wd_ref, ys_ref, wgu_s, wd_s):
    i = pl.program_id(0)
    g = tg_ref[i]
    new_group = (i == 0) | (tg_ref[jnp.maximum(i - 1, 0)] != g)
    width = EXPERTS_PER_GROUP * D_EXPERT

    @pl.when(i < na_ref[0])
    def _():
        @pl.when(new_group)
        def _():
            for e in range(EXPERTS_PER_GROUP):
                cols = slice(e * D_EXPERT, (e + 1) * D_EXPERT)
                wgu_s[:, cols] = wg_ref[0, 0, e].astype(BF16)
                wgu_s[:, width + e * D_EXPERT:width + (e + 1) * D_EXPERT] = wu_ref[0, 0, e].astype(BF16)
                wd_s[cols, :] = wd_ref[0, 0, e].astype(BF16)

        x = _from_token_tiles(xs_ref, MOE_TILE)
        i1, w1, i2, w2 = _route_experts(_router_logits(x, wr_ref, br_ref), g)
        au = _dot(x.astype(BF16), wgu_s[...])
        first = ROUTER_EXPERT_LANE0 + g * EXPERTS_PER_GROUP
        hid = []
        for e in range(EXPERTS_PER_GROUP):
            a = au[:, e * D_EXPERT:(e + 1) * D_EXPERT]
            u = au[:, width + e * D_EXPERT:width + (e + 1) * D_EXPERT]
            gate = jnp.where(i1 == first + e, w1, 0.0) + jnp.where(i2 == first + e, w2, 0.0)
            hid.append(((a * _sigmoid(a)) * u * gate).astype(BF16))
        _to_token_tiles(ys_ref, _dot(jnp.concatenate(hid, axis=1), wd_s[...]))


def _experts(layer, tile_group, n_active, xs, wr, br, w_gate, w_up, w_down):
    ntiles = tile_group.shape[0]
    last = lambda i, na: jnp.minimum(i, na[0] - 1)
    tile = pl.BlockSpec((MOE_TILE * TOKEN_ROWS, LANES), lambda i, tg, na: (last(i, na), 0))
    grouped = lambda w: w.reshape((w.shape[0], N_EXPERT_GROUPS, EXPERTS_PER_GROUP) + w.shape[2:])
    wspec = lambda w: pl.BlockSpec((1, 1, EXPERTS_PER_GROUP) + w.shape[2:],
                                   lambda i, tg, na: (layer, tg[last(i, na)], 0, 0, 0),
                                   pipeline_mode=pl.Buffered(1))
    width = EXPERTS_PER_GROUP * D_EXPERT
    return pl.pallas_call(
        _expert_kernel,
        grid_spec=pltpu.PrefetchScalarGridSpec(
            num_scalar_prefetch=2,
            grid=(ntiles,),
            in_specs=[tile, pl.BlockSpec(wr.shape, lambda i, tg, na: (0, 0)),
                      pl.BlockSpec(br.shape, lambda i, tg, na: (0, 0)),
                      wspec(w_gate), wspec(w_up), wspec(w_down)],
            out_specs=tile,
            scratch_shapes=[pltpu.VMEM((D_MODEL, 2 * width), BF16), pltpu.VMEM((width, D_MODEL), BF16)]),
        out_shape=jax.ShapeDtypeStruct(xs.shape, F32),
        input_output_aliases={2: 0},
        compiler_params=_params("arbitrary"),
        name="moe_experts",
    )(tile_group, n_active, xs, wr, br, grouped(w_gate), grouped(w_up), grouped(w_down))


def _combine_kernel(emit_h, pos_ref, nxt_ref, h_ref, gn_ref, ys_ref, *refs):
    outs, (buf0, buf1, sem0, sem1) = refs[:-4], refs[-4:]
    bufs, sems = (buf0, buf1), (sem0, sem1)
    rows = h_ref.shape[0]
    i = pl.program_id(0)

    def gather(idx_ref, slot):
        def copy(r):
            return pltpu.make_async_copy(_token_tile(ys_ref, idx_ref[0, 0, r]), _token_tile(bufs[slot], r),
                                         sems[slot])

        def issue(r, c):
            copy(2 * r).start(priority=0)
            copy(2 * r + 1).start(priority=1)
            return c

        lax.fori_loop(0, rows // 2, issue, 0, unroll=4)

    @pl.when(i == 0)
    def _():
        gather(pos_ref, 0)

    for slot in range(2):
        @pl.when(i % 2 == slot)
        def _():
            @pl.when(i + 1 < pl.num_programs(0))
            def _():
                gather(nxt_ref, 1 - slot)

            pltpu.make_async_copy(ys_ref.at[pl.ds(0, rows * TOKEN_ROWS), :], bufs[slot], sems[slot]).wait()
            h = h_ref[...] + _from_token_tiles(bufs[slot], rows)
            if emit_h:
                outs[0][...] = h
            outs[-1][...] = _rms(h, gn_ref[...])


def _combine(pos, h, gnext, ys, emit_h, tm=512):
    t = h.shape[0]
    nblk = t // tm
    row = lambda i: (i, 0)
    blocks = _pos_blocks(pos, tm)
    smem = lambda imap: pl.BlockSpec((1, 1, tm), imap, memory_space=pltpu.SMEM)
    n_out = 2 if emit_h else 1
    out = pl.pallas_call(
        functools.partial(_combine_kernel, emit_h),
        grid=(nblk,),
        in_specs=[smem(lambda i: (i, 0, 0)), smem(lambda i: (jnp.minimum(i + 1, nblk - 1), 0, 0)),
                  pl.BlockSpec((tm, D_MODEL), row),
                  pl.BlockSpec((1, D_MODEL), lambda i: (0, 0)),
                  pl.BlockSpec(memory_space=pl.ANY)],
        out_specs=[pl.BlockSpec((tm, D_MODEL), row)] * n_out,
        out_shape=[jax.ShapeDtypeStruct((t, D_MODEL), F32)] * n_out,
        scratch_shapes=[pltpu.VMEM((tm * TOKEN_ROWS, LANES), F32), pltpu.VMEM((tm * TOKEN_ROWS, LANES), F32),
                        pltpu.SemaphoreType.DMA(()), pltpu.SemaphoreType.DMA(())],
        compiler_params=_params("arbitrary"),
        name="moe_combine",
    )(blocks, blocks, h, gnext, ys)
    return (out[0], out[1]) if emit_h else (None, out[0])


def _moe(layer, h, hn_tt, route, counts, wr, br, w_gate, w_up, w_down, gnext, emit_h):
    t = h.shape[0]
    pos, tile_group, n_active, plan = _dispatch_plan(route, counts, t)
    xs = _dispatch(plan, pos, hn_tt, t)
    ys = _experts(layer, tile_group, n_active, xs, wr, br, w_gate, w_up, w_down)
    return _combine(pos, h, gnext, ys, emit_h)


def _dot3(a, b):
    a_hi = a.astype(BF16)
    a_lo = (a - a_hi.astype(F32)).astype(BF16)
    b_hi = b.astype(BF16)
    b_lo = (b - b_hi.astype(F32)).astype(BF16)
    return _dot(a_hi, b_hi) + (_dot(a_hi, b_lo) + _dot(a_lo, b_hi))


def _ssm_powers(lam_re, lam_im, log_step, lags):
    lam_re = jnp.minimum(lam_re, np.float32(-1e-4))
    step = jnp.exp(log_step)
    dec = jnp.exp(lags * (lam_re * step))
    ang = lags * (lam_im * step)
    return dec * jnp.cos(ang), dec * jnp.sin(ang)


def _ssm_operators(lamr_ref, lamc_ref, bre_ref, bim_ref, cre_ref, cim_ref, r_s, wd_s, vd_s):
    L, ns = SSM_CHUNK, SSM_BLOCK_STATE
    lam = lamr_ref[0]
    lre, lim, lst = lam[0:1], lam[1:2], lam[2:3]
    lag_rev = (L - 1 - lax.broadcasted_iota(jnp.int32, (L, 1), 0)).astype(F32)
    pr_re, pr_im = _ssm_powers(lre, lim, lst, lag_rev)
    ab_re, ab_im = _ssm_powers(lre, lim, lst, np.float32(1.0))
    lam_re = jnp.minimum(lre, np.float32(-1e-4))
    denom = lam_re * lam_re + lim * lim
    num_re = ab_re - 1.0
    f_re = (num_re * lam_re + ab_im * lim) / denom
    f_im = (ab_im * lam_re - num_re * lim) / denom
    bre, bim = bre_ref[0], bim_ref[0]
    bb_re = f_re * bre - f_im * bim
    bb_im = f_re * bim + f_im * bre
    cre, cim = cre_ref[0], cim_ref[0]
    for s in range(L):
        a_re, a_im = pr_re[s:s + 1], pr_im[s:s + 1]
        w_re = a_re * bb_re - a_im * bb_im
        w_im = a_re * bb_im + a_im * bb_re
        rows = slice(s * LANES, (s + 1) * LANES)
        wd_s[rows, 0:ns] = w_re.astype(BF16)
        wd_s[rows, ns:2 * ns] = w_im.astype(BF16)
        r_s[rows, :] = (_dot3(w_re, cre) - _dot3(w_im, cim)).astype(BF16)
    r_s[L * LANES:, :] = jnp.zeros((LANES, LANES), BF16)
    lamc = lamc_ref[0]
    lag1 = (1 + lax.broadcasted_iota(jnp.int32, (1, L), 1)).astype(F32)
    p1_re, p1_im = _ssm_powers(lamc[:, 0:1], lamc[:, 1:2], lamc[:, 2:3], lag1)
    for t in range(L):
        a_re, a_im = p1_re[:, t:t + 1], p1_im[:, t:t + 1]
        cols = slice(t * LANES, (t + 1) * LANES)
        vd_s[0:ns, cols] = (cre * a_re - cim * a_im).astype(BF16)
        vd_s[ns:2 * ns, cols] = (-(cre * a_im + cim * a_re)).astype(BF16)
    return _ssm_powers(lre, lim, lst, np.float32(L))


def _ssm_kernel(nch, x_ref, lamr_ref, lamc_ref, bre_ref, bim_ref, cre_ref, cim_ref, d_ref, o_ref,
                r_s, wd_s, vd_s, xcat, st, xprev):
    nb = x_ref.shape[0]
    ns = SSM_BLOCK_STATE
    ar, ai = _ssm_operators(lamr_ref, lamc_ref, bre_ref, bim_ref, cre_ref, cim_ref, r_s, wd_s, vd_s)
    for s in range(SSM_CHUNK):
        for b in range(nb):
            xs = x_ref[b, pl.ds(s, nch, stride=SSM_CHUNK), :]
            xcat[b * nch:(b + 1) * nch, s * LANES:(s + 1) * LANES] = xs.astype(BF16)
    st[...] = _dot(xcat[...], wd_s[...])

    def step(k, carry):
        out = []
        for b in range(nb):
            xr, xi = carry[b]
            row = b * nch + k
            sr = st[pl.ds(row, 1), 0:ns]
            si = st[pl.ds(row, 1), ns:2 * ns]
            st[pl.ds(row, 1), 0:ns] = xr
            st[pl.ds(row, 1), ns:2 * ns] = xi
            out.append((ar * xr - ai * xi + sr, ar * xi + ai * xr + si))
        return tuple(out)

    zero = jnp.zeros((1, ns), F32)
    lax.fori_loop(0, nch, step, tuple((zero, zero) for _ in range(nb)))
    xprev[...] = st[...].astype(BF16)
    dskip = d_ref[...]
    L = SSM_CHUNK
    for t in range(0, L, 2):
        taps = jnp.concatenate([r_s[(L - 1 - t) * LANES:, :], r_s[(L - 2 - t) * LANES:L * LANES, :]], axis=1)
        y2 = (_dot(xcat[:, :(t + 2) * LANES], taps)
              + _dot(xprev[...], vd_s[:, t * LANES:(t + 2) * LANES]))
        for k in range(2):
            y = y2[:, k * LANES:(k + 1) * LANES]
            for b in range(nb):
                u = x_ref[b, pl.ds(t + k, nch, stride=L), :]
                o_ref[b, pl.ds(t + k, nch, stride=L), :] = _gelu_tanh(y[b * nch:(b + 1) * nch, :] + dskip * u)


def _ssm(x3, params, dskip):
    nb, seq, _ = x3.shape
    nch = seq // SSM_CHUNK
    rows = nb * nch
    ns = SSM_BLOCK_STATE
    blk = pl.BlockSpec((nb, seq, LANES), lambda g: (0, 0, g))
    per = lambda shape: pl.BlockSpec((1,) + shape, lambda g: (g, 0, 0))
    return pl.pallas_call(
        functools.partial(_ssm_kernel, nch),
        grid=(SSM_BLOCKS,),
        in_specs=[blk, per((3, ns)), per((ns, 3)), per((LANES, ns)), per((LANES, ns)),
                  per((ns, LANES)), per((ns, LANES)), pl.BlockSpec((1, LANES), lambda g: (0, g))],
        out_specs=blk,
        out_shape=jax.ShapeDtypeStruct(x3.shape, F32),
        scratch_shapes=[pltpu.VMEM(((SSM_CHUNK + 1) * LANES, LANES), BF16),
                        pltpu.VMEM((SSM_CHUNK * LANES, 2 * ns), BF16),
                        pltpu.VMEM((2 * ns, SSM_CHUNK * LANES), BF16),
                        pltpu.VMEM((rows, SSM_CHUNK * LANES), BF16),
                        pltpu.VMEM((rows, 2 * ns), F32),
                        pltpu.VMEM((rows, 2 * ns), BF16)],
        compiler_params=_params("parallel"),
        name="ssm",
    )(x3, *params, dskip)


def _ssm_params(lambda_re, lambda_im, log_step, b_re, b_im, c_re, c_im):
    nb, ng, ns = SSM_BLOCKS, SSM_LANE_GROUPS, SSM_BLOCK_STATE
    lam = jnp.stack([lambda_re.astype(F32), lambda_im.astype(F32),
                     jnp.broadcast_to(log_step.astype(F32)[:, None], lambda_re.shape)])
    lam_rows = jnp.transpose(lam.reshape(3, nb, ns), (1, 0, 2))
    lam_cols = jnp.transpose(lam_rows, (0, 2, 1))
    eye = jnp.eye(ng, dtype=F32)

    def blockdiag(m):
        a, b = m.shape[-2:]
        out = jnp.einsum('ngab,gh->ngahb', m.astype(F32).reshape(nb, ng, a, b), eye)
        return out.reshape(nb, ng * a, ng * b)

    bt = lambda m: blockdiag(jnp.swapaxes(m, -1, -2))
    return lam_rows, lam_cols, bt(b_re), bt(b_im), bt(c_re), bt(c_im)


def _router_tables(w_group, b_group, w_expert, b_expert):
    w = jnp.concatenate([w_group.astype(F32),
                         jnp.transpose(w_expert.astype(F32), (1, 0, 2)).reshape(D_MODEL, N_EXPERTS)], axis=1)
    b = jnp.concatenate([b_group.astype(F32), b_expert.astype(F32).reshape(N_EXPERTS)])
    pad = ROUTER_LANES - w.shape[1]
    return jnp.pad(w, ((0, 0), (0, pad))), jnp.pad(b, (0, pad)).reshape(1, ROUTER_LANES)


def kernel(x, norm_mix, norm_ffn, norm_final, attn_w_qkv, attn_w_o, attn_rel_bias, ssm_lambda_re, ssm_lambda_im, ssm_log_step, ssm_b_re, ssm_b_im, ssm_c_re, ssm_c_im, ssm_d, ssm_w_out, moe_w_group_router, moe_b_group_router, moe_w_expert_router, moe_b_expert_router, moe_w_gate, moe_w_up, moe_w_down):
    batch, seq, d = x.shape
    t = batch * seq
    row = lambda v: v.astype(F32).reshape(1, d)
    h = x.astype(F32).reshape(t, d)

    qkv = _qkv(h, row(norm_mix[0]), attn_w_qkv[0].astype(F32))
    o = _attention(qkv, _rel_bias_rows(attn_rel_bias[0]), batch, seq)
    wr, br = _router_tables(moe_w_group_router[0], moe_b_group_router[0],
                            moe_w_expert_router[0], moe_b_expert_router[0])
    h, hn_tt, route, counts = _post(o, h, attn_w_o[0].astype(F32), row(norm_ffn[0]), wr, br, glu=False)
    h, hn32 = _moe(0, h, hn_tt, route, counts, wr, br, moe_w_gate, moe_w_up, moe_w_down, row(norm_mix[1]),
                   emit_h=True)

    ssm_params = _ssm_params(ssm_lambda_re[0], ssm_lambda_im[0], ssm_log_step[0],
                             ssm_b_re[0], ssm_b_im[0], ssm_c_re[0], ssm_c_im[0])
    y = _ssm(hn32.reshape(batch, seq, d), ssm_params, row(ssm_d[0]))
    wr, br = _router_tables(moe_w_group_router[1], moe_b_group_router[1],
                            moe_w_expert_router[1], moe_b_expert_router[1])
    h, hn_tt, route, counts = _post(y.reshape(t, d), h, ssm_w_out[0].astype(F32), row(norm_ffn[1]), wr, br, glu=True)
    _, out = _moe(1, h, hn_tt, route, counts, wr, br, moe_w_gate, moe_w_up, moe_w_down, row(norm_final),
                 emit_h=False)
    return out.reshape(batch, seq, d).astype(x.dtype)
```

```python
import functools

import jax
import jax.numpy as jnp
import numpy as np
from jax import lax
from jax.experimental import pallas as pl
from jax.experimental.pallas import tpu as pltpu

F32 = jnp.float32
BF16 = jnp.bfloat16

D_MODEL = 1024
CHUNK = 64
LOOKBACK_CHUNKS = 8
BAND = (LOOKBACK_CHUNKS + 1) * CHUNK
N_HEADS = 16
HEAD_DIM = D_MODEL // N_HEADS
MAX_REL = 128
SSM_GROUP = 16
SSM_GROUPS = D_MODEL // SSM_GROUP
SSM_STATE = 64
N_EXPERT_GROUPS = 4
EXPERTS_PER_GROUP = 8
N_EXPERTS = N_EXPERT_GROUPS * EXPERTS_PER_GROUP
D_EXPERT = D_MODEL // 4
RMS_EPS = 1e-6
NEG_BIG = -1e30

LANES = 128
SUBLANES = 8
MXU_DIM = 256
VMEM_LIMIT = 60 * 1024 * 1024

HEADS_PER_GROUP = MXU_DIM // HEAD_DIM
N_HEAD_GROUPS = N_HEADS // HEADS_PER_GROUP
ATTN_QBLOCK = LOOKBACK_CHUNKS * CHUNK
REL_PAD = -(-(2 * MAX_REL + 1) // LANES) * LANES
BIAS_WIDTH = -(-(BAND + CHUNK - 1) // LANES) * LANES
ROUTER_LANES = LANES
ROUTER_EXPERT_LANE0 = N_EXPERT_GROUPS
SSM_CHUNK = 16
SSM_LANE_GROUPS = LANES // SSM_GROUP
SSM_BLOCKS = D_MODEL // LANES
SSM_BLOCK_STATE = SSM_LANE_GROUPS * SSM_STATE
TOKEN_ROWS = D_MODEL // LANES
MOE_TILE = 256
R_GROUP, R_RANK = range(2)


def _dot(a, b):
    return jnp.dot(a, b, preferred_element_type=F32)


def _rms(x, g):
    return x * lax.rsqrt(jnp.mean(x * x, axis=-1, keepdims=True) + RMS_EPS) * g


def _sigmoid(x):
    return 1.0 / (1.0 + jnp.exp(-x))


def _gelu_tanh(x):
    c = np.float32(np.sqrt(2.0 / np.pi))
    return 0.5 * x * (1.0 + jnp.tanh(c * (x + np.float32(0.044715) * (x * x * x))))


def _params(*sem):
    return pltpu.CompilerParams(dimension_semantics=sem, vmem_limit_bytes=VMEM_LIMIT)


def _resident(shape):
    return pl.BlockSpec(shape, lambda *_: (0,) * len(shape), pipeline_mode=pl.Buffered(1))


def _cast_once(w_ref, wb):
    @pl.when(pl.program_id(0) == 0)
    def _():
        wb[...] = w_ref[...].astype(BF16)


def _qkv_kernel(x_ref, g_ref, w_ref, o_ref, wb):
    _cast_once(w_ref, wb)
    xn = _rms(x_ref[...], g_ref[...]).astype(BF16)
    for c in range(3):
        acc = _dot(xn, wb[:, c * D_MODEL:(c + 1) * D_MODEL])
        if c == 0:
            acc = acc * np.float32(HEAD_DIM ** -0.5)
        o_ref[:, c * D_MODEL:(c + 1) * D_MODEL] = acc.astype(BF16)


def _qkv(x, g, w, tm=512):
    t = x.shape[0]
    return pl.pallas_call(
        _qkv_kernel,
        grid=(t // tm,),
        in_specs=[pl.BlockSpec((tm, D_MODEL), lambda i: (i, 0)),
                  pl.BlockSpec((1, D_MODEL), lambda i: (0, 0)),
                  _resident(w.shape)],
        out_specs=pl.BlockSpec((tm, 3 * D_MODEL), lambda i: (i, 0)),
        out_shape=jax.ShapeDtypeStruct((t, 3 * D_MODEL), BF16),
        scratch_shapes=[pltpu.VMEM(w.shape, BF16)],
        compiler_params=_params("arbitrary"),
        name="qkv",
    )(x, g, w)


def _expand_rel_bias(rel_ref, bias):
    rel = rel_ref[0]
    r1 = rel.astype(BF16)
    r2 = (rel - r1.astype(F32)).astype(BF16)
    r3 = (rel - r1.astype(F32) - r2.astype(F32)).astype(BF16)
    j = lax.broadcasted_iota(jnp.int32, (REL_PAD, BIAS_WIDTH), 1)
    src = jnp.clip(BAND - 1 - j, -MAX_REL, MAX_REL) + MAX_REL
    sel = jnp.where(lax.broadcasted_iota(jnp.int32, (REL_PAD, BIAS_WIDTH), 0) == src, 1.0, 0.0).astype(BF16)
    u = (_dot(r1, sel) + _dot(r2, sel)) + _dot(r3, sel)
    for h in range(HEADS_PER_GROUP):
        rows = jnp.broadcast_to(u[h:h + 1, :], (CHUNK, BIAS_WIDTH))
        rows = pltpu.roll(rows, BIAS_WIDTH - (CHUNK - 1), 1, stride=1, stride_axis=0)
        bias[h * CHUNK:(h + 1) * CHUNK, :] = rows[:, :BAND]


def _attn_kernel(q_ref, kp_ref, kc_ref, vp_ref, vc_ref, rel_ref, o_ref, kk, vv, bias):
    qb = pl.program_id(2)

    @pl.when(qb == 0)
    def _():
        _expand_rel_bias(rel_ref, bias)

    kk[0:ATTN_QBLOCK, :] = kp_ref[...]
    kk[ATTN_QBLOCK:, :] = kc_ref[...]
    vv[0:ATTN_QBLOCK, :] = vp_ref[...]
    vv[ATTN_QBLOCK:, :] = vc_ref[...]
    lane_head = lax.broadcasted_iota(jnp.int32, (CHUNK, MXU_DIM), 1) // HEAD_DIM
    col = lax.broadcasted_iota(jnp.int32, (1, BAND), 1)

    def chunks(first_block):
        for j in range(ATTN_QBLOCK // CHUNK):
            qj = q_ref[j * CHUNK:(j + 1) * CHUNK, :]
            lhs = jnp.concatenate(
                [jnp.where(lane_head == h, qj, jnp.zeros_like(qj)) for h in range(HEADS_PER_GROUP)], axis=0)
            kwin = kk[j * CHUNK:j * CHUNK + BAND, :]
            vwin = vv[j * CHUNK:j * CHUNK + BAND, :]
            s = lax.dot_general(lhs, kwin, (((1,), (1,)), ((), ())), preferred_element_type=F32) + bias[...]
            if first_block:
                s = jnp.where(col + j * CHUNK >= ATTN_QBLOCK, s, NEG_BIG)
            m = jnp.max(s, axis=-1, keepdims=True)
            p = jnp.exp(s - m)
            l = jnp.sum(p, axis=-1, keepdims=True)
            o_all = _dot(p.astype(BF16), vwin) * (1.0 / l)
            o = jnp.zeros((CHUNK, MXU_DIM), F32)
            for h in range(HEADS_PER_GROUP):
                o = o + jnp.where(lane_head == h, o_all[h * CHUNK:(h + 1) * CHUNK, :], 0.0)
            o_ref[j * CHUNK:(j + 1) * CHUNK, :] = o.astype(BF16)

    @pl.when(qb == 0)
    def _():
        chunks(True)

    @pl.when(qb > 0)
    def _():
        chunks(False)


def _attention(qkv, rel, batch, seq):
    nqb = seq // ATTN_QBLOCK
    kcol = D_MODEL // MXU_DIM
    blk = (ATTN_QBLOCK, MXU_DIM)
    cur = lambda off: (lambda b, g, i: (b * nqb + i, off + g))
    prev = lambda off: (lambda b, g, i: (b * nqb + jnp.maximum(i - 1, 0), off + g))
    return pl.pallas_call(
        _attn_kernel,
        grid=(batch, N_HEAD_GROUPS, nqb),
        in_specs=[pl.BlockSpec(blk, cur(0)),
                  pl.BlockSpec(blk, prev(kcol)), pl.BlockSpec(blk, cur(kcol)),
                  pl.BlockSpec(blk, prev(2 * kcol)), pl.BlockSpec(blk, cur(2 * kcol)),
                  pl.BlockSpec((1, SUBLANES, REL_PAD), lambda b, g, i: (g, 0, 0))],
        out_specs=pl.BlockSpec(blk, cur(0)),
        out_shape=jax.ShapeDtypeStruct((batch * seq, D_MODEL), BF16),
        scratch_shapes=[pltpu.VMEM((2 * ATTN_QBLOCK, MXU_DIM), BF16),
                        pltpu.VMEM((2 * ATTN_QBLOCK, MXU_DIM), BF16),
                        pltpu.VMEM((HEADS_PER_GROUP * CHUNK, BAND), F32)],
        compiler_params=_params("parallel", "parallel", "arbitrary"),
        name="attn",
    )(qkv, qkv, qkv, qkv, qkv, rel)


def _rel_bias_rows(rel_bias):
    rel = rel_bias.astype(F32).reshape(N_HEAD_GROUPS, HEADS_PER_GROUP, 2 * MAX_REL + 1)
    return jnp.pad(rel, ((0, 0), (0, SUBLANES - HEADS_PER_GROUP), (0, REL_PAD - (2 * MAX_REL + 1))))


def _router_logits(hn, wr_ref, br_ref):
    a_hi = hn.astype(BF16)
    a_lo = (hn - a_hi.astype(F32)).astype(BF16)
    w = wr_ref[...]
    w_hi = w.astype(BF16)
    w_lo = (w - w_hi.astype(F32)).astype(BF16)
    return _dot(a_hi, w_hi) + (_dot(a_hi, w_lo) + _dot(a_lo, w_hi)) + br_ref[...]


def _route_group(lg, base):
    rows = lg.shape[0]
    lane = lax.broadcasted_iota(jnp.int32, lg.shape, 1)
    gl = jnp.where(lane < N_EXPERT_GROUPS, lg, np.float32(-np.inf))
    gmax = jnp.max(gl, axis=-1, keepdims=True)
    gidx = jnp.min(jnp.where(gl == gmax, lane, np.int32(ROUTER_LANES)), axis=-1, keepdims=True)
    og = lane == gidx
    tri = (lax.broadcasted_iota(jnp.int32, (rows, rows), 0)
           > lax.broadcasted_iota(jnp.int32, (rows, rows), 1))
    tri = jnp.where(tri, 1.0, 0.0).astype(BF16)
    ogf = jnp.where(og, 1.0, 0.0)
    rank = jnp.sum(jnp.where(og, base + _dot(tri, ogf.astype(BF16)), 0.0), axis=-1, keepdims=True)
    rec = jnp.where(lane == R_GROUP, gidx.astype(F32), jnp.where(lane == R_RANK, rank, 0.0))
    return rec, base + jnp.sum(ogf, axis=0, keepdims=True)


def _route_experts(lg, g):
    lane = lax.broadcasted_iota(jnp.int32, lg.shape, 1)
    ninf = np.float32(-np.inf)
    big = np.int32(ROUTER_LANES)
    gmask = lane < N_EXPERT_GROUPS
    gmax = jnp.max(jnp.where(gmask, lg, ninf), axis=-1, keepdims=True)
    ge = jnp.where(gmask, jnp.exp(lg - gmax), 0.0)
    gprob = jnp.sum(jnp.where(lane == g, ge, 0.0), axis=-1, keepdims=True) / jnp.sum(ge, axis=-1, keepdims=True)
    lo = ROUTER_EXPERT_LANE0 + EXPERTS_PER_GROUP * g
    el = jnp.where((lane >= lo) & (lane < lo + EXPERTS_PER_GROUP), lg, ninf)
    l1 = jnp.max(el, axis=-1, keepdims=True)
    i1 = jnp.min(jnp.where(el == l1, lane, big), axis=-1, keepdims=True)
    el2 = jnp.where(lane == i1, ninf, el)
    l2 = jnp.max(el2, axis=-1, keepdims=True)
    i2 = jnp.min(jnp.where(el2 == l2, lane, big), axis=-1, keepdims=True)
    t = jnp.exp(l2 - l1)
    w1 = gprob / (1.0 + t)
    return i1, w1, i2, w1 * t


def _to_token_tiles(ref, x):
    rows = x.shape[0]
    for j in range(TOKEN_ROWS):
        ref[pl.ds(j, rows, stride=TOKEN_ROWS), :] = x[:, j * LANES:(j + 1) * LANES]


def _from_token_tiles(ref, rows):
    return jnp.concatenate([ref[pl.ds(j, rows, stride=TOKEN_ROWS), :] for j in range(TOKEN_ROWS)], axis=1)


def _post_kernel(glu, y_ref, h_ref, w_ref, g_ref, wr_ref, br_ref, hout_ref, hntt_ref, route_ref, cnt_ref, wb):
    @pl.when(pl.program_id(0) == 0)
    def _():
        cnt_ref[...] = jnp.zeros_like(cnt_ref)

    _cast_once(w_ref, wb)
    y = y_ref[...].astype(BF16)
    if glu:
        mix = _dot(y, wb[:, :D_MODEL]) * _sigmoid(_dot(y, wb[:, D_MODEL:]))
    else:
        mix = _dot(y, wb[...])
    h = h_ref[...] + mix
    hout_ref[...] = h
    hn = _rms(h, g_ref[...])
    _to_token_tiles(hntt_ref, hn)
    route_ref[...], cnt_ref[...] = _route_group(_router_logits(hn, wr_ref, br_ref), cnt_ref[...])


def _post(y, h, w, g, wr, br, glu, tm=512):
    t = h.shape[0]
    row = lambda i: (i, 0)
    fixed = lambda i: (0, 0)
    return pl.pallas_call(
        functools.partial(_post_kernel, glu),
        grid=(t // tm,),
        in_specs=[pl.BlockSpec((tm, D_MODEL), row), pl.BlockSpec((tm, D_MODEL), row),
                  _resident(w.shape), pl.BlockSpec((1, D_MODEL), fixed),
                  pl.BlockSpec((D_MODEL, ROUTER_LANES), fixed), pl.BlockSpec((1, ROUTER_LANES), fixed)],
        out_specs=[pl.BlockSpec((tm, D_MODEL), row), pl.BlockSpec((tm * TOKEN_ROWS, LANES), row),
                   pl.BlockSpec((tm, ROUTER_LANES), row), pl.BlockSpec((1, ROUTER_LANES), fixed)],
        out_shape=[jax.ShapeDtypeStruct((t, D_MODEL), F32), jax.ShapeDtypeStruct((t * TOKEN_ROWS, LANES), F32),
                   jax.ShapeDtypeStruct((t, ROUTER_LANES), F32), jax.ShapeDtypeStruct((1, ROUTER_LANES), F32)],
        scratch_shapes=[pltpu.VMEM(w.shape, BF16)],
        compiler_params=_params("arbitrary"),
        name="post_glu" if glu else "post_attn",
    )(y, h, w, g, wr, br)


def _moe_tiles(t):
    return t // MOE_TILE + N_EXPERT_GROUPS


def _dispatch_plan(route, counts, t):
    g = route[:, R_GROUP].astype(jnp.int32)
    rank = route[:, R_RANK].astype(jnp.int32)
    cnt = counts[0, :N_EXPERT_GROUPS].astype(jnp.int32)
    padded = ((cnt + (MOE_TILE - 1)) // MOE_TILE) * MOE_TILE
    ends = jnp.cumsum(padded)
    off = ends - padded
    onehot = g[:, None] == jnp.arange(N_EXPERT_GROUPS, dtype=jnp.int32)
    pos = jnp.sum(jnp.where(onehot, off, 0), axis=-1) + rank
    ntiles = _moe_tiles(t)
    tile_start = jnp.arange(ntiles, dtype=jnp.int32) * MOE_TILE
    tile_group = jnp.minimum(jnp.sum((tile_start[:, None] >= ends[None, :]).astype(jnp.int32), axis=1),
                             N_EXPERT_GROUPS - 1)
    n_active = (ends[-1] // MOE_TILE).reshape(1)
    plan = jnp.concatenate([off + cnt, padded - cnt, n_active])
    groups = jnp.arange(N_EXPERT_GROUPS, dtype=jnp.int32)
    later = (groups[None, :] > groups[:, None]) & (cnt[None, :] > 0)
    following = jnp.min(jnp.where(later, groups[None, :], N_EXPERT_GROUPS), axis=1)
    following = jnp.where(following == N_EXPERT_GROUPS, -1, following)
    next_group = jnp.sum(jnp.where(tile_group[:, None] == groups[None, :], following[None, :], 0), axis=1)
    return pos, tile_group, n_active, next_group, plan


def _token_tile(ref, i):
    return ref.at[pl.ds(pl.multiple_of(i * TOKEN_ROWS, TOKEN_ROWS), TOKEN_ROWS), :]


def _zero_fill(plan_ref, xs_ref, zbuf, zsem, ntiles):
    nseg = N_EXPERT_GROUPS
    zbuf[...] = jnp.zeros_like(zbuf)
    half = MOE_TILE // 2
    sizes = [1 << b for b in reversed(range(half.bit_length()))]

    def chunk(first, size):
        return pltpu.make_async_copy(zbuf.at[pl.ds(0, size * TOKEN_ROWS), :],
                                     xs_ref.at[pl.ds(pl.multiple_of(first * TOKEN_ROWS, TOKEN_ROWS),
                                                     size * TOKEN_ROWS), :], zsem)

    def pads(wait):
        def body(e, c):
            first = plan_ref[e]
            n = plan_ref[nseg + e]
            for size in sizes:
                hit = (n & size) != 0

                @pl.when(hit)
                def _():
                    cp = chunk(first, size)
                    cp.wait() if wait else cp.start()

                first = first + jnp.where(hit, size, 0)
            return c
        lax.fori_loop(0, nseg, body, 0)

    def tail(wait):
        def body(i, c):
            for k in range(2):
                cp = chunk(i * MOE_TILE + k * half, half)
                cp.wait() if wait else cp.start()
            return c
        lax.fori_loop(plan_ref[2 * nseg], ntiles, body, 0)

    pads(False)
    tail(False)
    pads(True)
    tail(True)


def _dispatch_kernel(ntiles, plan_ref, pos_ref, hn_ref, xs_ref, sem, zbuf, zsem):
    rows = hn_ref.shape[0] // TOKEN_ROWS

    @pl.when(pl.program_id(0) == 0)
    def _():
        _zero_fill(plan_ref, xs_ref, zbuf, zsem, ntiles)

    def copy(r):
        return pltpu.make_async_copy(_token_tile(hn_ref, r), _token_tile(xs_ref, pos_ref[0, 0, r]), sem)

    def issue(r, c):
        copy(2 * r).start(priority=0)
        copy(2 * r + 1).start(priority=1)
        return c

    lax.fori_loop(0, rows // 2, issue, 0, unroll=4)
    pltpu.make_async_copy(hn_ref, xs_ref.at[pl.ds(0, rows * TOKEN_ROWS), :], sem).wait()


def _pos_blocks(pos, tm):
    return pos.reshape(pos.shape[0] // tm, 1, tm)


def _dispatch(plan, pos, hn_tt, t, tm=1024):
    ntiles = _moe_tiles(t)
    nrows = ntiles * MOE_TILE * TOKEN_ROWS
    return pl.pallas_call(
        functools.partial(_dispatch_kernel, ntiles),
        grid_spec=pltpu.PrefetchScalarGridSpec(
            num_scalar_prefetch=1,
            grid=(t // tm,),
            in_specs=[pl.BlockSpec((1, 1, tm), lambda i, plan: (i, 0, 0), memory_space=pltpu.SMEM),
                      pl.BlockSpec((tm * TOKEN_ROWS, LANES), lambda i, plan: (i, 0))],
            out_specs=pl.BlockSpec(memory_space=pl.ANY),
            scratch_shapes=[pltpu.SemaphoreType.DMA(()),
                            pltpu.VMEM((MOE_TILE // 2 * TOKEN_ROWS, LANES), F32),
                            pltpu.SemaphoreType.DMA(())]),
        out_shape=jax.ShapeDtypeStruct((nrows, LANES), F32),
        compiler_params=_params("arbitrary"),
        name="moe_dispatch",
    )(plan, _pos_blocks(pos, tm), hn_tt)


def _expert_kernel(layer, tg_ref, na_ref, ng_ref, xs_ref, wr_ref, br_ref, wg_hbm, wu_hbm, wd_hbm, ys_ref,
                   wg_f, wu_f, wd_f, wgu_s, wd_s, sems):
    i = pl.program_id(0)
    g = tg_ref[i]
    new_group = (i == 0) | (tg_ref[jnp.maximum(i - 1, 0)] != g)
    width = EXPERTS_PER_GROUP * D_EXPERT

    def fetch(group):
        return [pltpu.make_async_copy(w.at[layer, group], stage, sems.at[k])
                for k, (w, stage) in enumerate(((wg_hbm, wg_f), (wu_hbm, wu_f), (wd_hbm, wd_f)))]

    @pl.when(i < na_ref[0])
    def _():
        @pl.when(new_group)
        def _():
            @pl.when(i == 0)
            def _():
                for cp in fetch(g):
                    cp.start()

            for cp in fetch(g):
                cp.wait()
            for e in range(EXPERTS_PER_GROUP):
                cols = slice(e * D_EXPERT, (e + 1) * D_EXPERT)
                wgu_s[:, cols] = wg_f[e].astype(BF16)
                wgu_s[:, width + e * D_EXPERT:width + (e + 1) * D_EXPERT] = wu_f[e].astype(BF16)
                wd_s[cols, :] = wd_f[e].astype(BF16)
            nxt = ng_ref[i]

            @pl.when(nxt >= 0)
            def _():
                for cp in fetch(nxt):
                    cp.start()

        x = _from_token_tiles(xs_ref, MOE_TILE)
        i1, w1, i2, w2 = _route_experts(_router_logits(x, wr_ref, br_ref), g)
        au = _dot(x.astype(BF16), wgu_s[...])
        first = ROUTER_EXPERT_LANE0 + g * EXPERTS_PER_GROUP
        hid = []
        for e in range(EXPERTS_PER_GROUP):
            a = au[:, e * D_EXPERT:(e + 1) * D_EXPERT]
            u = au[:, width + e * D_EXPERT:width + (e + 1) * D_EXPERT]
            gate = jnp.where(i1 == first + e, w1, 0.0) + jnp.where(i2 == first + e, w2, 0.0)
            hid.append(((a * _sigmoid(a)) * u * gate).astype(BF16))
        _to_token_tiles(ys_ref, _dot(jnp.concatenate(hid, axis=1), wd_s[...]))


def _experts(layer, tile_group, n_active, next_group, xs, wr, br, w_gate, w_up, w_down):
    ntiles = tile_group.shape[0]
    last = lambda i, na: jnp.minimum(i, na[0] - 1)
    tile = pl.BlockSpec((MOE_TILE * TOKEN_ROWS, LANES), lambda i, tg, na, ng: (last(i, na), 0))
    fixed = lambda a: pl.BlockSpec(a.shape, lambda i, tg, na, ng: (0, 0))
    grouped = lambda w: w.reshape((w.shape[0], N_EXPERT_GROUPS, EXPERTS_PER_GROUP) + w.shape[2:])
    stage = lambda w: pltpu.VMEM((EXPERTS_PER_GROUP,) + w.shape[2:], F32)
    hbm = pl.BlockSpec(memory_space=pl.ANY)
    width = EXPERTS_PER_GROUP * D_EXPERT
    return pl.pallas_call(
        functools.partial(_expert_kernel, layer),
        grid_spec=pltpu.PrefetchScalarGridSpec(
            num_scalar_prefetch=3,
            grid=(ntiles,),
            in_specs=[tile, fixed(wr), fixed(br), hbm, hbm, hbm],
            out_specs=tile,
            scratch_shapes=[stage(w_gate), stage(w_up), stage(w_down),
                            pltpu.VMEM((D_MODEL, 2 * width), BF16), pltpu.VMEM((width, D_MODEL), BF16),
                            pltpu.SemaphoreType.DMA((3,))]),
        out_shape=jax.ShapeDtypeStruct(xs.shape, F32),
        input_output_aliases={3: 0},
        compiler_params=_params("arbitrary"),
        name="moe_experts",
    )(tile_group, n_active, next_group, xs, wr, br, grouped(w_gate), grouped(w_up), grouped(w_down))


def _combine_kernel(emit_h, pos_ref, nxt_ref, h_ref, gn_ref, ys_ref, *refs):
    outs, (buf0, buf1, sem0, sem1) = refs[:-4], refs[-4:]
    bufs, sems = (buf0, buf1), (sem0, sem1)
    rows = h_ref.shape[0]
    i = pl.program_id(0)

    def gather(idx_ref, slot):
        def copy(r):
            return pltpu.make_async_copy(_token_tile(ys_ref, idx_ref[0, 0, r]), _token_tile(bufs[slot], r),
                                         sems[slot])

        def issue(r, c):
            copy(2 * r).start(priority=0)
            copy(2 * r + 1).start(priority=1)
            return c

        lax.fori_loop(0, rows // 2, issue, 0, unroll=4)

    @pl.when(i == 0)
    def _():
        gather(pos_ref, 0)

    for slot in range(2):
        @pl.when(i % 2 == slot)
        def _():
            @pl.when(i + 1 < pl.num_programs(0))
            def _():
                gather(nxt_ref, 1 - slot)

            pltpu.make_async_copy(ys_ref.at[pl.ds(0, rows * TOKEN_ROWS), :], bufs[slot], sems[slot]).wait()
            h = h_ref[...] + _from_token_tiles(bufs[slot], rows)
            if emit_h:
                outs[0][...] = h
            outs[-1][...] = _rms(h, gn_ref[...])


def _combine(pos, h, gnext, ys, emit_h, tm=512):
    t = h.shape[0]
    nblk = t // tm
    row = lambda i: (i, 0)
    blocks = _pos_blocks(pos, tm)
    smem = lambda imap: pl.BlockSpec((1, 1, tm), imap, memory_space=pltpu.SMEM)
    n_out = 2 if emit_h else 1
    out = pl.pallas_call(
        functools.partial(_combine_kernel, emit_h),
        grid=(nblk,),
        in_specs=[smem(lambda i: (i, 0, 0)), smem(lambda i: (jnp.minimum(i + 1, nblk - 1), 0, 0)),
                  pl.BlockSpec((tm, D_MODEL), row),
                  pl.BlockSpec((1, D_MODEL), lambda i: (0, 0)),
                  pl.BlockSpec(memory_space=pl.ANY)],
        out_specs=[pl.BlockSpec((tm, D_MODEL), row)] * n_out,
        out_shape=[jax.ShapeDtypeStruct((t, D_MODEL), F32)] * n_out,
        scratch_shapes=[pltpu.VMEM((tm * TOKEN_ROWS, LANES), F32), pltpu.VMEM((tm * TOKEN_ROWS, LANES), F32),
                        pltpu.SemaphoreType.DMA(()), pltpu.SemaphoreType.DMA(())],
        compiler_params=_params("arbitrary"),
        name="moe_combine",
    )(blocks, blocks, h, gnext, ys)
    return (out[0], out[1]) if emit_h else (None, out[0])


def _moe(layer, h, hn_tt, route, counts, wr, br, w_gate, w_up, w_down, gnext, emit_h):
    t = h.shape[0]
    pos, tile_group, n_active, next_group, plan = _dispatch_plan(route, counts, t)
    xs = _dispatch(plan, pos, hn_tt, t)
    ys = _experts(layer, tile_group, n_active, next_group, xs, wr, br, w_gate, w_up, w_down)
    return _combine(pos, h, gnext, ys, emit_h)


def _dot3(a, b):
    a_hi = a.astype(BF16)
    a_lo = (a - a_hi.astype(F32)).astype(BF16)
    b_hi = b.astype(BF16)
    b_lo = (b - b_hi.astype(F32)).astype(BF16)
    return _dot(a_hi, b_hi) + (_dot(a_hi, b_lo) + _dot(a_lo, b_hi))


def _ssm_powers(lam_re, lam_im, log_step, lags):
    lam_re = jnp.minimum(lam_re, np.float32(-1e-4))
    step = jnp.exp(log_step)
    dec = jnp.exp(lags * (lam_re * step))
    ang = lags * (lam_im * step)
    return dec * jnp.cos(ang), dec * jnp.sin(ang)


def _ssm_operators(lamr_ref, lamc_ref, bre_ref, bim_ref, cre_ref, cim_ref, r_s, wd_s, vd_s):
    L, ns = SSM_CHUNK, SSM_BLOCK_STATE
    lam = lamr_ref[0]
    lre, lim, lst = lam[0:1], lam[1:2], lam[2:3]
    lag_rev = (L - 1 - lax.broadcasted_iota(jnp.int32, (L, 1), 0)).astype(F32)
    pr_re, pr_im = _ssm_powers(lre, lim, lst, lag_rev)
    ab_re, ab_im = _ssm_powers(lre, lim, lst, np.float32(1.0))
    lam_re = jnp.minimum(lre, np.float32(-1e-4))
    denom = lam_re * lam_re + lim * lim
    num_re = ab_re - 1.0
    f_re = (num_re * lam_re + ab_im * lim) / denom
    f_im = (ab_im * lam_re - num_re * lim) / denom
    bre, bim = bre_ref[0], bim_ref[0]
    bb_re = f_re * bre - f_im * bim
    bb_im = f_re * bim + f_im * bre
    cre, cim = cre_ref[0], cim_ref[0]
    for s in range(L):
        a_re, a_im = pr_re[s:s + 1], pr_im[s:s + 1]
        w_re = a_re * bb_re - a_im * bb_im
        w_im = a_re * bb_im + a_im * bb_re
        rows = slice(s * LANES, (s + 1) * LANES)
        wd_s[rows, 0:ns] = w_re.astype(BF16)
        wd_s[rows, ns:2 * ns] = w_im.astype(BF16)
        r_s[rows, :] = (_dot3(w_re, cre) - _dot3(w_im, cim)).astype(BF16)
    r_s[L * LANES:, :] = jnp.zeros((LANES, LANES), BF16)
    lamc = lamc_ref[0]
    lag1 = (1 + lax.broadcasted_iota(jnp.int32, (1, L), 1)).astype(F32)
    p1_re, p1_im = _ssm_powers(lamc[:, 0:1], lamc[:, 1:2], lamc[:, 2:3], lag1)
    for t in range(L):
        a_re, a_im = p1_re[:, t:t + 1], p1_im[:, t:t + 1]
        cols = slice(t * LANES, (t + 1) * LANES)
        vd_s[0:ns, cols] = (cre * a_re - cim * a_im).astype(BF16)
        vd_s[ns:2 * ns, cols] = (-(cre * a_im + cim * a_re)).astype(BF16)
    return _ssm_powers(lre, lim, lst, np.float32(L))


def _ssm_kernel(nch, x_ref, lamr_ref, lamc_ref, bre_ref, bim_ref, cre_ref, cim_ref, d_ref, o_ref,
                r_s, wd_s, vd_s, xcat, st, xprev):
    nb = x_ref.shape[0]
    ns = SSM_BLOCK_STATE
    ar, ai = _ssm_operators(lamr_ref, lamc_ref, bre_ref, bim_ref, cre_ref, cim_ref, r_s, wd_s, vd_s)
    for s in range(SSM_CHUNK):
        for b in range(nb):
            xs = x_ref[b, pl.ds(s, nch, stride=SSM_CHUNK), :]
            xcat[b * nch:(b + 1) * nch, s * LANES:(s + 1) * LANES] = xs.astype(BF16)
    st[...] = _dot(xcat[...], wd_s[...])

    def step(k, carry):
        out = []
        for b in range(nb):
            xr, xi = carry[b]
            row = b * nch + k
            sr = st[pl.ds(row, 1), 0:ns]
            si = st[pl.ds(row, 1), ns:2 * ns]
            st[pl.ds(row, 1), 0:ns] = xr
            st[pl.ds(row, 1), ns:2 * ns] = xi
            out.append((ar * xr - ai * xi + sr, ar * xi + ai * xr + si))
        return tuple(out)

    zero = jnp.zeros((1, ns), F32)
    lax.fori_loop(0, nch, step, tuple((zero, zero) for _ in range(nb)))
    xprev[...] = st[...].astype(BF16)
    dskip = d_ref[...]
    L = SSM_CHUNK
    for t in range(0, L, 2):
        taps = jnp.concatenate([r_s[(L - 1 - t) * LANES:, :], r_s[(L - 2 - t) * LANES:L * LANES, :]], axis=1)
        y2 = (_dot(xcat[:, :(t + 2) * LANES], taps)
              + _dot(xprev[...], vd_s[:, t * LANES:(t + 2) * LANES]))
        for k in range(2):
            y = y2[:, k * LANES:(k + 1) * LANES]
            for b in range(nb):
                u = x_ref[b, pl.ds(t + k, nch, stride=L), :]
                o_ref[b, pl.ds(t + k, nch, stride=L), :] = _gelu_tanh(y[b * nch:(b + 1) * nch, :] + dskip * u)


def _ssm(x3, params, dskip):
    nb, seq, _ = x3.shape
    nch = seq // SSM_CHUNK
    rows = nb * nch
    ns = SSM_BLOCK_STATE
    blk = pl.BlockSpec((nb, seq, LANES), lambda g: (0, 0, g))
    per = lambda shape: pl.BlockSpec((1,) + shape, lambda g: (g, 0, 0))
    return pl.pallas_call(
        functools.partial(_ssm_kernel, nch),
        grid=(SSM_BLOCKS,),
        in_specs=[blk, per((3, ns)), per((ns, 3)), per((LANES, ns)), per((LANES, ns)),
                  per((ns, LANES)), per((ns, LANES)), pl.BlockSpec((1, LANES), lambda g: (0, g))],
        out_specs=blk,
        out_shape=jax.ShapeDtypeStruct(x3.shape, F32),
        scratch_shapes=[pltpu.VMEM(((SSM_CHUNK + 1) * LANES, LANES), BF16),
                        pltpu.VMEM((SSM_CHUNK * LANES, 2 * ns), BF16),
                        pltpu.VMEM((2 * ns, SSM_CHUNK * LANES), BF16),
                        pltpu.VMEM((rows, SSM_CHUNK * LANES), BF16),
                        pltpu.VMEM((rows, 2 * ns), F32),
                        pltpu.VMEM((rows, 2 * ns), BF16)],
        compiler_params=_params("parallel"),
        name="ssm",
    )(x3, *params, dskip)


def _ssm_params(lambda_re, lambda_im, log_step, b_re, b_im, c_re, c_im):
    nb, ng, ns = SSM_BLOCKS, SSM_LANE_GROUPS, SSM_BLOCK_STATE
    lam = jnp.stack([lambda_re.astype(F32), lambda_im.astype(F32),
                     jnp.broadcast_to(log_step.astype(F32)[:, None], lambda_re.shape)])
    lam_rows = jnp.transpose(lam.reshape(3, nb, ns), (1, 0, 2))
    lam_cols = jnp.transpose(lam_rows, (0, 2, 1))
    eye = jnp.eye(ng, dtype=F32)

    def blockdiag(m):
        a, b = m.shape[-2:]
        out = jnp.einsum('ngab,gh->ngahb', m.astype(F32).reshape(nb, ng, a, b), eye)
        return out.reshape(nb, ng * a, ng * b)

    bt = lambda m: blockdiag(jnp.swapaxes(m, -1, -2))
    return lam_rows, lam_cols, bt(b_re), bt(b_im), bt(c_re), bt(c_im)


def _router_tables(w_group, b_group, w_expert, b_expert):
    w = jnp.concatenate([w_group.astype(F32),
                         jnp.transpose(w_expert.astype(F32), (1, 0, 2)).reshape(D_MODEL, N_EXPERTS)], axis=1)
    b = jnp.concatenate([b_group.astype(F32), b_expert.astype(F32).reshape(N_EXPERTS)])
    pad = ROUTER_LANES - w.shape[1]
    return jnp.pad(w, ((0, 0), (0, pad))), jnp.pad(b, (0, pad)).reshape(1, ROUTER_LANES)


def kernel(x, norm_mix, norm_ffn, norm_final, attn_w_qkv, attn_w_o, attn_rel_bias, ssm_lambda_re, ssm_lambda_im, ssm_log_step, ssm_b_re, ssm_b_im, ssm_c_re, ssm_c_im, ssm_d, ssm_w_out, moe_w_group_router, moe_b_group_router, moe_w_expert_router, moe_b_expert_router, moe_w_gate, moe_w_up, moe_w_down):
    batch, seq, d = x.shape
    t = batch * seq
    row = lambda v: v.astype(F32).reshape(1, d)
    h = x.astype(F32).reshape(t, d)

    qkv = _qkv(h, row(norm_mix[0]), attn_w_qkv[0].astype(F32))
    o = _attention(qkv, _rel_bias_rows(attn_rel_bias[0]), batch, seq)
    wr, br = _router_tables(moe_w_group_router[0], moe_b_group_router[0],
                            moe_w_expert_router[0], moe_b_expert_router[0])
    h, hn_tt, route, counts = _post(o, h, attn_w_o[0].astype(F32), row(norm_ffn[0]), wr, br, glu=False)
    h, hn32 = _moe(0, h, hn_tt, route, counts, wr, br, moe_w_gate, moe_w_up, moe_w_down, row(norm_mix[1]),
                   emit_h=True)

    ssm_params = _ssm_params(ssm_lambda_re[0], ssm_lambda_im[0], ssm_log_step[0],
                             ssm_b_re[0], ssm_b_im[0], ssm_c_re[0], ssm_c_im[0])
    y = _ssm(hn32.reshape(batch, seq, d), ssm_params, row(ssm_d[0]))
    wr, br = _router_tables(moe_w_group_router[1], moe_b_group_router[1],
                            moe_w_expert_router[1], moe_b_expert_router[1])
    h, hn_tt, route, counts = _post(y.reshape(t, d), h, ssm_w_out[0].astype(F32), row(norm_ffn[1]), wr, br, glu=True)
    _, out = _moe(1, h, hn_tt, route, counts, wr, br, moe_w_gate, moe_w_up, moe_w_down, row(norm_final),
                 emit_h=False)
    return out.reshape(batch, seq, d).astype(x.dtype)
```

```python
import functools

import jax
import jax.numpy as jnp
import numpy as np
from jax import lax
from jax.experimental import pallas as pl
from jax.experimental.pallas import tpu as pltpu

F32 = jnp.float32
BF16 = jnp.bfloat16

D_MODEL = 1024
CHUNK = 64
LOOKBACK_CHUNKS = 8
BAND = (LOOKBACK_CHUNKS + 1) * CHUNK
N_HEADS = 16
HEAD_DIM = D_MODEL // N_HEADS
MAX_REL = 128
SSM_GROUP = 16
SSM_GROUPS = D_MODEL // SSM_GROUP
SSM_STATE = 64
N_EXPERT_GROUPS = 4
EXPERTS_PER_GROUP = 8
N_EXPERTS = N_EXPERT_GROUPS * EXPERTS_PER_GROUP
D_EXPERT = D_MODEL // 4
RMS_EPS = 1e-6
NEG_BIG = -1e30

LANES = 128
SUBLANES = 8
MXU_DIM = 256
VMEM_LIMIT = 60 * 1024 * 1024

HEADS_PER_GROUP = MXU_DIM // HEAD_DIM
N_HEAD_GROUPS = N_HEADS // HEADS_PER_GROUP
ATTN_QBLOCK = LOOKBACK_CHUNKS * CHUNK
ATTN_QROWS = 2 * ATTN_QBLOCK
REL_PAD = -(-(2 * MAX_REL + 1) // LANES) * LANES
BIAS_WIDTH = -(-(BAND + CHUNK - 1) // LANES) * LANES
ROUTER_LANES = LANES
ROUTER_EXPERT_LANE0 = N_EXPERT_GROUPS
SSM_CHUNK = 16
SSM_LANE_GROUPS = LANES // SSM_GROUP
SSM_BLOCKS = D_MODEL // LANES
SSM_BLOCK_STATE = SSM_LANE_GROUPS * SSM_STATE
TOKEN_ROWS = D_MODEL // LANES
MOE_TILE = 256
R_GROUP, R_RANK = range(2)


def _dot(a, b):
    return jnp.dot(a, b, preferred_element_type=F32)


def _rms(x, g):
    return x * lax.rsqrt(jnp.mean(x * x, axis=-1, keepdims=True) + RMS_EPS) * g


def _sigmoid(x):
    return 1.0 / (1.0 + jnp.exp(-x))


def _gelu_tanh(x):
    c = np.float32(np.sqrt(2.0 / np.pi))
    return 0.5 * x * (1.0 + jnp.tanh(c * (x + np.float32(0.044715) * (x * x * x))))


def _params(*sem):
    return pltpu.CompilerParams(dimension_semantics=sem, vmem_limit_bytes=VMEM_LIMIT)


def _resident(shape):
    return pl.BlockSpec(shape, lambda *_: (0,) * len(shape), pipeline_mode=pl.Buffered(1))


def _cast_once(w_ref, wb):
    @pl.when(pl.program_id(0) == 0)
    def _():
        wb[...] = w_ref[...].astype(BF16)


def _qkv_kernel(x_ref, g_ref, w_ref, o_ref, wb):
    _cast_once(w_ref, wb)
    xn = _rms(x_ref[...], g_ref[...]).astype(BF16)
    for c in range(3):
        acc = _dot(xn, wb[:, c * D_MODEL:(c + 1) * D_MODEL])
        if c == 0:
            acc = acc * np.float32(HEAD_DIM ** -0.5)
        o_ref[:, c * D_MODEL:(c + 1) * D_MODEL] = acc.astype(BF16)


def _qkv(x, g, w, tm=512):
    t = x.shape[0]
    return pl.pallas_call(
        _qkv_kernel,
        grid=(t // tm,),
        in_specs=[pl.BlockSpec((tm, D_MODEL), lambda i: (i, 0)),
                  pl.BlockSpec((1, D_MODEL), lambda i: (0, 0)),
                  _resident(w.shape)],
        out_specs=pl.BlockSpec((tm, 3 * D_MODEL), lambda i: (i, 0)),
        out_shape=jax.ShapeDtypeStruct((t, 3 * D_MODEL), BF16),
        scratch_shapes=[pltpu.VMEM(w.shape, BF16)],
        compiler_params=_params("arbitrary"),
        name="qkv",
    )(x, g, w)


def _expand_rel_bias(rel_ref, bias):
    rel = rel_ref[0]
    r1 = rel.astype(BF16)
    r2 = (rel - r1.astype(F32)).astype(BF16)
    r3 = (rel - r1.astype(F32) - r2.astype(F32)).astype(BF16)
    j = lax.broadcasted_iota(jnp.int32, (REL_PAD, BIAS_WIDTH), 1)
    src = jnp.clip(BAND - 1 - j, -MAX_REL, MAX_REL) + MAX_REL
    sel = jnp.where(lax.broadcasted_iota(jnp.int32, (REL_PAD, BIAS_WIDTH), 0) == src, 1.0, 0.0).astype(BF16)
    u = (_dot(r1, sel) + _dot(r2, sel)) + _dot(r3, sel)
    for h in range(HEADS_PER_GROUP):
        rows = jnp.broadcast_to(u[h:h + 1, :], (CHUNK, BIAS_WIDTH))
        rows = pltpu.roll(rows, BIAS_WIDTH - (CHUNK - 1), 1, stride=1, stride_axis=0)
        bias[h * CHUNK:(h + 1) * CHUNK, :] = rows[:, :BAND]


def _attn_kernel(q_ref, kp_ref, kc_ref, vp_ref, vc_ref, rel_ref, o_ref, kk, vv, bias):
    qb = pl.program_id(2)

    @pl.when(qb == 0)
    def _():
        _expand_rel_bias(rel_ref, bias)

    kk[0:ATTN_QBLOCK, :] = kp_ref[...]
    kk[ATTN_QBLOCK:, :] = kc_ref[...]
    vv[0:ATTN_QBLOCK, :] = vp_ref[...]
    vv[ATTN_QBLOCK:, :] = vc_ref[...]
    lane_head = lax.broadcasted_iota(jnp.int32, (CHUNK, MXU_DIM), 1) // HEAD_DIM
    col = lax.broadcasted_iota(jnp.int32, (1, BAND), 1)

    def chunks(first_block):
        for j in range(ATTN_QROWS // CHUNK):
            qj = q_ref[j * CHUNK:(j + 1) * CHUNK, :]
            lhs = jnp.concatenate(
                [jnp.where(lane_head == h, qj, jnp.zeros_like(qj)) for h in range(HEADS_PER_GROUP)], axis=0)
            kwin = kk[j * CHUNK:j * CHUNK + BAND, :]
            vwin = vv[j * CHUNK:j * CHUNK + BAND, :]
            s = lax.dot_general(lhs, kwin, (((1,), (1,)), ((), ())), preferred_element_type=F32) + bias[...]
            if first_block:
                s = jnp.where(col + j * CHUNK >= ATTN_QBLOCK, s, NEG_BIG)
            m = jnp.max(s, axis=-1, keepdims=True)
            p = jnp.exp(s - m)
            l = jnp.sum(p, axis=-1, keepdims=True)
            o_all = _dot(p.astype(BF16), vwin) * (1.0 / l)
            o = jnp.zeros((CHUNK, MXU_DIM), F32)
            for h in range(HEADS_PER_GROUP):
                o = o + jnp.where(lane_head == h, o_all[h * CHUNK:(h + 1) * CHUNK, :], 0.0)
            o_ref[j * CHUNK:(j + 1) * CHUNK, :] = o.astype(BF16)

    @pl.when(qb == 0)
    def _():
        chunks(True)

    @pl.when(qb > 0)
    def _():
        chunks(False)


def _attention(qkv, rel, batch, seq):
    nqb = seq // ATTN_QROWS
    per = ATTN_QROWS // ATTN_QBLOCK
    kcol = D_MODEL // MXU_DIM
    blk = (ATTN_QROWS, MXU_DIM)
    back = (ATTN_QBLOCK, MXU_DIM)
    cur = lambda off: (lambda b, g, i: (b * nqb + i, off + g))
    prev = lambda off: (lambda b, g, i: ((b * nqb + i) * per - jnp.minimum(i, 1), off + g))
    return pl.pallas_call(
        _attn_kernel,
        grid=(batch, N_HEAD_GROUPS, nqb),
        in_specs=[pl.BlockSpec(blk, cur(0)),
                  pl.BlockSpec(back, prev(kcol)), pl.BlockSpec(blk, cur(kcol)),
                  pl.BlockSpec(back, prev(2 * kcol)), pl.BlockSpec(blk, cur(2 * kcol)),
                  pl.BlockSpec((1, SUBLANES, REL_PAD), lambda b, g, i: (g, 0, 0))],
        out_specs=pl.BlockSpec(blk, cur(0)),
        out_shape=jax.ShapeDtypeStruct((batch * seq, D_MODEL), BF16),
        scratch_shapes=[pltpu.VMEM((ATTN_QBLOCK + ATTN_QROWS, MXU_DIM), BF16),
                        pltpu.VMEM((ATTN_QBLOCK + ATTN_QROWS, MXU_DIM), BF16),
                        pltpu.VMEM((HEADS_PER_GROUP * CHUNK, BAND), F32)],
        compiler_params=_params("parallel", "parallel", "arbitrary"),
        name="attn",
    )(qkv, qkv, qkv, qkv, qkv, rel)


def _rel_bias_rows(rel_bias):
    rel = rel_bias.astype(F32).reshape(N_HEAD_GROUPS, HEADS_PER_GROUP, 2 * MAX_REL + 1)
    return jnp.pad(rel, ((0, 0), (0, SUBLANES - HEADS_PER_GROUP), (0, REL_PAD - (2 * MAX_REL + 1))))


def _router_logits(hn, wr_ref, br_ref):
    a_hi = hn.astype(BF16)
    a_lo = (hn - a_hi.astype(F32)).astype(BF16)
    w = wr_ref[...]
    w_hi = w.astype(BF16)
    w_lo = (w - w_hi.astype(F32)).astype(BF16)
    return _dot(a_hi, w_hi) + (_dot(a_hi, w_lo) + _dot(a_lo, w_hi)) + br_ref[...]


def _route_group(lg, base):
    rows = lg.shape[0]
    lane = lax.broadcasted_iota(jnp.int32, lg.shape, 1)
    gl = jnp.where(lane < N_EXPERT_GROUPS, lg, np.float32(-np.inf))
    gmax = jnp.max(gl, axis=-1, keepdims=True)
    gidx = jnp.min(jnp.where(gl == gmax, lane, np.int32(ROUTER_LANES)), axis=-1, keepdims=True)
    og = lane == gidx
    tri = (lax.broadcasted_iota(jnp.int32, (rows, rows), 0)
           > lax.broadcasted_iota(jnp.int32, (rows, rows), 1))
    tri = jnp.where(tri, 1.0, 0.0).astype(BF16)
    ogf = jnp.where(og, 1.0, 0.0)
    rank = jnp.sum(jnp.where(og, base + _dot(tri, ogf.astype(BF16)), 0.0), axis=-1, keepdims=True)
    rec = jnp.where(lane == R_GROUP, gidx.astype(F32), jnp.where(lane == R_RANK, rank, 0.0))
    return rec, base + jnp.sum(ogf, axis=0, keepdims=True)


def _route_experts(lg, g):
    lane = lax.broadcasted_iota(jnp.int32, lg.shape, 1)
    ninf = np.float32(-np.inf)
    big = np.int32(ROUTER_LANES)
    gmask = lane < N_EXPERT_GROUPS
    gmax = jnp.max(jnp.where(gmask, lg, ninf), axis=-1, keepdims=True)
    ge = jnp.where(gmask, jnp.exp(lg - gmax), 0.0)
    gprob = jnp.sum(jnp.where(lane == g, ge, 0.0), axis=-1, keepdims=True) / jnp.sum(ge, axis=-1, keepdims=True)
    lo = ROUTER_EXPERT_LANE0 + EXPERTS_PER_GROUP * g
    el = jnp.where((lane >= lo) & (lane < lo + EXPERTS_PER_GROUP), lg, ninf)
    l1 = jnp.max(el, axis=-1, keepdims=True)
    i1 = jnp.min(jnp.where(el == l1, lane, big), axis=-1, keepdims=True)
    el2 = jnp.where(lane == i1, ninf, el)
    l2 = jnp.max(el2, axis=-1, keepdims=True)
    i2 = jnp.min(jnp.where(el2 == l2, lane, big), axis=-1, keepdims=True)
    t = jnp.exp(l2 - l1)
    w1 = gprob / (1.0 + t)
    return i1, w1, i2, w1 * t


def _to_token_tiles(ref, x):
    rows = x.shape[0]
    for j in range(TOKEN_ROWS):
        ref[pl.ds(j, rows, stride=TOKEN_ROWS), :] = x[:, j * LANES:(j + 1) * LANES]


def _from_token_tiles(ref, rows):
    return jnp.concatenate([ref[pl.ds(j, rows, stride=TOKEN_ROWS), :] for j in range(TOKEN_ROWS)], axis=1)


def _post_kernel(glu, y_ref, h_ref, w_ref, g_ref, wr_ref, br_ref, hout_ref, hntt_ref, route_ref, cnt_ref, wb):
    @pl.when(pl.program_id(0) == 0)
    def _():
        cnt_ref[...] = jnp.zeros_like(cnt_ref)

    _cast_once(w_ref, wb)
    y = y_ref[...].astype(BF16)
    if glu:
        mix = _dot(y, wb[:, :D_MODEL]) * _sigmoid(_dot(y, wb[:, D_MODEL:]))
    else:
        mix = _dot(y, wb[...])
    h = h_ref[...] + mix
    hout_ref[...] = h
    hn = _rms(h, g_ref[...])
    _to_token_tiles(hntt_ref, hn)
    route_ref[...], cnt_ref[...] = _route_group(_router_logits(hn, wr_ref, br_ref), cnt_ref[...])


def _post(y, h, w, g, wr, br, glu, tm=512):
    t = h.shape[0]
    row = lambda i: (i, 0)
    fixed = lambda i: (0, 0)
    return pl.pallas_call(
        functools.partial(_post_kernel, glu),
        grid=(t // tm,),
        in_specs=[pl.BlockSpec((tm, D_MODEL), row), pl.BlockSpec((tm, D_MODEL), row),
                  _resident(w.shape), pl.BlockSpec((1, D_MODEL), fixed),
                  pl.BlockSpec((D_MODEL, ROUTER_LANES), fixed), pl.BlockSpec((1, ROUTER_LANES), fixed)],
        out_specs=[pl.BlockSpec((tm, D_MODEL), row), pl.BlockSpec((tm * TOKEN_ROWS, LANES), row),
                   pl.BlockSpec((tm, ROUTER_LANES), row), pl.BlockSpec((1, ROUTER_LANES), fixed)],
        out_shape=[jax.ShapeDtypeStruct((t, D_MODEL), F32), jax.ShapeDtypeStruct((t * TOKEN_ROWS, LANES), F32),
                   jax.ShapeDtypeStruct((t, ROUTER_LANES), F32), jax.ShapeDtypeStruct((1, ROUTER_LANES), F32)],
        scratch_shapes=[pltpu.VMEM(w.shape, BF16)],
        compiler_params=_params("arbitrary"),
        name="post_glu" if glu else "post_attn",
    )(y, h, w, g, wr, br)


def _moe_tiles(t):
    return t // MOE_TILE + N_EXPERT_GROUPS


def _dispatch_plan(route, counts, t):
    g = route[:, R_GROUP].astype(jnp.int32)
    rank = route[:, R_RANK].astype(jnp.int32)
    cnt = counts[0, :N_EXPERT_GROUPS].astype(jnp.int32)
    padded = ((cnt + (MOE_TILE - 1)) // MOE_TILE) * MOE_TILE
    ends = jnp.cumsum(padded)
    off = ends - padded
    onehot = g[:, None] == jnp.arange(N_EXPERT_GROUPS, dtype=jnp.int32)
    pos = jnp.sum(jnp.where(onehot, off, 0), axis=-1) + rank
    ntiles = _moe_tiles(t)
    tile_start = jnp.arange(ntiles, dtype=jnp.int32) * MOE_TILE
    tile_group = jnp.minimum(jnp.sum((tile_start[:, None] >= ends[None, :]).astype(jnp.int32), axis=1),
                             N_EXPERT_GROUPS - 1)
    n_active = (ends[-1] // MOE_TILE).reshape(1)
    plan = jnp.concatenate([off + cnt, padded - cnt, n_active])
    groups = jnp.arange(N_EXPERT_GROUPS, dtype=jnp.int32)
    later = (groups[None, :] > groups[:, None]) & (cnt[None, :] > 0)
    following = jnp.min(jnp.where(later, groups[None, :], N_EXPERT_GROUPS), axis=1)
    following = jnp.where(following == N_EXPERT_GROUPS, -1, following)
    next_group = jnp.sum(jnp.where(tile_group[:, None] == groups[None, :], following[None, :], 0), axis=1)
    return pos, tile_group, n_active, next_group, plan


def _token_tile(ref, i):
    return ref.at[pl.ds(pl.multiple_of(i * TOKEN_ROWS, TOKEN_ROWS), TOKEN_ROWS), :]


def _zero_fill(plan_ref, xs_ref, zbuf, zsem, ntiles):
    nseg = N_EXPERT_GROUPS
    zbuf[...] = jnp.zeros_like(zbuf)
    half = MOE_TILE // 2
    sizes = [1 << b for b in reversed(range(half.bit_length()))]

    def chunk(first, size):
        return pltpu.make_async_copy(zbuf.at[pl.ds(0, size * TOKEN_ROWS), :],
                                     xs_ref.at[pl.ds(pl.multiple_of(first * TOKEN_ROWS, TOKEN_ROWS),
                                                     size * TOKEN_ROWS), :], zsem)

    def pads(wait):
        def body(e, c):
            first = plan_ref[e]
            n = plan_ref[nseg + e]
            for size in sizes:
                hit = (n & size) != 0

                @pl.when(hit)
                def _():
                    cp = chunk(first, size)
                    cp.wait() if wait else cp.start()

                first = first + jnp.where(hit, size, 0)
            return c
        lax.fori_loop(0, nseg, body, 0)

    def tail(wait):
        def body(i, c):
            for k in range(2):
                cp = chunk(i * MOE_TILE + k * half, half)
                cp.wait() if wait else cp.start()
            return c
        lax.fori_loop(plan_ref[2 * nseg], ntiles, body, 0)

    pads(False)
    tail(False)
    pads(True)
    tail(True)


def _dispatch_kernel(ntiles, plan_ref, pos_ref, hn_ref, xs_ref, sem, zbuf, zsem):
    rows = hn_ref.shape[0] // TOKEN_ROWS

    @pl.when(pl.program_id(0) == 0)
    def _():
        _zero_fill(plan_ref, xs_ref, zbuf, zsem, ntiles)

    def copy(r):
        return pltpu.make_async_copy(_token_tile(hn_ref, r), _token_tile(xs_ref, pos_ref[0, 0, r]), sem)

    def issue(r, c):
        copy(2 * r).start(priority=0)
        copy(2 * r + 1).start(priority=1)
        return c

    lax.fori_loop(0, rows // 2, issue, 0, unroll=4)
    pltpu.make_async_copy(hn_ref, xs_ref.at[pl.ds(0, rows * TOKEN_ROWS), :], sem).wait()


def _pos_blocks(pos, tm):
    return pos.reshape(pos.shape[0] // tm, 1, tm)


def _dispatch(plan, pos, hn_tt, t, tm=1024):
    ntiles = _moe_tiles(t)
    nrows = ntiles * MOE_TILE * TOKEN_ROWS
    return pl.pallas_call(
        functools.partial(_dispatch_kernel, ntiles),
        grid_spec=pltpu.PrefetchScalarGridSpec(
            num_scalar_prefetch=1,
            grid=(t // tm,),
            in_specs=[pl.BlockSpec((1, 1, tm), lambda i, plan: (i, 0, 0), memory_space=pltpu.SMEM),
                      pl.BlockSpec((tm * TOKEN_ROWS, LANES), lambda i, plan: (i, 0))],
            out_specs=pl.BlockSpec(memory_space=pl.ANY),
            scratch_shapes=[pltpu.SemaphoreType.DMA(()),
                            pltpu.VMEM((MOE_TILE // 2 * TOKEN_ROWS, LANES), F32),
                            pltpu.SemaphoreType.DMA(())]),
        out_shape=jax.ShapeDtypeStruct((nrows, LANES), F32),
        compiler_params=_params("arbitrary"),
        name="moe_dispatch",
    )(plan, _pos_blocks(pos, tm), hn_tt)


def _expert_kernel(layer, tg_ref, na_ref, ng_ref, xs_ref, wr_ref, br_ref, wg_hbm, wu_hbm, wd_hbm, ys_ref,
                   wg_f, wu_f, wd_f, wgu_s, wd_s, sems):
    i = pl.program_id(0)
    g = tg_ref[i]
    new_group = (i == 0) | (tg_ref[jnp.maximum(i - 1, 0)] != g)
    width = EXPERTS_PER_GROUP * D_EXPERT

    def fetch(group):
        return [pltpu.make_async_copy(w.at[layer, group], stage, sems.at[k])
                for k, (w, stage) in enumerate(((wg_hbm, wg_f), (wu_hbm, wu_f), (wd_hbm, wd_f)))]

    @pl.when(i < na_ref[0])
    def _():
        @pl.when(new_group)
        def _():
            @pl.when(i == 0)
            def _():
                for cp in fetch(g):
                    cp.start()

            for cp in fetch(g):
                cp.wait()
            for e in range(EXPERTS_PER_GROUP):
                cols = slice(e * D_EXPERT, (e + 1) * D_EXPERT)
                wgu_s[:, cols] = wg_f[e].astype(BF16)
                wgu_s[:, width + e * D_EXPERT:width + (e + 1) * D_EXPERT] = wu_f[e].astype(BF16)
                wd_s[cols, :] = wd_f[e].astype(BF16)
            nxt = ng_ref[i]

            @pl.when(nxt >= 0)
            def _():
                for cp in fetch(nxt):
                    cp.start()

        x = _from_token_tiles(xs_ref, MOE_TILE)
        i1, w1, i2, w2 = _route_experts(_router_logits(x, wr_ref, br_ref), g)
        xb = x.astype(BF16)
        first = ROUTER_EXPERT_LANE0 + g * EXPERTS_PER_GROUP
        hid = []
        for e in range(EXPERTS_PER_GROUP):
            a = _dot(xb, wgu_s[:, e * D_EXPERT:(e + 1) * D_EXPERT])
            u = _dot(xb, wgu_s[:, width + e * D_EXPERT:width + (e + 1) * D_EXPERT])
            gate = jnp.where(i1 == first + e, w1, 0.0) + jnp.where(i2 == first + e, w2, 0.0)
            hid.append(((a * _sigmoid(a)) * u * gate).astype(BF16))
        _to_token_tiles(ys_ref, _dot(jnp.concatenate(hid, axis=1), wd_s[...]))


def _experts(layer, tile_group, n_active, next_group, xs, wr, br, w_gate, w_up, w_down):
    ntiles = tile_group.shape[0]
    last = lambda i, na: jnp.minimum(i, na[0] - 1)
    tile = pl.BlockSpec((MOE_TILE * TOKEN_ROWS, LANES), lambda i, tg, na, ng: (last(i, na), 0))
    fixed = lambda a: pl.BlockSpec(a.shape, lambda i, tg, na, ng: (0, 0))
    grouped = lambda w: w.reshape((w.shape[0], N_EXPERT_GROUPS, EXPERTS_PER_GROUP) + w.shape[2:])
    stage = lambda w: pltpu.VMEM((EXPERTS_PER_GROUP,) + w.shape[2:], F32)
    hbm = pl.BlockSpec(memory_space=pl.ANY)
    width = EXPERTS_PER_GROUP * D_EXPERT
    return pl.pallas_call(
        functools.partial(_expert_kernel, layer),
        grid_spec=pltpu.PrefetchScalarGridSpec(
            num_scalar_prefetch=3,
            grid=(ntiles,),
            in_specs=[tile, fixed(wr), fixed(br), hbm, hbm, hbm],
            out_specs=tile,
            scratch_shapes=[stage(w_gate), stage(w_up), stage(w_down),
                            pltpu.VMEM((D_MODEL, 2 * width), BF16), pltpu.VMEM((width, D_MODEL), BF16),
                            pltpu.SemaphoreType.DMA((3,))]),
        out_shape=jax.ShapeDtypeStruct(xs.shape, F32),
        input_output_aliases={3: 0},
        compiler_params=_params("arbitrary"),
        name="moe_experts",
    )(tile_group, n_active, next_group, xs, wr, br, grouped(w_gate), grouped(w_up), grouped(w_down))


def _combine_kernel(emit_h, pos_ref, nxt_ref, h_ref, gn_ref, ys_ref, *refs):
    outs, (buf0, buf1, sem0, sem1) = refs[:-4], refs[-4:]
    bufs, sems = (buf0, buf1), (sem0, sem1)
    rows = h_ref.shape[0]
    i = pl.program_id(0)

    def gather(idx_ref, slot):
        def copy(r):
            return pltpu.make_async_copy(_token_tile(ys_ref, idx_ref[0, 0, r]), _token_tile(bufs[slot], r),
                                         sems[slot])

        def issue(r, c):
            copy(2 * r).start(priority=0)
            copy(2 * r + 1).start(priority=1)
            return c

        lax.fori_loop(0, rows // 2, issue, 0, unroll=4)

    @pl.when(i == 0)
    def _():
        gather(pos_ref, 0)

    for slot in range(2):
        @pl.when(i % 2 == slot)
        def _():
            @pl.when(i + 1 < pl.num_programs(0))
            def _():
                gather(nxt_ref, 1 - slot)

            pltpu.make_async_copy(ys_ref.at[pl.ds(0, rows * TOKEN_ROWS), :], bufs[slot], sems[slot]).wait()
            h = h_ref[...] + _from_token_tiles(bufs[slot], rows)
            if emit_h:
                outs[0][...] = h
            outs[-1][...] = _rms(h, gn_ref[...])


def _combine(pos, h, gnext, ys, emit_h, tm=512):
    t = h.shape[0]
    nblk = t // tm
    row = lambda i: (i, 0)
    blocks = _pos_blocks(pos, tm)
    smem = lambda imap: pl.BlockSpec((1, 1, tm), imap, memory_space=pltpu.SMEM)
    n_out = 2 if emit_h else 1
    out = pl.pallas_call(
        functools.partial(_combine_kernel, emit_h),
        grid=(nblk,),
        in_specs=[smem(lambda i: (i, 0, 0)), smem(lambda i: (jnp.minimum(i + 1, nblk - 1), 0, 0)),
                  pl.BlockSpec((tm, D_MODEL), row),
                  pl.BlockSpec((1, D_MODEL), lambda i: (0, 0)),
                  pl.BlockSpec(memory_space=pl.ANY)],
        out_specs=[pl.BlockSpec((tm, D_MODEL), row)] * n_out,
        out_shape=[jax.ShapeDtypeStruct((t, D_MODEL), F32)] * n_out,
        scratch_shapes=[pltpu.VMEM((tm * TOKEN_ROWS, LANES), F32), pltpu.VMEM((tm * TOKEN_ROWS, LANES), F32),
                        pltpu.SemaphoreType.DMA(()), pltpu.SemaphoreType.DMA(())],
        compiler_params=_params("arbitrary"),
        name="moe_combine",
    )(blocks, blocks, h, gnext, ys)
    return (out[0], out[1]) if emit_h else (None, out[0])


def _moe(layer, h, hn_tt, route, counts, wr, br, w_gate, w_up, w_down, gnext, emit_h):
    t = h.shape[0]
    pos, tile_group, n_active, next_group, plan = _dispatch_plan(route, counts, t)
    xs = _dispatch(plan, pos, hn_tt, t)
    ys = _experts(layer, tile_group, n_active, next_group, xs, wr, br, w_gate, w_up, w_down)
    return _combine(pos, h, gnext, ys, emit_h)


def _dot3(a, b):
    a_hi = a.astype(BF16)
    a_lo = (a - a_hi.astype(F32)).astype(BF16)
    b_hi = b.astype(BF16)
    b_lo = (b - b_hi.astype(F32)).astype(BF16)
    return _dot(a_hi, b_hi) + (_dot(a_hi, b_lo) + _dot(a_lo, b_hi))


def _ssm_powers(lam_re, lam_im, log_step, lags):
    lam_re = jnp.minimum(lam_re, np.float32(-1e-4))
    step = jnp.exp(log_step)
    dec = jnp.exp(lags * (lam_re * step))
    ang = lags * (lam_im * step)
    return dec * jnp.cos(ang), dec * jnp.sin(ang)


def _ssm_operators(lamr_ref, lamc_ref, bre_ref, bim_ref, cre_ref, cim_ref, r_s, wd_s, vd_s):
    L, ns = SSM_CHUNK, SSM_BLOCK_STATE
    lam = lamr_ref[0]
    lre, lim, lst = lam[0:1], lam[1:2], lam[2:3]
    lag_rev = (L - 1 - lax.broadcasted_iota(jnp.int32, (L, 1), 0)).astype(F32)
    pr_re, pr_im = _ssm_powers(lre, lim, lst, lag_rev)
    ab_re, ab_im = _ssm_powers(lre, lim, lst, np.float32(1.0))
    lam_re = jnp.minimum(lre, np.float32(-1e-4))
    denom = lam_re * lam_re + lim * lim
    num_re = ab_re - 1.0
    f_re = (num_re * lam_re + ab_im * lim) / denom
    f_im = (ab_im * lam_re - num_re * lim) / denom
    bre, bim = bre_ref[0], bim_ref[0]
    bb_re = f_re * bre - f_im * bim
    bb_im = f_re * bim + f_im * bre
    cre, cim = cre_ref[0], cim_ref[0]
    for s in range(L):
        a_re, a_im = pr_re[s:s + 1], pr_im[s:s + 1]
        w_re = a_re * bb_re - a_im * bb_im
        w_im = a_re * bb_im + a_im * bb_re
        rows = slice(s * LANES, (s + 1) * LANES)
        wd_s[rows, 0:ns] = w_re.astype(BF16)
        wd_s[rows, ns:2 * ns] = w_im.astype(BF16)
        r_s[rows, :] = (_dot3(w_re, cre) - _dot3(w_im, cim)).astype(BF16)
    r_s[L * LANES:, :] = jnp.zeros((LANES, LANES), BF16)
    lamc = lamc_ref[0]
    lag1 = (1 + lax.broadcasted_iota(jnp.int32, (1, L), 1)).astype(F32)
    p1_re, p1_im = _ssm_powers(lamc[:, 0:1], lamc[:, 1:2], lamc[:, 2:3], lag1)
    for t in range(L):
        a_re, a_im = p1_re[:, t:t + 1], p1_im[:, t:t + 1]
        cols = slice(t * LANES, (t + 1) * LANES)
        vd_s[0:ns, cols] = (cre * a_re - cim * a_im).astype(BF16)
        vd_s[ns:2 * ns, cols] = (-(cre * a_im + cim * a_re)).astype(BF16)
    return _ssm_powers(lre, lim, lst, np.float32(L))


def _ssm_kernel(nch, x_ref, lamr_ref, lamc_ref, bre_ref, bim_ref, cre_ref, cim_ref, d_ref, o_ref,
                r_s, wd_s, vd_s, xcat, st, xprev):
    nb = x_ref.shape[0]
    ns = SSM_BLOCK_STATE
    ar, ai = _ssm_operators(lamr_ref, lamc_ref, bre_ref, bim_ref, cre_ref, cim_ref, r_s, wd_s, vd_s)
    for s in range(SSM_CHUNK):
        for b in range(nb):
            xs = x_ref[b, pl.ds(s, nch, stride=SSM_CHUNK), :]
            xcat[b * nch:(b + 1) * nch, s * LANES:(s + 1) * LANES] = xs.astype(BF16)
    st[...] = _dot(xcat[...], wd_s[...])

    def step(k, carry):
        out = []
        for b in range(nb):
            xr, xi = carry[b]
            row = b * nch + k
            sr = st[pl.ds(row, 1), 0:ns]
            si = st[pl.ds(row, 1), ns:2 * ns]
            st[pl.ds(row, 1), 0:ns] = xr
            st[pl.ds(row, 1), ns:2 * ns] = xi
            out.append((ar * xr - ai * xi + sr, ar * xi + ai * xr + si))
        return tuple(out)

    zero = jnp.zeros((1, ns), F32)
    lax.fori_loop(0, nch, step, tuple((zero, zero) for _ in range(nb)))
    xprev[...] = st[...].astype(BF16)
    dskip = d_ref[...]
    L = SSM_CHUNK
    for t in range(0, L, 2):
        taps = jnp.concatenate([r_s[(L - 1 - t) * LANES:, :], r_s[(L - 2 - t) * LANES:L * LANES, :]], axis=1)
        y2 = (_dot(xcat[:, :(t + 2) * LANES], taps)
              + _dot(xprev[...], vd_s[:, t * LANES:(t + 2) * LANES]))
        for k in range(2):
            y = y2[:, k * LANES:(k + 1) * LANES]
            for b in range(nb):
                u = x_ref[b, pl.ds(t + k, nch, stride=L), :]
                o_ref[b, pl.ds(t + k, nch, stride=L), :] = _gelu_tanh(y[b * nch:(b + 1) * nch, :] + dskip * u)


def _ssm(x3, params, dskip):
    nb, seq, _ = x3.shape
    nch = seq // SSM_CHUNK
    rows = nb * nch
    ns = SSM_BLOCK_STATE
    blk = pl.BlockSpec((nb, seq, LANES), lambda g: (0, 0, g))
    per = lambda shape: pl.BlockSpec((1,) + shape, lambda g: (g, 0, 0))
    return pl.pallas_call(
        functools.partial(_ssm_kernel, nch),
        grid=(SSM_BLOCKS,),
        in_specs=[blk, per((3, ns)), per((ns, 3)), per((LANES, ns)), per((LANES, ns)),
                  per((ns, LANES)), per((ns, LANES)), pl.BlockSpec((1, LANES), lambda g: (0, g))],
        out_specs=blk,
        out_shape=jax.ShapeDtypeStruct(x3.shape, F32),
        scratch_shapes=[pltpu.VMEM(((SSM_CHUNK + 1) * LANES, LANES), BF16),
                        pltpu.VMEM((SSM_CHUNK * LANES, 2 * ns), BF16),
                        pltpu.VMEM((2 * ns, SSM_CHUNK * LANES), BF16),
                        pltpu.VMEM((rows, SSM_CHUNK * LANES), BF16),
                        pltpu.VMEM((rows, 2 * ns), F32),
                        pltpu.VMEM((rows, 2 * ns), BF16)],
        compiler_params=_params("parallel"),
        name="ssm",
    )(x3, *params, dskip)


def _ssm_params(lambda_re, lambda_im, log_step, b_re, b_im, c_re, c_im):
    nb, ng, ns = SSM_BLOCKS, SSM_LANE_GROUPS, SSM_BLOCK_STATE
    lam = jnp.stack([lambda_re.astype(F32), lambda_im.astype(F32),
                     jnp.broadcast_to(log_step.astype(F32)[:, None], lambda_re.shape)])
    lam_rows = jnp.transpose(lam.reshape(3, nb, ns), (1, 0, 2))
    lam_cols = jnp.transpose(lam_rows, (0, 2, 1))
    eye = jnp.eye(ng, dtype=F32)

    def blockdiag(m):
        a, b = m.shape[-2:]
        out = jnp.einsum('ngab,gh->ngahb', m.astype(F32).reshape(nb, ng, a, b), eye)
        return out.reshape(nb, ng * a, ng * b)

    bt = lambda m: blockdiag(jnp.swapaxes(m, -1, -2))
    return lam_rows, lam_cols, bt(b_re), bt(b_im), bt(c_re), bt(c_im)


def _router_tables(w_group, b_group, w_expert, b_expert):
    w = jnp.concatenate([w_group.astype(F32),
                         jnp.transpose(w_expert.astype(F32), (1, 0, 2)).reshape(D_MODEL, N_EXPERTS)], axis=1)
    b = jnp.concatenate([b_group.astype(F32), b_expert.astype(F32).reshape(N_EXPERTS)])
    pad = ROUTER_LANES - w.shape[1]
    return jnp.pad(w, ((0, 0), (0, pad))), jnp.pad(b, (0, pad)).reshape(1, ROUTER_LANES)


def kernel(x, norm_mix, norm_ffn, norm_final, attn_w_qkv, attn_w_o, attn_rel_bias, ssm_lambda_re, ssm_lambda_im, ssm_log_step, ssm_b_re, ssm_b_im, ssm_c_re, ssm_c_im, ssm_d, ssm_w_out, moe_w_group_router, moe_b_group_router, moe_w_expert_router, moe_b_expert_router, moe_w_gate, moe_w_up, moe_w_down):
    batch, seq, d = x.shape
    t = batch * seq
    row = lambda v: v.astype(F32).reshape(1, d)
    h = x.astype(F32).reshape(t, d)

    qkv = _qkv(h, row(norm_mix[0]), attn_w_qkv[0].astype(F32))
    o = _attention(qkv, _rel_bias_rows(attn_rel_bias[0]), batch, seq)
    wr, br = _router_tables(moe_w_group_router[0], moe_b_group_router[0],
                            moe_w_expert_router[0], moe_b_expert_router[0])
    h, hn_tt, route, counts = _post(o, h, attn_w_o[0].astype(F32), row(norm_ffn[0]), wr, br, glu=False)
    h, hn32 = _moe(0, h, hn_tt, route, counts, wr, br, moe_w_gate, moe_w_up, moe_w_down, row(norm_mix[1]),
                   emit_h=True)

    ssm_params = _ssm_params(ssm_lambda_re[0], ssm_lambda_im[0], ssm_log_step[0],
                             ssm_b_re[0], ssm_b_im[0], ssm_c_re[0], ssm_c_im[0])
    y = _ssm(hn32.reshape(batch, seq, d), ssm_params, row(ssm_d[0]))
    wr, br = _router_tables(moe_w_group_router[1], moe_b_group_router[1],
                            moe_w_expert_router[1], moe_b_expert_router[1])
    h, hn_tt, route, counts = _post(y.reshape(t, d), h, ssm_w_out[0].astype(F32), row(norm_ffn[1]), wr, br, glu=True)
    _, out = _moe(1, h, hn_tt, route, counts, wr, br, moe_w_gate, moe_w_up, moe_w_down, row(norm_final),
                 emit_h=False)
    return out.reshape(batch, seq, d).astype(x.dtype)
```

```python
import functools

import jax
import jax.numpy as jnp
import numpy as np
from jax import lax
from jax.experimental import pallas as pl
from jax.experimental.pallas import tpu as pltpu

F32 = jnp.float32
BF16 = jnp.bfloat16

D_MODEL = 1024
CHUNK = 64
LOOKBACK_CHUNKS = 8
BAND = (LOOKBACK_CHUNKS + 1) * CHUNK
N_HEADS = 16
HEAD_DIM = D_MODEL // N_HEADS
MAX_REL = 128
SSM_GROUP = 16
SSM_GROUPS = D_MODEL // SSM_GROUP
SSM_STATE = 64
N_EXPERT_GROUPS = 4
EXPERTS_PER_GROUP = 8
N_EXPERTS = N_EXPERT_GROUPS * EXPERTS_PER_GROUP
D_EXPERT = D_MODEL // 4
RMS_EPS = 1e-6
NEG_BIG = -1e30

LANES = 128
SUBLANES = 8
MXU_DIM = 256
VMEM_LIMIT = 60 * 1024 * 1024

HEADS_PER_GROUP = MXU_DIM // HEAD_DIM
N_HEAD_GROUPS = N_HEADS // HEADS_PER_GROUP
ATTN_QBLOCK = LOOKBACK_CHUNKS * CHUNK
ATTN_QROWS = 4 * ATTN_QBLOCK
REL_PAD = -(-(2 * MAX_REL + 1) // LANES) * LANES
BIAS_WIDTH = -(-(BAND + CHUNK - 1) // LANES) * LANES
ROUTER_LANES = LANES
ROUTER_EXPERT_LANE0 = N_EXPERT_GROUPS
SSM_CHUNK = 16
SSM_LANE_GROUPS = LANES // SSM_GROUP
SSM_BLOCKS = D_MODEL // LANES
SSM_BLOCK_STATE = SSM_LANE_GROUPS * SSM_STATE
TOKEN_ROWS = D_MODEL // LANES
MOE_TILE = 256
R_GROUP, R_RANK = range(2)
QKV_ROWS = 512
POST_ROWS = 512
DISPATCH_ROWS = 1024
COMBINE_ROWS = 512
DMA_ISSUE_UNROLL = 4


def _dot(a, b):
    return jnp.dot(a, b, preferred_element_type=F32)


def _rms(x, g):
    return x * lax.rsqrt(jnp.mean(x * x, axis=-1, keepdims=True) + RMS_EPS) * g


def _sigmoid(x):
    return 1.0 / (1.0 + jnp.exp(-x))


def _gelu_tanh(x):
    c = np.float32(np.sqrt(2.0 / np.pi))
    return 0.5 * x * (1.0 + jnp.tanh(c * (x + np.float32(0.044715) * (x * x * x))))


def _params(*sem):
    return pltpu.CompilerParams(dimension_semantics=sem, vmem_limit_bytes=VMEM_LIMIT)


def _resident(shape):
    return pl.BlockSpec(shape, lambda *_: (0,) * len(shape), pipeline_mode=pl.Buffered(1))


def _cast_once(w_ref, wb):
    @pl.when(pl.program_id(0) == 0)
    def _():
        wb[...] = w_ref[...].astype(BF16)


def _qkv_kernel(x_ref, g_ref, w_ref, o_ref, wb):
    _cast_once(w_ref, wb)
    xn = _rms(x_ref[...], g_ref[...]).astype(BF16)
    for c in range(3):
        acc = _dot(xn, wb[:, c * D_MODEL:(c + 1) * D_MODEL])
        if c == 0:
            acc = acc * np.float32(HEAD_DIM ** -0.5)
        o_ref[:, c * D_MODEL:(c + 1) * D_MODEL] = acc.astype(BF16)


def _qkv(x, g, w, tm=QKV_ROWS):
    t = x.shape[0]
    return pl.pallas_call(
        _qkv_kernel,
        grid=(t // tm,),
        in_specs=[pl.BlockSpec((tm, D_MODEL), lambda i: (i, 0)),
                  pl.BlockSpec((1, D_MODEL), lambda i: (0, 0)),
                  _resident(w.shape)],
        out_specs=pl.BlockSpec((tm, 3 * D_MODEL), lambda i: (i, 0)),
        out_shape=jax.ShapeDtypeStruct((t, 3 * D_MODEL), BF16),
        scratch_shapes=[pltpu.VMEM(w.shape, BF16)],
        compiler_params=_params("arbitrary"),
        name="qkv",
    )(x, g, w)


def _expand_rel_bias(rel_ref, bias):
    rel = rel_ref[0]
    r1 = rel.astype(BF16)
    r2 = (rel - r1.astype(F32)).astype(BF16)
    r3 = (rel - r1.astype(F32) - r2.astype(F32)).astype(BF16)
    j = lax.broadcasted_iota(jnp.int32, (REL_PAD, BIAS_WIDTH), 1)
    src = jnp.clip(BAND - 1 - j, -MAX_REL, MAX_REL) + MAX_REL
    sel = jnp.where(lax.broadcasted_iota(jnp.int32, (REL_PAD, BIAS_WIDTH), 0) == src, 1.0, 0.0).astype(BF16)
    u = (_dot(r1, sel) + _dot(r2, sel)) + _dot(r3, sel)
    for h in range(HEADS_PER_GROUP):
        rows = jnp.broadcast_to(u[h:h + 1, :], (CHUNK, BIAS_WIDTH))
        rows = pltpu.roll(rows, BIAS_WIDTH - (CHUNK - 1), 1, stride=1, stride_axis=0)
        bias[h * CHUNK:(h + 1) * CHUNK, :] = rows[:, :BAND]


def _attn_kernel(q_ref, kp_ref, kc_ref, vp_ref, vc_ref, rel_ref, o_ref, kk, vv, bias):
    qb = pl.program_id(2)

    @pl.when(qb == 0)
    def _():
        _expand_rel_bias(rel_ref, bias)

    kk[0:ATTN_QBLOCK, :] = kp_ref[...]
    kk[ATTN_QBLOCK:, :] = kc_ref[...]
    vv[0:ATTN_QBLOCK, :] = vp_ref[...]
    vv[ATTN_QBLOCK:, :] = vc_ref[...]
    lane_head = lax.broadcasted_iota(jnp.int32, (CHUNK, MXU_DIM), 1) // HEAD_DIM
    col = lax.broadcasted_iota(jnp.int32, (1, BAND), 1)

    def chunks(first_block):
        for j in range(ATTN_QROWS // CHUNK):
            qj = q_ref[j * CHUNK:(j + 1) * CHUNK, :]
            lhs = jnp.concatenate(
                [jnp.where(lane_head == h, qj, jnp.zeros_like(qj)) for h in range(HEADS_PER_GROUP)], axis=0)
            kwin = kk[j * CHUNK:j * CHUNK + BAND, :]
            vwin = vv[j * CHUNK:j * CHUNK + BAND, :]
            s = lax.dot_general(lhs, kwin, (((1,), (1,)), ((), ())), preferred_element_type=F32) + bias[...]
            if first_block:
                s = jnp.where(col + j * CHUNK >= ATTN_QBLOCK, s, NEG_BIG)
            m = jnp.max(s, axis=-1, keepdims=True)
            p = jnp.exp(s - m)
            l = jnp.sum(p, axis=-1, keepdims=True)
            o_all = _dot(p.astype(BF16), vwin) * (1.0 / l)
            o = jnp.zeros((CHUNK, MXU_DIM), F32)
            for h in range(HEADS_PER_GROUP):
                o = o + jnp.where(lane_head == h, o_all[h * CHUNK:(h + 1) * CHUNK, :], 0.0)
            o_ref[j * CHUNK:(j + 1) * CHUNK, :] = o.astype(BF16)

    @pl.when(qb == 0)
    def _():
        chunks(True)

    @pl.when(qb > 0)
    def _():
        chunks(False)


def _attention(qkv, rel, batch, seq):
    nqb = seq // ATTN_QROWS
    per = ATTN_QROWS // ATTN_QBLOCK
    kcol = D_MODEL // MXU_DIM
    blk = (ATTN_QROWS, MXU_DIM)
    back = (ATTN_QBLOCK, MXU_DIM)
    cur = lambda off: (lambda b, g, i: (b * nqb + i, off + g))
    prev = lambda off: (lambda b, g, i: ((b * nqb + i) * per - jnp.minimum(i, 1), off + g))
    return pl.pallas_call(
        _attn_kernel,
        grid=(batch, N_HEAD_GROUPS, nqb),
        in_specs=[pl.BlockSpec(blk, cur(0)),
                  pl.BlockSpec(back, prev(kcol)), pl.BlockSpec(blk, cur(kcol)),
                  pl.BlockSpec(back, prev(2 * kcol)), pl.BlockSpec(blk, cur(2 * kcol)),
                  pl.BlockSpec((1, SUBLANES, REL_PAD), lambda b, g, i: (g, 0, 0))],
        out_specs=pl.BlockSpec(blk, cur(0)),
        out_shape=jax.ShapeDtypeStruct((batch * seq, D_MODEL), BF16),
        scratch_shapes=[pltpu.VMEM((ATTN_QBLOCK + ATTN_QROWS, MXU_DIM), BF16),
                        pltpu.VMEM((ATTN_QBLOCK + ATTN_QROWS, MXU_DIM), BF16),
                        pltpu.VMEM((HEADS_PER_GROUP * CHUNK, BAND), F32)],
        compiler_params=_params("parallel", "parallel", "arbitrary"),
        name="attn",
    )(qkv, qkv, qkv, qkv, qkv, rel)


def _rel_bias_rows(rel_bias):
    rel = rel_bias.astype(F32).reshape(N_HEAD_GROUPS, HEADS_PER_GROUP, 2 * MAX_REL + 1)
    return jnp.pad(rel, ((0, 0), (0, SUBLANES - HEADS_PER_GROUP), (0, REL_PAD - (2 * MAX_REL + 1))))


def _router_logits(hn, wr_ref, br_ref, one_matmul=False):
    a_hi = hn.astype(BF16)
    a_lo = (hn - a_hi.astype(F32)).astype(BF16)
    w = wr_ref[...]
    w_hi = w.astype(BF16)
    w_lo = (w - w_hi.astype(F32)).astype(BF16)
    if not one_matmul:
        return _dot(a_hi, w_hi) + (_dot(a_hi, w_lo) + _dot(a_lo, w_hi)) + br_ref[...]
    w2 = jnp.concatenate([jnp.concatenate([w_hi, w_lo], axis=1),
                          jnp.concatenate([w_hi, jnp.zeros_like(w_lo)], axis=1)], axis=0)
    out = _dot(jnp.concatenate([a_hi, a_lo], axis=1), w2)
    return out[:, :ROUTER_LANES] + out[:, ROUTER_LANES:] + br_ref[...]


def _route_group(lg, base):
    rows = lg.shape[0]
    lane = lax.broadcasted_iota(jnp.int32, lg.shape, 1)
    gl = jnp.where(lane < N_EXPERT_GROUPS, lg, np.float32(-np.inf))
    gmax = jnp.max(gl, axis=-1, keepdims=True)
    gidx = jnp.min(jnp.where(gl == gmax, lane, np.int32(ROUTER_LANES)), axis=-1, keepdims=True)
    og = lane == gidx
    tri = (lax.broadcasted_iota(jnp.int32, (rows, rows), 0)
           > lax.broadcasted_iota(jnp.int32, (rows, rows), 1))
    tri = jnp.where(tri, 1.0, 0.0).astype(BF16)
    ogf = jnp.where(og, 1.0, 0.0)
    rank = jnp.sum(jnp.where(og, base + _dot(tri, ogf.astype(BF16)), 0.0), axis=-1, keepdims=True)
    rec = jnp.where(lane == R_GROUP, gidx.astype(F32), jnp.where(lane == R_RANK, rank, 0.0))
    return rec, base + jnp.sum(ogf, axis=0, keepdims=True)


def _route_experts(lg, g):
    lane = lax.broadcasted_iota(jnp.int32, lg.shape, 1)
    ninf = np.float32(-np.inf)
    big = np.int32(ROUTER_LANES)
    gmask = lane < N_EXPERT_GROUPS
    gmax = jnp.max(jnp.where(gmask, lg, ninf), axis=-1, keepdims=True)
    ge = jnp.where(gmask, jnp.exp(lg - gmax), 0.0)
    gprob = jnp.sum(jnp.where(lane == g, ge, 0.0), axis=-1, keepdims=True) / jnp.sum(ge, axis=-1, keepdims=True)
    lo = ROUTER_EXPERT_LANE0 + EXPERTS_PER_GROUP * g
    el = jnp.where((lane >= lo) & (lane < lo + EXPERTS_PER_GROUP), lg, ninf)
    l1 = jnp.max(el, axis=-1, keepdims=True)
    i1 = jnp.min(jnp.where(el == l1, lane, big), axis=-1, keepdims=True)
    el2 = jnp.where(lane == i1, ninf, el)
    l2 = jnp.max(el2, axis=-1, keepdims=True)
    i2 = jnp.min(jnp.where(el2 == l2, lane, big), axis=-1, keepdims=True)
    t = jnp.exp(l2 - l1)
    w1 = gprob / (1.0 + t)
    return i1, w1, i2, w1 * t


def _to_token_tiles(ref, x):
    rows = x.shape[0]
    for j in range(TOKEN_ROWS):
        ref[pl.ds(j, rows, stride=TOKEN_ROWS), :] = x[:, j * LANES:(j + 1) * LANES]


def _from_token_tiles(ref, rows):
    return jnp.concatenate([ref[pl.ds(j, rows, stride=TOKEN_ROWS), :] for j in range(TOKEN_ROWS)], axis=1)


def _post_kernel(glu, y_ref, h_ref, w_ref, g_ref, wr_ref, br_ref, hout_ref, hntt_ref, route_ref, cnt_ref, wb):
    @pl.when(pl.program_id(0) == 0)
    def _():
        cnt_ref[...] = jnp.zeros_like(cnt_ref)

    _cast_once(w_ref, wb)
    y = y_ref[...].astype(BF16)
    if glu:
        mix = _dot(y, wb[:, :D_MODEL]) * _sigmoid(_dot(y, wb[:, D_MODEL:]))
    else:
        mix = _dot(y, wb[...])
    h = h_ref[...] + mix
    hout_ref[...] = h
    hn = _rms(h, g_ref[...])
    _to_token_tiles(hntt_ref, hn)
    route_ref[...], cnt_ref[...] = _route_group(_router_logits(hn, wr_ref, br_ref), cnt_ref[...])


def _post(y, h, w, g, wr, br, glu, tm=POST_ROWS):
    t = h.shape[0]
    row = lambda i: (i, 0)
    fixed = lambda i: (0, 0)
    return pl.pallas_call(
        functools.partial(_post_kernel, glu),
        grid=(t // tm,),
        in_specs=[pl.BlockSpec((tm, D_MODEL), row), pl.BlockSpec((tm, D_MODEL), row),
                  _resident(w.shape), pl.BlockSpec((1, D_MODEL), fixed),
                  pl.BlockSpec((D_MODEL, ROUTER_LANES), fixed), pl.BlockSpec((1, ROUTER_LANES), fixed)],
        out_specs=[pl.BlockSpec((tm, D_MODEL), row), pl.BlockSpec((tm * TOKEN_ROWS, LANES), row),
                   pl.BlockSpec((tm, ROUTER_LANES), row), pl.BlockSpec((1, ROUTER_LANES), fixed)],
        out_shape=[jax.ShapeDtypeStruct((t, D_MODEL), F32), jax.ShapeDtypeStruct((t * TOKEN_ROWS, LANES), F32),
                   jax.ShapeDtypeStruct((t, ROUTER_LANES), F32), jax.ShapeDtypeStruct((1, ROUTER_LANES), F32)],
        scratch_shapes=[pltpu.VMEM(w.shape, BF16)],
        compiler_params=_params("arbitrary"),
        name="post_glu" if glu else "post_attn",
    )(y, h, w, g, wr, br)


def _moe_tiles(t):
    return t // MOE_TILE + N_EXPERT_GROUPS


def _dispatch_plan(route, counts, t):
    g = route[:, R_GROUP].astype(jnp.int32)
    rank = route[:, R_RANK].astype(jnp.int32)
    cnt = counts[0, :N_EXPERT_GROUPS].astype(jnp.int32)
    padded = ((cnt + (MOE_TILE - 1)) // MOE_TILE) * MOE_TILE
    ends = jnp.cumsum(padded)
    off = ends - padded
    onehot = g[:, None] == jnp.arange(N_EXPERT_GROUPS, dtype=jnp.int32)
    pos = jnp.sum(jnp.where(onehot, off, 0), axis=-1) + rank
    ntiles = _moe_tiles(t)
    tile_start = jnp.arange(ntiles, dtype=jnp.int32) * MOE_TILE
    tile_group = jnp.minimum(jnp.sum((tile_start[:, None] >= ends[None, :]).astype(jnp.int32), axis=1),
                             N_EXPERT_GROUPS - 1)
    n_active = (ends[-1] // MOE_TILE).reshape(1)
    plan = jnp.concatenate([off + cnt, padded - cnt, n_active])
    groups = jnp.arange(N_EXPERT_GROUPS, dtype=jnp.int32)
    later = (groups[None, :] > groups[:, None]) & (cnt[None, :] > 0)
    following = jnp.min(jnp.where(later, groups[None, :], N_EXPERT_GROUPS), axis=1)
    following = jnp.where(following == N_EXPERT_GROUPS, -1, following)
    next_group = jnp.sum(jnp.where(tile_group[:, None] == groups[None, :], following[None, :], 0), axis=1)
    return pos, tile_group, n_active, next_group, plan


def _token_tile(ref, i):
    return ref.at[pl.ds(pl.multiple_of(i * TOKEN_ROWS, TOKEN_ROWS), TOKEN_ROWS), :]


def _zero_fill(plan_ref, xs_ref, zbuf, zsem, ntiles):
    nseg = N_EXPERT_GROUPS
    zbuf[...] = jnp.zeros_like(zbuf)
    half = MOE_TILE // 2
    sizes = [1 << b for b in reversed(range(half.bit_length()))]

    def chunk(first, size):
        return pltpu.make_async_copy(zbuf.at[pl.ds(0, size * TOKEN_ROWS), :],
                                     xs_ref.at[pl.ds(pl.multiple_of(first * TOKEN_ROWS, TOKEN_ROWS),
                                                     size * TOKEN_ROWS), :], zsem)

    def pads(wait):
        def body(e, c):
            first = plan_ref[e]
            n = plan_ref[nseg + e]
            for size in sizes:
                hit = (n & size) != 0

                @pl.when(hit)
                def _():
                    cp = chunk(first, size)
                    cp.wait() if wait else cp.start()

                first = first + jnp.where(hit, size, 0)
            return c
        lax.fori_loop(0, nseg, body, 0)

    def tail(wait):
        def body(i, c):
            for k in range(2):
                cp = chunk(i * MOE_TILE + k * half, half)
                cp.wait() if wait else cp.start()
            return c
        lax.fori_loop(plan_ref[2 * nseg], ntiles, body, 0)

    pads(False)
    tail(False)
    pads(True)
    tail(True)


def _dispatch_kernel(ntiles, plan_ref, pos_ref, hn_ref, xs_ref, sem, zbuf, zsem):
    rows = hn_ref.shape[0] // TOKEN_ROWS

    @pl.when(pl.program_id(0) == 0)
    def _():
        _zero_fill(plan_ref, xs_ref, zbuf, zsem, ntiles)

    def copy(r):
        return pltpu.make_async_copy(_token_tile(hn_ref, r), _token_tile(xs_ref, pos_ref[0, 0, r]), sem)

    def issue(r, c):
        copy(2 * r).start(priority=0)
        copy(2 * r + 1).start(priority=1)
        return c

    lax.fori_loop(0, rows // 2, issue, 0, unroll=DMA_ISSUE_UNROLL)
    pltpu.make_async_copy(hn_ref, xs_ref.at[pl.ds(0, rows * TOKEN_ROWS), :], sem).wait()


def _pos_blocks(pos, tm):
    return pos.reshape(pos.shape[0] // tm, 1, tm)


def _dispatch(plan, pos, hn_tt, t, tm=DISPATCH_ROWS):
    ntiles = _moe_tiles(t)
    nrows = ntiles * MOE_TILE * TOKEN_ROWS
    return pl.pallas_call(
        functools.partial(_dispatch_kernel, ntiles),
        grid_spec=pltpu.PrefetchScalarGridSpec(
            num_scalar_prefetch=1,
            grid=(t // tm,),
            in_specs=[pl.BlockSpec((1, 1, tm), lambda i, plan: (i, 0, 0), memory_space=pltpu.SMEM),
                      pl.BlockSpec((tm * TOKEN_ROWS, LANES), lambda i, plan: (i, 0))],
            out_specs=pl.BlockSpec(memory_space=pl.ANY),
            scratch_shapes=[pltpu.SemaphoreType.DMA(()),
                            pltpu.VMEM((MOE_TILE // 2 * TOKEN_ROWS, LANES), F32),
                            pltpu.SemaphoreType.DMA(())]),
        out_shape=jax.ShapeDtypeStruct((nrows, LANES), F32),
        compiler_params=_params("arbitrary"),
        name="moe_dispatch",
    )(plan, _pos_blocks(pos, tm), hn_tt)


def _expert_kernel(layer, tg_ref, na_ref, ng_ref, xs_ref, wr_ref, br_ref, wg_hbm, wu_hbm, wd_hbm, ys_ref,
                   wg_f, wu_f, wd_f, wgu_s, wd_s, sems):
    i = pl.program_id(0)
    g = tg_ref[i]
    new_group = (i == 0) | (tg_ref[jnp.maximum(i - 1, 0)] != g)
    width = EXPERTS_PER_GROUP * D_EXPERT

    def fetch(group):
        return [pltpu.make_async_copy(w.at[layer, group], stage, sems.at[k])
                for k, (w, stage) in enumerate(((wg_hbm, wg_f), (wu_hbm, wu_f), (wd_hbm, wd_f)))]

    @pl.when(i < na_ref[0])
    def _():
        @pl.when(new_group)
        def _():
            @pl.when(i == 0)
            def _():
                for cp in fetch(g):
                    cp.start()

            for cp in fetch(g):
                cp.wait()
            for e in range(EXPERTS_PER_GROUP):
                cols = slice(e * D_EXPERT, (e + 1) * D_EXPERT)
                wgu_s[:, cols] = wg_f[e].astype(BF16)
                wgu_s[:, width + e * D_EXPERT:width + (e + 1) * D_EXPERT] = wu_f[e].astype(BF16)
                wd_s[cols, :] = wd_f[e].astype(BF16)
            nxt = ng_ref[i]

            @pl.when(nxt >= 0)
            def _():
                for cp in fetch(nxt):
                    cp.start()

        x = _from_token_tiles(xs_ref, MOE_TILE)
        i1, w1, i2, w2 = _route_experts(_router_logits(x, wr_ref, br_ref, one_matmul=True), g)
        xb = x.astype(BF16)
        first = ROUTER_EXPERT_LANE0 + g * EXPERTS_PER_GROUP
        hid = []
        for e in range(EXPERTS_PER_GROUP):
            a = _dot(xb, wgu_s[:, e * D_EXPERT:(e + 1) * D_EXPERT])
            u = _dot(xb, wgu_s[:, width + e * D_EXPERT:width + (e + 1) * D_EXPERT])
            gate = jnp.where(i1 == first + e, w1, 0.0) + jnp.where(i2 == first + e, w2, 0.0)
            hid.append(((a * _sigmoid(a)) * u * gate).astype(BF16))
        _to_token_tiles(ys_ref, _dot(jnp.concatenate(hid, axis=1), wd_s[...]))


def _experts(layer, tile_group, n_active, next_group, xs, wr, br, w_gate, w_up, w_down):
    ntiles = tile_group.shape[0]
    last = lambda i, na: jnp.minimum(i, na[0] - 1)
    tile = pl.BlockSpec((MOE_TILE * TOKEN_ROWS, LANES), lambda i, tg, na, ng: (last(i, na), 0))
    fixed = lambda a: pl.BlockSpec(a.shape, lambda i, tg, na, ng: (0, 0))
    grouped = lambda w: w.reshape((w.shape[0], N_EXPERT_GROUPS, EXPERTS_PER_GROUP) + w.shape[2:])
    stage = lambda w: pltpu.VMEM((EXPERTS_PER_GROUP,) + w.shape[2:], F32)
    hbm = pl.BlockSpec(memory_space=pl.ANY)
    width = EXPERTS_PER_GROUP * D_EXPERT
    return pl.pallas_call(
        functools.partial(_expert_kernel, layer),
        grid_spec=pltpu.PrefetchScalarGridSpec(
            num_scalar_prefetch=3,
            grid=(ntiles,),
            in_specs=[tile, fixed(wr), fixed(br), hbm, hbm, hbm],
            out_specs=tile,
            scratch_shapes=[stage(w_gate), stage(w_up), stage(w_down),
                            pltpu.VMEM((D_MODEL, 2 * width), BF16), pltpu.VMEM((width, D_MODEL), BF16),
                            pltpu.SemaphoreType.DMA((3,))]),
        out_shape=jax.ShapeDtypeStruct(xs.shape, F32),
        input_output_aliases={3: 0},
        compiler_params=_params("arbitrary"),
        name="moe_experts",
    )(tile_group, n_active, next_group, xs, wr, br, grouped(w_gate), grouped(w_up), grouped(w_down))


def _combine_kernel(emit_h, pos_ref, nxt_ref, h_ref, gn_ref, ys_ref, *refs):
    outs, (buf0, buf1, sem0, sem1) = refs[:-4], refs[-4:]
    bufs, sems = (buf0, buf1), (sem0, sem1)
    rows = h_ref.shape[0]
    i = pl.program_id(0)

    def gather(idx_ref, slot):
        def copy(r):
            return pltpu.make_async_copy(_token_tile(ys_ref, idx_ref[0, 0, r]), _token_tile(bufs[slot], r),
                                         sems[slot])

        def issue(r, c):
            copy(2 * r).start(priority=0)
            copy(2 * r + 1).start(priority=1)
            return c

        lax.fori_loop(0, rows // 2, issue, 0, unroll=DMA_ISSUE_UNROLL)

    @pl.when(i == 0)
    def _():
        gather(pos_ref, 0)

    for slot in range(2):
        @pl.when(i % 2 == slot)
        def _():
            @pl.when(i + 1 < pl.num_programs(0))
            def _():
                gather(nxt_ref, 1 - slot)

            pltpu.make_async_copy(ys_ref.at[pl.ds(0, rows * TOKEN_ROWS), :], bufs[slot], sems[slot]).wait()
            h = h_ref[...] + _from_token_tiles(bufs[slot], rows)
            if emit_h:
                outs[0][...] = h
            outs[-1][...] = _rms(h, gn_ref[...])


def _combine(pos, h, gnext, ys, emit_h, tm=COMBINE_ROWS):
    t = h.shape[0]
    nblk = t // tm
    row = lambda i: (i, 0)
    blocks = _pos_blocks(pos, tm)
    smem = lambda imap: pl.BlockSpec((1, 1, tm), imap, memory_space=pltpu.SMEM)
    n_out = 2 if emit_h else 1
    out = pl.pallas_call(
        functools.partial(_combine_kernel, emit_h),
        grid=(nblk,),
        in_specs=[smem(lambda i: (i, 0, 0)), smem(lambda i: (jnp.minimum(i + 1, nblk - 1), 0, 0)),
                  pl.BlockSpec((tm, D_MODEL), row),
                  pl.BlockSpec((1, D_MODEL), lambda i: (0, 0)),
                  pl.BlockSpec(memory_space=pl.ANY)],
        out_specs=[pl.BlockSpec((tm, D_MODEL), row)] * n_out,
        out_shape=[jax.ShapeDtypeStruct((t, D_MODEL), F32)] * n_out,
        scratch_shapes=[pltpu.VMEM((tm * TOKEN_ROWS, LANES), F32), pltpu.VMEM((tm * TOKEN_ROWS, LANES), F32),
                        pltpu.SemaphoreType.DMA(()), pltpu.SemaphoreType.DMA(())],
        compiler_params=_params("arbitrary"),
        name="moe_combine",
    )(blocks, blocks, h, gnext, ys)
    return (out[0], out[1]) if emit_h else (None, out[0])


def _moe(layer, h, hn_tt, route, counts, wr, br, w_gate, w_up, w_down, gnext, emit_h):
    t = h.shape[0]
    pos, tile_group, n_active, next_group, plan = _dispatch_plan(route, counts, t)
    xs = _dispatch(plan, pos, hn_tt, t)
    ys = _experts(layer, tile_group, n_active, next_group, xs, wr, br, w_gate, w_up, w_down)
    return _combine(pos, h, gnext, ys, emit_h)


def _dot3(a, b):
    a_hi = a.astype(BF16)
    a_lo = (a - a_hi.astype(F32)).astype(BF16)
    b_hi = b.astype(BF16)
    b_lo = (b - b_hi.astype(F32)).astype(BF16)
    return _dot(a_hi, b_hi) + (_dot(a_hi, b_lo) + _dot(a_lo, b_hi))


def _ssm_powers(lam_re, lam_im, log_step, lags):
    lam_re = jnp.minimum(lam_re, np.float32(-1e-4))
    step = jnp.exp(log_step)
    dec = jnp.exp(lags * (lam_re * step))
    ang = lags * (lam_im * step)
    return dec * jnp.cos(ang), dec * jnp.sin(ang)


def _ssm_operators(lamr_ref, lamc_ref, bre_ref, bim_ref, cre_ref, cim_ref, r_s, wd_s, vd_s):
    L, ns = SSM_CHUNK, SSM_BLOCK_STATE
    lam = lamr_ref[0]
    lre, lim, lst = lam[0:1], lam[1:2], lam[2:3]
    lag_rev = (L - 1 - lax.broadcasted_iota(jnp.int32, (L, 1), 0)).astype(F32)
    pr_re, pr_im = _ssm_powers(lre, lim, lst, lag_rev)
    ab_re, ab_im = _ssm_powers(lre, lim, lst, np.float32(1.0))
    lam_re = jnp.minimum(lre, np.float32(-1e-4))
    denom = lam_re * lam_re + lim * lim
    num_re = ab_re - 1.0
    f_re = (num_re * lam_re + ab_im * lim) / denom
    f_im = (ab_im * lam_re - num_re * lim) / denom
    bre, bim = bre_ref[0], bim_ref[0]
    bb_re = f_re * bre - f_im * bim
    bb_im = f_re * bim + f_im * bre
    cre, cim = cre_ref[0], cim_ref[0]
    for s in range(L):
        a_re, a_im = pr_re[s:s + 1], pr_im[s:s + 1]
        w_re = a_re * bb_re - a_im * bb_im
        w_im = a_re * bb_im + a_im * bb_re
        rows = slice(s * LANES, (s + 1) * LANES)
        wd_s[rows, 0:ns] = w_re.astype(BF16)
        wd_s[rows, ns:2 * ns] = w_im.astype(BF16)
        r_s[rows, :] = (_dot3(w_re, cre) - _dot3(w_im, cim)).astype(BF16)
    r_s[L * LANES:, :] = jnp.zeros((LANES, LANES), BF16)
    lamc = lamc_ref[0]
    lag1 = (1 + lax.broadcasted_iota(jnp.int32, (1, L), 1)).astype(F32)
    p1_re, p1_im = _ssm_powers(lamc[:, 0:1], lamc[:, 1:2], lamc[:, 2:3], lag1)
    for t in range(L):
        a_re, a_im = p1_re[:, t:t + 1], p1_im[:, t:t + 1]
        cols = slice(t * LANES, (t + 1) * LANES)
        vd_s[0:ns, cols] = (cre * a_re - cim * a_im).astype(BF16)
        vd_s[ns:2 * ns, cols] = (-(cre * a_im + cim * a_re)).astype(BF16)
    return _ssm_powers(lre, lim, lst, np.float32(L))


def _ssm_kernel(nch, x_ref, lamr_ref, lamc_ref, bre_ref, bim_ref, cre_ref, cim_ref, d_ref, o_ref,
                r_s, wd_s, vd_s, xcat, st, xprev):
    nb = x_ref.shape[0]
    ns = SSM_BLOCK_STATE
    ar, ai = _ssm_operators(lamr_ref, lamc_ref, bre_ref, bim_ref, cre_ref, cim_ref, r_s, wd_s, vd_s)
    for s in range(SSM_CHUNK):
        for b in range(nb):
            xs = x_ref[b, pl.ds(s, nch, stride=SSM_CHUNK), :]
            xcat[b * nch:(b + 1) * nch, s * LANES:(s + 1) * LANES] = xs.astype(BF16)
    st[...] = _dot(xcat[...], wd_s[...])

    def step(k, carry):
        out = []
        for b in range(nb):
            xr, xi = carry[b]
            row = b * nch + k
            sr = st[pl.ds(row, 1), 0:ns]
            si = st[pl.ds(row, 1), ns:2 * ns]
            st[pl.ds(row, 1), 0:ns] = xr
            st[pl.ds(row, 1), ns:2 * ns] = xi
            out.append((ar * xr - ai * xi + sr, ar * xi + ai * xr + si))
        return tuple(out)

    zero = jnp.zeros((1, ns), F32)
    lax.fori_loop(0, nch, step, tuple((zero, zero) for _ in range(nb)))
    xprev[...] = st[...].astype(BF16)
    dskip = d_ref[...]
    L = SSM_CHUNK
    for t in range(0, L, 2):
        taps = jnp.concatenate([r_s[(L - 1 - t) * LANES:, :], r_s[(L - 2 - t) * LANES:L * LANES, :]], axis=1)
        y2 = (_dot(xcat[:, :(t + 2) * LANES], taps)
              + _dot(xprev[...], vd_s[:, t * LANES:(t + 2) * LANES]))
        for k in range(2):
            y = y2[:, k * LANES:(k + 1) * LANES]
            for b in range(nb):
                u = x_ref[b, pl.ds(t + k, nch, stride=L), :]
                o_ref[b, pl.ds(t + k, nch, stride=L), :] = _gelu_tanh(y[b * nch:(b + 1) * nch, :] + dskip * u)


def _ssm(x3, params, dskip):
    nb, seq, _ = x3.shape
    nch = seq // SSM_CHUNK
    rows = nb * nch
    ns = SSM_BLOCK_STATE
    blk = pl.BlockSpec((nb, seq, LANES), lambda g: (0, 0, g))
    per = lambda shape: pl.BlockSpec((1,) + shape, lambda g: (g, 0, 0))
    return pl.pallas_call(
        functools.partial(_ssm_kernel, nch),
        grid=(SSM_BLOCKS,),
        in_specs=[blk, per((3, ns)), per((ns, 3)), per((LANES, ns)), per((LANES, ns)),
                  per((ns, LANES)), per((ns, LANES)), pl.BlockSpec((1, LANES), lambda g: (0, g))],
        out_specs=blk,
        out_shape=jax.ShapeDtypeStruct(x3.shape, F32),
        scratch_shapes=[pltpu.VMEM(((SSM_CHUNK + 1) * LANES, LANES), BF16),
                        pltpu.VMEM((SSM_CHUNK * LANES, 2 * ns), BF16),
                        pltpu.VMEM((2 * ns, SSM_CHUNK * LANES), BF16),
                        pltpu.VMEM((rows, SSM_CHUNK * LANES), BF16),
                        pltpu.VMEM((rows, 2 * ns), F32),
                        pltpu.VMEM((rows, 2 * ns), BF16)],
        compiler_params=_params("parallel"),
        name="ssm",
    )(x3, *params, dskip)


def _ssm_params(lambda_re, lambda_im, log_step, b_re, b_im, c_re, c_im):
    nb, ng, ns = SSM_BLOCKS, SSM_LANE_GROUPS, SSM_BLOCK_STATE
    lam = jnp.stack([lambda_re.astype(F32), lambda_im.astype(F32),
                     jnp.broadcast_to(log_step.astype(F32)[:, None], lambda_re.shape)])
    lam_rows = jnp.transpose(lam.reshape(3, nb, ns), (1, 0, 2))
    lam_cols = jnp.transpose(lam_rows, (0, 2, 1))
    eye = jnp.eye(ng, dtype=F32)

    def blockdiag(m):
        a, b = m.shape[-2:]
        out = jnp.einsum('ngab,gh->ngahb', m.astype(F32).reshape(nb, ng, a, b), eye)
        return out.reshape(nb, ng * a, ng * b)

    bt = lambda m: blockdiag(jnp.swapaxes(m, -1, -2))
    return lam_rows, lam_cols, bt(b_re), bt(b_im), bt(c_re), bt(c_im)


def _router_tables(w_group, b_group, w_expert, b_expert):
    w = jnp.concatenate([w_group.astype(F32),
                         jnp.transpose(w_expert.astype(F32), (1, 0, 2)).reshape(D_MODEL, N_EXPERTS)], axis=1)
    b = jnp.concatenate([b_group.astype(F32), b_expert.astype(F32).reshape(N_EXPERTS)])
    pad = ROUTER_LANES - w.shape[1]
    return jnp.pad(w, ((0, 0), (0, pad))), jnp.pad(b, (0, pad)).reshape(1, ROUTER_LANES)


def kernel(x, norm_mix, norm_ffn, norm_final, attn_w_qkv, attn_w_o, attn_rel_bias, ssm_lambda_re, ssm_lambda_im, ssm_log_step, ssm_b_re, ssm_b_im, ssm_c_re, ssm_c_im, ssm_d, ssm_w_out, moe_w_group_router, moe_b_group_router, moe_w_expert_router, moe_b_expert_router, moe_w_gate, moe_w_up, moe_w_down):
    batch, seq, d = x.shape
    t = batch * seq
    row = lambda v: v.astype(F32).reshape(1, d)
    h = x.astype(F32).reshape(t, d)

    qkv = _qkv(h, row(norm_mix[0]), attn_w_qkv[0].astype(F32))
    o = _attention(qkv, _rel_bias_rows(attn_rel_bias[0]), batch, seq)
    wr, br = _router_tables(moe_w_group_router[0], moe_b_group_router[0],
                            moe_w_expert_router[0], moe_b_expert_router[0])
    h, hn_tt, route, counts = _post(o, h, attn_w_o[0].astype(F32), row(norm_ffn[0]), wr, br, glu=False)
    h, hn32 = _moe(0, h, hn_tt, route, counts, wr, br, moe_w_gate, moe_w_up, moe_w_down, row(norm_mix[1]),
                   emit_h=True)

    ssm_params = _ssm_params(ssm_lambda_re[0], ssm_lambda_im[0], ssm_log_step[0],
                             ssm_b_re[0], ssm_b_im[0], ssm_c_re[0], ssm_c_im[0])
    y = _ssm(hn32.reshape(batch, seq, d), ssm_params, row(ssm_d[0]))
    wr, br = _router_tables(moe_w_group_router[1], moe_b_group_router[1],
                            moe_w_expert_router[1], moe_b_expert_router[1])
    h, hn_tt, route, counts = _post(y.reshape(t, d), h, ssm_w_out[0].astype(F32), row(norm_ffn[1]), wr, br, glu=True)
    _, out = _moe(1, h, hn_tt, route, counts, wr, br, moe_w_gate, moe_w_up, moe_w_down, row(norm_final),
                 emit_h=False)
    return out.reshape(batch, seq, d).astype(x.dtype)
```

```python
import functools

import jax
import jax.numpy as jnp
import numpy as np
from jax import lax
from jax.experimental import pallas as pl
from jax.experimental.pallas import tpu as pltpu

F32 = jnp.float32
BF16 = jnp.bfloat16

D_MODEL = 1024
CHUNK = 64
LOOKBACK_CHUNKS = 8
BAND = (LOOKBACK_CHUNKS + 1) * CHUNK
N_HEADS = 16
HEAD_DIM = D_MODEL // N_HEADS
MAX_REL = 128
SSM_GROUP = 16
SSM_GROUPS = D_MODEL // SSM_GROUP
SSM_STATE = 64
N_EXPERT_GROUPS = 4
EXPERTS_PER_GROUP = 8
N_EXPERTS = N_EXPERT_GROUPS * EXPERTS_PER_GROUP
D_EXPERT = D_MODEL // 4
RMS_EPS = 1e-6
NEG_BIG = -1e30

LANES = 128
SUBLANES = 8
MXU_DIM = 256
VMEM_LIMIT = 60 * 1024 * 1024

HEADS_PER_GROUP = MXU_DIM // HEAD_DIM
N_HEAD_GROUPS = N_HEADS // HEADS_PER_GROUP
ATTN_QBLOCK = LOOKBACK_CHUNKS * CHUNK
ATTN_QROWS = 4 * ATTN_QBLOCK
REL_PAD = -(-(2 * MAX_REL + 1) // LANES) * LANES
BIAS_WIDTH = -(-(BAND + CHUNK - 1) // LANES) * LANES
ROUTER_LANES = LANES
ROUTER_EXPERT_LANE0 = N_EXPERT_GROUPS
SSM_CHUNK = 16
SSM_LANE_GROUPS = LANES // SSM_GROUP
SSM_BLOCKS = D_MODEL // LANES
SSM_BLOCK_STATE = SSM_LANE_GROUPS * SSM_STATE
TOKEN_ROWS = D_MODEL // LANES
MOE_TILE = 256
R_GROUP, R_RANK = range(2)
QKV_ROWS = 512
POST_ROWS = 512
DISPATCH_ROWS = 2048
COMBINE_ROWS = 1024
DMA_ISSUE_UNROLL = 8


def _dot(a, b):
    return jnp.dot(a, b, preferred_element_type=F32)


def _rms(x, g):
    return x * lax.rsqrt(jnp.mean(x * x, axis=-1, keepdims=True) + RMS_EPS) * g


def _sigmoid(x):
    return 1.0 / (1.0 + jnp.exp(-x))


def _gelu_tanh(x):
    c = np.float32(np.sqrt(2.0 / np.pi))
    return 0.5 * x * (1.0 + jnp.tanh(c * (x + np.float32(0.044715) * (x * x * x))))


def _params(*sem):
    return pltpu.CompilerParams(dimension_semantics=sem, vmem_limit_bytes=VMEM_LIMIT)


def _resident(shape):
    return pl.BlockSpec(shape, lambda *_: (0,) * len(shape), pipeline_mode=pl.Buffered(1))


def _cast_once(w_ref, wb):
    @pl.when(pl.program_id(0) == 0)
    def _():
        wb[...] = w_ref[...].astype(BF16)


def _qkv_kernel(x_ref, g_ref, w_ref, o_ref, wb):
    _cast_once(w_ref, wb)
    xn = _rms(x_ref[...], g_ref[...]).astype(BF16)
    for c in range(3):
        acc = _dot(xn, wb[:, c * D_MODEL:(c + 1) * D_MODEL])
        if c == 0:
            acc = acc * np.float32(HEAD_DIM ** -0.5)
        o_ref[:, c * D_MODEL:(c + 1) * D_MODEL] = acc.astype(BF16)


def _qkv(x, g, w, tm=QKV_ROWS):
    t = x.shape[0]
    return pl.pallas_call(
        _qkv_kernel,
        grid=(t // tm,),
        in_specs=[pl.BlockSpec((tm, D_MODEL), lambda i: (i, 0)),
                  pl.BlockSpec((1, D_MODEL), lambda i: (0, 0)),
                  _resident(w.shape)],
        out_specs=pl.BlockSpec((tm, 3 * D_MODEL), lambda i: (i, 0)),
        out_shape=jax.ShapeDtypeStruct((t, 3 * D_MODEL), BF16),
        scratch_shapes=[pltpu.VMEM(w.shape, BF16)],
        compiler_params=_params("arbitrary"),
        name="qkv",
    )(x, g, w)


def _expand_rel_bias(rel_ref, bias):
    rel = rel_ref[0]
    r1 = rel.astype(BF16)
    r2 = (rel - r1.astype(F32)).astype(BF16)
    r3 = (rel - r1.astype(F32) - r2.astype(F32)).astype(BF16)
    j = lax.broadcasted_iota(jnp.int32, (REL_PAD, BIAS_WIDTH), 1)
    src = jnp.clip(BAND - 1 - j, -MAX_REL, MAX_REL) + MAX_REL
    sel = jnp.where(lax.broadcasted_iota(jnp.int32, (REL_PAD, BIAS_WIDTH), 0) == src, 1.0, 0.0).astype(BF16)
    u = (_dot(r1, sel) + _dot(r2, sel)) + _dot(r3, sel)
    for h in range(HEADS_PER_GROUP):
        rows = jnp.broadcast_to(u[h:h + 1, :], (CHUNK, BIAS_WIDTH))
        rows = pltpu.roll(rows, BIAS_WIDTH - (CHUNK - 1), 1, stride=1, stride_axis=0)
        bias[h * CHUNK:(h + 1) * CHUNK, :] = rows[:, :BAND]


def _attn_kernel(q_ref, kp_ref, kc_ref, vp_ref, vc_ref, rel_ref, o_ref, kk, vv, bias):
    qb = pl.program_id(2)

    @pl.when(qb == 0)
    def _():
        _expand_rel_bias(rel_ref, bias)

    kk[0:ATTN_QBLOCK, :] = kp_ref[...]
    kk[ATTN_QBLOCK:, :] = kc_ref[...]
    vv[0:ATTN_QBLOCK, :] = vp_ref[...]
    vv[ATTN_QBLOCK:, :] = vc_ref[...]
    lane_head = lax.broadcasted_iota(jnp.int32, (CHUNK, MXU_DIM), 1) // HEAD_DIM
    col = lax.broadcasted_iota(jnp.int32, (1, BAND), 1)

    def chunks(first_block):
        for j in range(ATTN_QROWS // CHUNK):
            qj = q_ref[j * CHUNK:(j + 1) * CHUNK, :]
            lhs = jnp.concatenate(
                [jnp.where(lane_head == h, qj, jnp.zeros_like(qj)) for h in range(HEADS_PER_GROUP)], axis=0)
            kwin = kk[j * CHUNK:j * CHUNK + BAND, :]
            vwin = vv[j * CHUNK:j * CHUNK + BAND, :]
            s = lax.dot_general(lhs, kwin, (((1,), (1,)), ((), ())), preferred_element_type=F32) + bias[...]
            if first_block:
                s = jnp.where(col + j * CHUNK >= ATTN_QBLOCK, s, NEG_BIG)
            m = jnp.max(s, axis=-1, keepdims=True)
            p = jnp.exp(s - m)
            l = jnp.sum(p, axis=-1, keepdims=True)
            o_all = _dot(p.astype(BF16), vwin) * (1.0 / l)
            o = jnp.zeros((CHUNK, MXU_DIM), F32)
            for h in range(HEADS_PER_GROUP):
                o = o + jnp.where(lane_head == h, o_all[h * CHUNK:(h + 1) * CHUNK, :], 0.0)
            o_ref[j * CHUNK:(j + 1) * CHUNK, :] = o.astype(BF16)

    @pl.when(qb == 0)
    def _():
        chunks(True)

    @pl.when(qb > 0)
    def _():
        chunks(False)


def _attention(qkv, rel, batch, seq):
    nqb = seq // ATTN_QROWS
    per = ATTN_QROWS // ATTN_QBLOCK
    kcol = D_MODEL // MXU_DIM
    blk = (ATTN_QROWS, MXU_DIM)
    back = (ATTN_QBLOCK, MXU_DIM)
    cur = lambda off: (lambda b, g, i: (b * nqb + i, off + g))
    prev = lambda off: (lambda b, g, i: ((b * nqb + i) * per - jnp.minimum(i, 1), off + g))
    return pl.pallas_call(
        _attn_kernel,
        grid=(batch, N_HEAD_GROUPS, nqb),
        in_specs=[pl.BlockSpec(blk, cur(0)),
                  pl.BlockSpec(back, prev(kcol)), pl.BlockSpec(blk, cur(kcol)),
                  pl.BlockSpec(back, prev(2 * kcol)), pl.BlockSpec(blk, cur(2 * kcol)),
                  pl.BlockSpec((1, SUBLANES, REL_PAD), lambda b, g, i: (g, 0, 0))],
        out_specs=pl.BlockSpec(blk, cur(0)),
        out_shape=jax.ShapeDtypeStruct((batch * seq, D_MODEL), BF16),
        scratch_shapes=[pltpu.VMEM((ATTN_QBLOCK + ATTN_QROWS, MXU_DIM), BF16),
                        pltpu.VMEM((ATTN_QBLOCK + ATTN_QROWS, MXU_DIM), BF16),
                        pltpu.VMEM((HEADS_PER_GROUP * CHUNK, BAND), F32)],
        compiler_params=_params("parallel", "parallel", "arbitrary"),
        name="attn",
    )(qkv, qkv, qkv, qkv, qkv, rel)


def _rel_bias_rows(rel_bias):
    rel = rel_bias.astype(F32).reshape(N_HEAD_GROUPS, HEADS_PER_GROUP, 2 * MAX_REL + 1)
    return jnp.pad(rel, ((0, 0), (0, SUBLANES - HEADS_PER_GROUP), (0, REL_PAD - (2 * MAX_REL + 1))))


def _split_router(wr_ref, w2_s):
    w = wr_ref[...]
    w_hi = w.astype(BF16)
    w2_s[:D_MODEL, :ROUTER_LANES] = w_hi
    w2_s[:D_MODEL, ROUTER_LANES:] = (w - w_hi.astype(F32)).astype(BF16)
    w2_s[D_MODEL:, :ROUTER_LANES] = w_hi
    w2_s[D_MODEL:, ROUTER_LANES:] = jnp.zeros((D_MODEL, ROUTER_LANES), BF16)


def _router_logits(hn, w2_s, br_ref, one_matmul=False):
    a_hi = hn.astype(BF16)
    a_lo = (hn - a_hi.astype(F32)).astype(BF16)
    if not one_matmul:
        w_hi = w2_s[:D_MODEL, :ROUTER_LANES]
        w_lo = w2_s[:D_MODEL, ROUTER_LANES:]
        return _dot(a_hi, w_hi) + (_dot(a_hi, w_lo) + _dot(a_lo, w_hi)) + br_ref[...]
    out = _dot(jnp.concatenate([a_hi, a_lo], axis=1), w2_s[...])
    return out[:, :ROUTER_LANES] + out[:, ROUTER_LANES:] + br_ref[...]


def _strict_lower(tri_s):
    rows = tri_s.shape[0]
    below = (lax.broadcasted_iota(jnp.int32, (rows, rows), 0)
             > lax.broadcasted_iota(jnp.int32, (rows, rows), 1))
    tri_s[...] = jnp.where(below, 1.0, 0.0).astype(BF16)


def _route_group(lg, base, tri_s):
    lane = lax.broadcasted_iota(jnp.int32, lg.shape, 1)
    gl = jnp.where(lane < N_EXPERT_GROUPS, lg, np.float32(-np.inf))
    gmax = jnp.max(gl, axis=-1, keepdims=True)
    gidx = jnp.min(jnp.where(gl == gmax, lane, np.int32(ROUTER_LANES)), axis=-1, keepdims=True)
    og = lane == gidx
    ogf = jnp.where(og, 1.0, 0.0)
    rank = jnp.sum(jnp.where(og, base + _dot(tri_s[...], ogf.astype(BF16)), 0.0), axis=-1, keepdims=True)
    rec = jnp.where(lane == R_GROUP, gidx.astype(F32), jnp.where(lane == R_RANK, rank, 0.0))
    return rec, base + jnp.sum(ogf, axis=0, keepdims=True)


def _route_experts(lg, g):
    lane = lax.broadcasted_iota(jnp.int32, lg.shape, 1)
    ninf = np.float32(-np.inf)
    big = np.int32(ROUTER_LANES)
    gmask = lane < N_EXPERT_GROUPS
    gmax = jnp.max(jnp.where(gmask, lg, ninf), axis=-1, keepdims=True)
    ge = jnp.where(gmask, jnp.exp(lg - gmax), 0.0)
    gprob = jnp.sum(jnp.where(lane == g, ge, 0.0), axis=-1, keepdims=True) / jnp.sum(ge, axis=-1, keepdims=True)
    lo = ROUTER_EXPERT_LANE0 + EXPERTS_PER_GROUP * g
    el = jnp.where((lane >= lo) & (lane < lo + EXPERTS_PER_GROUP), lg, ninf)
    l1 = jnp.max(el, axis=-1, keepdims=True)
    i1 = jnp.min(jnp.where(el == l1, lane, big), axis=-1, keepdims=True)
    el2 = jnp.where(lane == i1, ninf, el)
    l2 = jnp.max(el2, axis=-1, keepdims=True)
    i2 = jnp.min(jnp.where(el2 == l2, lane, big), axis=-1, keepdims=True)
    t = jnp.exp(l2 - l1)
    w1 = gprob / (1.0 + t)
    return i1, w1, i2, w1 * t


def _to_token_tiles(ref, x):
    rows = x.shape[0]
    for j in range(TOKEN_ROWS):
        ref[pl.ds(j, rows, stride=TOKEN_ROWS), :] = x[:, j * LANES:(j + 1) * LANES]


def _from_token_tiles(ref, rows):
    return jnp.concatenate([ref[pl.ds(j, rows, stride=TOKEN_ROWS), :] for j in range(TOKEN_ROWS)], axis=1)


def _post_kernel(glu, y_ref, h_ref, w_ref, g_ref, wr_ref, br_ref, hout_ref, hntt_ref, route_ref, cnt_ref,
                 wb, w2_s, tri_s):
    @pl.when(pl.program_id(0) == 0)
    def _():
        cnt_ref[...] = jnp.zeros_like(cnt_ref)
        _split_router(wr_ref, w2_s)
        _strict_lower(tri_s)

    _cast_once(w_ref, wb)
    y = y_ref[...].astype(BF16)
    if glu:
        mix = _dot(y, wb[:, :D_MODEL]) * _sigmoid(_dot(y, wb[:, D_MODEL:]))
    else:
        mix = _dot(y, wb[...])
    h = h_ref[...] + mix
    hout_ref[...] = h
    hn = _rms(h, g_ref[...])
    _to_token_tiles(hntt_ref, hn)
    route_ref[...], cnt_ref[...] = _route_group(_router_logits(hn, w2_s, br_ref), cnt_ref[...], tri_s)


def _post(y, h, w, g, wr, br, glu, tm=POST_ROWS):
    t = h.shape[0]
    row = lambda i: (i, 0)
    fixed = lambda i: (0, 0)
    return pl.pallas_call(
        functools.partial(_post_kernel, glu),
        grid=(t // tm,),
        in_specs=[pl.BlockSpec((tm, D_MODEL), row), pl.BlockSpec((tm, D_MODEL), row),
                  _resident(w.shape), pl.BlockSpec((1, D_MODEL), fixed),
                  pl.BlockSpec((D_MODEL, ROUTER_LANES), fixed), pl.BlockSpec((1, ROUTER_LANES), fixed)],
        out_specs=[pl.BlockSpec((tm, D_MODEL), row), pl.BlockSpec((tm * TOKEN_ROWS, LANES), row),
                   pl.BlockSpec((tm, ROUTER_LANES), row), pl.BlockSpec((1, ROUTER_LANES), fixed)],
        out_shape=[jax.ShapeDtypeStruct((t, D_MODEL), F32), jax.ShapeDtypeStruct((t * TOKEN_ROWS, LANES), F32),
                   jax.ShapeDtypeStruct((t, ROUTER_LANES), F32), jax.ShapeDtypeStruct((1, ROUTER_LANES), F32)],
        scratch_shapes=[pltpu.VMEM(w.shape, BF16), pltpu.VMEM((2 * D_MODEL, 2 * ROUTER_LANES), BF16),
                        pltpu.VMEM((tm, tm), BF16)],
        compiler_params=_params("arbitrary"),
        name="post_glu" if glu else "post_attn",
    )(y, h, w, g, wr, br)


def _moe_tiles(t):
    return t // MOE_TILE + N_EXPERT_GROUPS


def _dispatch_plan(route, counts, t):
    g = route[:, R_GROUP].astype(jnp.int32)
    rank = route[:, R_RANK].astype(jnp.int32)
    cnt = counts[0, :N_EXPERT_GROUPS].astype(jnp.int32)
    padded = ((cnt + (MOE_TILE - 1)) // MOE_TILE) * MOE_TILE
    ends = jnp.cumsum(padded)
    off = ends - padded
    onehot = g[:, None] == jnp.arange(N_EXPERT_GROUPS, dtype=jnp.int32)
    pos = jnp.sum(jnp.where(onehot, off, 0), axis=-1) + rank
    ntiles = _moe_tiles(t)
    tile_start = jnp.arange(ntiles, dtype=jnp.int32) * MOE_TILE
    tile_group = jnp.minimum(jnp.sum((tile_start[:, None] >= ends[None, :]).astype(jnp.int32), axis=1),
                             N_EXPERT_GROUPS - 1)
    n_active = (ends[-1] // MOE_TILE).reshape(1)
    plan = jnp.concatenate([off + cnt, padded - cnt, n_active])
    groups = jnp.arange(N_EXPERT_GROUPS, dtype=jnp.int32)
    later = (groups[None, :] > groups[:, None]) & (cnt[None, :] > 0)
    following = jnp.min(jnp.where(later, groups[None, :], N_EXPERT_GROUPS), axis=1)
    following = jnp.where(following == N_EXPERT_GROUPS, -1, following)
    next_group = jnp.sum(jnp.where(tile_group[:, None] == groups[None, :], following[None, :], 0), axis=1)
    return pos, tile_group, n_active, next_group, plan


def _token_tile(ref, i):
    return ref.at[pl.ds(pl.multiple_of(i * TOKEN_ROWS, TOKEN_ROWS), TOKEN_ROWS), :]


def _zero_fill(plan_ref, xs_ref, zbuf, zsem, ntiles):
    nseg = N_EXPERT_GROUPS
    zbuf[...] = jnp.zeros_like(zbuf)
    half = MOE_TILE // 2
    sizes = [1 << b for b in reversed(range(half.bit_length()))]

    def chunk(first, size):
        return pltpu.make_async_copy(zbuf.at[pl.ds(0, size * TOKEN_ROWS), :],
                                     xs_ref.at[pl.ds(pl.multiple_of(first * TOKEN_ROWS, TOKEN_ROWS),
                                                     size * TOKEN_ROWS), :], zsem)

    def pads(wait):
        def body(e, c):
            first = plan_ref[e]
            n = plan_ref[nseg + e]
            for size in sizes:
                hit = (n & size) != 0

                @pl.when(hit)
                def _():
                    cp = chunk(first, size)
                    cp.wait() if wait else cp.start()

                first = first + jnp.where(hit, size, 0)
            return c
        lax.fori_loop(0, nseg, body, 0)

    def tail(wait):
        def body(i, c):
            for k in range(2):
                cp = chunk(i * MOE_TILE + k * half, half)
                cp.wait() if wait else cp.start()
            return c
        lax.fori_loop(plan_ref[2 * nseg], ntiles, body, 0)

    pads(False)
    tail(False)
    pads(True)
    tail(True)


def _dispatch_kernel(ntiles, plan_ref, pos_ref, hn_ref, xs_ref, sem, zbuf, zsem):
    rows = hn_ref.shape[0] // TOKEN_ROWS

    @pl.when(pl.program_id(0) == 0)
    def _():
        _zero_fill(plan_ref, xs_ref, zbuf, zsem, ntiles)

    def copy(r):
        return pltpu.make_async_copy(_token_tile(hn_ref, r), _token_tile(xs_ref, pos_ref[0, 0, r]), sem)

    def issue(r, c):
        copy(2 * r).start(priority=0)
        copy(2 * r + 1).start(priority=1)
        return c

    lax.fori_loop(0, rows // 2, issue, 0, unroll=DMA_ISSUE_UNROLL)
    pltpu.make_async_copy(hn_ref, xs_ref.at[pl.ds(0, rows * TOKEN_ROWS), :], sem).wait()


def _pos_blocks(pos, tm):
    return pos.reshape(pos.shape[0] // tm, 1, tm)


def _dispatch(plan, pos, hn_tt, t, tm=DISPATCH_ROWS):
    ntiles = _moe_tiles(t)
    nrows = ntiles * MOE_TILE * TOKEN_ROWS
    return pl.pallas_call(
        functools.partial(_dispatch_kernel, ntiles),
        grid_spec=pltpu.PrefetchScalarGridSpec(
            num_scalar_prefetch=1,
            grid=(t // tm,),
            in_specs=[pl.BlockSpec((1, 1, tm), lambda i, plan: (i, 0, 0), memory_space=pltpu.SMEM),
                      pl.BlockSpec((tm * TOKEN_ROWS, LANES), lambda i, plan: (i, 0))],
            out_specs=pl.BlockSpec(memory_space=pl.ANY),
            scratch_shapes=[pltpu.SemaphoreType.DMA(()),
                            pltpu.VMEM((MOE_TILE // 2 * TOKEN_ROWS, LANES), F32),
                            pltpu.SemaphoreType.DMA(())]),
        out_shape=jax.ShapeDtypeStruct((nrows, LANES), F32),
        compiler_params=_params("arbitrary"),
        name="moe_dispatch",
    )(plan, _pos_blocks(pos, tm), hn_tt)


def _expert_kernel(layer, tg_ref, na_ref, ng_ref, xs_ref, wr_ref, br_ref, wg_hbm, wu_hbm, wd_hbm, ys_ref,
                   wg_f, wu_f, wd_f, wgu_s, wd_s, w2_s, sems):
    i = pl.program_id(0)
    g = tg_ref[i]
    new_group = (i == 0) | (tg_ref[jnp.maximum(i - 1, 0)] != g)
    width = EXPERTS_PER_GROUP * D_EXPERT

    def fetch(group):
        return [pltpu.make_async_copy(w.at[layer, group], stage, sems.at[k])
                for k, (w, stage) in enumerate(((wg_hbm, wg_f), (wu_hbm, wu_f), (wd_hbm, wd_f)))]

    @pl.when(i < na_ref[0])
    def _():
        @pl.when(new_group)
        def _():
            @pl.when(i == 0)
            def _():
                for cp in fetch(g):
                    cp.start()
                _split_router(wr_ref, w2_s)

            for cp in fetch(g):
                cp.wait()
            for e in range(EXPERTS_PER_GROUP):
                cols = slice(e * D_EXPERT, (e + 1) * D_EXPERT)
                wgu_s[:, cols] = wg_f[e].astype(BF16)
                wgu_s[:, width + e * D_EXPERT:width + (e + 1) * D_EXPERT] = wu_f[e].astype(BF16)
                wd_s[cols, :] = wd_f[e].astype(BF16)
            nxt = ng_ref[i]

            @pl.when(nxt >= 0)
            def _():
                for cp in fetch(nxt):
                    cp.start()

        x = _from_token_tiles(xs_ref, MOE_TILE)
        i1, w1, i2, w2 = _route_experts(_router_logits(x, w2_s, br_ref, one_matmul=True), g)
        xb = x.astype(BF16)
        first = ROUTER_EXPERT_LANE0 + g * EXPERTS_PER_GROUP
        hid = []
        for e in range(EXPERTS_PER_GROUP):
            a = _dot(xb, wgu_s[:, e * D_EXPERT:(e + 1) * D_EXPERT])
            u = _dot(xb, wgu_s[:, width + e * D_EXPERT:width + (e + 1) * D_EXPERT])
            gate = jnp.where(i1 == first + e, w1, 0.0) + jnp.where(i2 == first + e, w2, 0.0)
            hid.append(((a * _sigmoid(a)) * u * gate).astype(BF16))
        _to_token_tiles(ys_ref, _dot(jnp.concatenate(hid, axis=1), wd_s[...]))


def _experts(layer, tile_group, n_active, next_group, xs, wr, br, w_gate, w_up, w_down):
    ntiles = tile_group.shape[0]
    last = lambda i, na: jnp.minimum(i, na[0] - 1)
    tile = pl.BlockSpec((MOE_TILE * TOKEN_ROWS, LANES), lambda i, tg, na, ng: (last(i, na), 0))
    fixed = lambda a: pl.BlockSpec(a.shape, lambda i, tg, na, ng: (0, 0))
    grouped = lambda w: w.reshape((w.shape[0], N_EXPERT_GROUPS, EXPERTS_PER_GROUP) + w.shape[2:])
    stage = lambda w: pltpu.VMEM((EXPERTS_PER_GROUP,) + w.shape[2:], F32)
    hbm = pl.BlockSpec(memory_space=pl.ANY)
    width = EXPERTS_PER_GROUP * D_EXPERT
    return pl.pallas_call(
        functools.partial(_expert_kernel, layer),
        grid_spec=pltpu.PrefetchScalarGridSpec(
            num_scalar_prefetch=3,
            grid=(ntiles,),
            in_specs=[tile, fixed(wr), fixed(br), hbm, hbm, hbm],
            out_specs=tile,
            scratch_shapes=[stage(w_gate), stage(w_up), stage(w_down),
                            pltpu.VMEM((D_MODEL, 2 * width), BF16), pltpu.VMEM((width, D_MODEL), BF16),
                            pltpu.VMEM((2 * D_MODEL, 2 * ROUTER_LANES), BF16),
                            pltpu.SemaphoreType.DMA((3,))]),
        out_shape=jax.ShapeDtypeStruct(xs.shape, F32),
        input_output_aliases={3: 0},
        compiler_params=_params("arbitrary"),
        name="moe_experts",
    )(tile_group, n_active, next_group, xs, wr, br, grouped(w_gate), grouped(w_up), grouped(w_down))


def _combine_kernel(emit_h, pos_ref, nxt_ref, h_ref, gn_ref, ys_ref, *refs):
    outs, (buf0, buf1, sem0, sem1) = refs[:-4], refs[-4:]
    bufs, sems = (buf0, buf1), (sem0, sem1)
    rows = h_ref.shape[0]
    i = pl.program_id(0)

    def gather(idx_ref, slot):
        def copy(r):
            return pltpu.make_async_copy(_token_tile(ys_ref, idx_ref[0, 0, r]), _token_tile(bufs[slot], r),
                                         sems[slot])

        def issue(r, c):
            copy(2 * r).start(priority=0)
            copy(2 * r + 1).start(priority=1)
            return c

        lax.fori_loop(0, rows // 2, issue, 0, unroll=DMA_ISSUE_UNROLL)

    @pl.when(i == 0)
    def _():
        gather(pos_ref, 0)

    for slot in range(2):
        @pl.when(i % 2 == slot)
        def _():
            @pl.when(i + 1 < pl.num_programs(0))
            def _():
                gather(nxt_ref, 1 - slot)

            pltpu.make_async_copy(ys_ref.at[pl.ds(0, rows * TOKEN_ROWS), :], bufs[slot], sems[slot]).wait()
            h = h_ref[...] + _from_token_tiles(bufs[slot], rows)
            if emit_h:
                outs[0][...] = h
            outs[-1][...] = _rms(h, gn_ref[...])


def _combine(pos, h, gnext, ys, emit_h, tm=COMBINE_ROWS):
    t = h.shape[0]
    nblk = t // tm
    row = lambda i: (i, 0)
    blocks = _pos_blocks(pos, tm)
    smem = lambda imap: pl.BlockSpec((1, 1, tm), imap, memory_space=pltpu.SMEM)
    n_out = 2 if emit_h else 1
    out = pl.pallas_call(
        functools.partial(_combine_kernel, emit_h),
        grid=(nblk,),
        in_specs=[smem(lambda i: (i, 0, 0)), smem(lambda i: (jnp.minimum(i + 1, nblk - 1), 0, 0)),
                  pl.BlockSpec((tm, D_MODEL), row),
                  pl.BlockSpec((1, D_MODEL), lambda i: (0, 0)),
                  pl.BlockSpec(memory_space=pl.ANY)],
        out_specs=[pl.BlockSpec((tm, D_MODEL), row)] * n_out,
        out_shape=[jax.ShapeDtypeStruct((t, D_MODEL), F32)] * n_out,
        scratch_shapes=[pltpu.VMEM((tm * TOKEN_ROWS, LANES), F32), pltpu.VMEM((tm * TOKEN_ROWS, LANES), F32),
                        pltpu.SemaphoreType.DMA(()), pltpu.SemaphoreType.DMA(())],
        compiler_params=_params("arbitrary"),
        name="moe_combine",
    )(blocks, blocks, h, gnext, ys)
    return (out[0], out[1]) if emit_h else (None, out[0])


def _moe(layer, h, hn_tt, route, counts, wr, br, w_gate, w_up, w_down, gnext, emit_h):
    t = h.shape[0]
    pos, tile_group, n_active, next_group, plan = _dispatch_plan(route, counts, t)
    xs = _dispatch(plan, pos, hn_tt, t)
    ys = _experts(layer, tile_group, n_active, next_group, xs, wr, br, w_gate, w_up, w_down)
    return _combine(pos, h, gnext, ys, emit_h)


def _dot3(a, b):
    a_hi = a.astype(BF16)
    a_lo = (a - a_hi.astype(F32)).astype(BF16)
    b_hi = b.astype(BF16)
    b_lo = (b - b_hi.astype(F32)).astype(BF16)
    return _dot(a_hi, b_hi) + (_dot(a_hi, b_lo) + _dot(a_lo, b_hi))


def _ssm_powers(lam_re, lam_im, log_step, lags):
    lam_re = jnp.minimum(lam_re, np.float32(-1e-4))
    step = jnp.exp(log_step)
    dec = jnp.exp(lags * (lam_re * step))
    ang = lags * (lam_im * step)
    return dec * jnp.cos(ang), dec * jnp.sin(ang)


def _ssm_operators(lamr_ref, lamc_ref, bre_ref, bim_ref, cre_ref, cim_ref, r_s, wd_s, vd_s):
    L, ns = SSM_CHUNK, SSM_BLOCK_STATE
    lam = lamr_ref[0]
    lre, lim, lst = lam[0:1], lam[1:2], lam[2:3]
    lag_rev = (L - 1 - lax.broadcasted_iota(jnp.int32, (L, 1), 0)).astype(F32)
    pr_re, pr_im = _ssm_powers(lre, lim, lst, lag_rev)
    ab_re, ab_im = _ssm_powers(lre, lim, lst, np.float32(1.0))
    lam_re = jnp.minimum(lre, np.float32(-1e-4))
    denom = lam_re * lam_re + lim * lim
    num_re = ab_re - 1.0
    f_re = (num_re * lam_re + ab_im * lim) / denom
    f_im = (ab_im * lam_re - num_re * lim) / denom
    bre, bim = bre_ref[0], bim_ref[0]
    bb_re = f_re * bre - f_im * bim
    bb_im = f_re * bim + f_im * bre
    cre, cim = cre_ref[0], cim_ref[0]
    for s in range(L):
        a_re, a_im = pr_re[s:s + 1], pr_im[s:s + 1]
        w_re = a_re * bb_re - a_im * bb_im
        w_im = a_re * bb_im + a_im * bb_re
        rows = slice(s * LANES, (s + 1) * LANES)
        wd_s[rows, 0:ns] = w_re.astype(BF16)
        wd_s[rows, ns:2 * ns] = w_im.astype(BF16)
        r_s[rows, :] = (_dot3(w_re, cre) - _dot3(w_im, cim)).astype(BF16)
    r_s[L * LANES:, :] = jnp.zeros((LANES, LANES), BF16)
    lamc = lamc_ref[0]
    lag1 = (1 + lax.broadcasted_iota(jnp.int32, (1, L), 1)).astype(F32)
    p1_re, p1_im = _ssm_powers(lamc[:, 0:1], lamc[:, 1:2], lamc[:, 2:3], lag1)
    for t in range(L):
        a_re, a_im = p1_re[:, t:t + 1], p1_im[:, t:t + 1]
        cols = slice(t * LANES, (t + 1) * LANES)
        vd_s[0:ns, cols] = (cre * a_re - cim * a_im).astype(BF16)
        vd_s[ns:2 * ns, cols] = (-(cre * a_im + cim * a_re)).astype(BF16)
    return _ssm_powers(lre, lim, lst, np.float32(L))


def _ssm_kernel(nch, x_ref, lamr_ref, lamc_ref, bre_ref, bim_ref, cre_ref, cim_ref, d_ref, o_ref,
                r_s, wd_s, vd_s, xcat, st, xprev):
    nb = x_ref.shape[0]
    ns = SSM_BLOCK_STATE
    ar, ai = _ssm_operators(lamr_ref, lamc_ref, bre_ref, bim_ref, cre_ref, cim_ref, r_s, wd_s, vd_s)
    for s in range(SSM_CHUNK):
        for b in range(nb):
            xs = x_ref[b, pl.ds(s, nch, stride=SSM_CHUNK), :]
            xcat[b * nch:(b + 1) * nch, s * LANES:(s + 1) * LANES] = xs.astype(BF16)
    st[...] = _dot(xcat[...], wd_s[...])

    def step(k, carry):
        out = []
        for b in range(nb):
            xr, xi = carry[b]
            row = b * nch + k
            sr = st[pl.ds(row, 1), 0:ns]
            si = st[pl.ds(row, 1), ns:2 * ns]
            st[pl.ds(row, 1), 0:ns] = xr
            st[pl.ds(row, 1), ns:2 * ns] = xi
            out.append((ar * xr - ai * xi + sr, ar * xi + ai * xr + si))
        return tuple(out)

    zero = jnp.zeros((1, ns), F32)
    lax.fori_loop(0, nch, step, tuple((zero, zero) for _ in range(nb)))
    xprev[...] = st[...].astype(BF16)
    dskip = d_ref[...]
    L = SSM_CHUNK
    for t in range(0, L, 2):
        taps = jnp.concatenate([r_s[(L - 1 - t) * LANES:, :], r_s[(L - 2 - t) * LANES:L * LANES, :]], axis=1)
        y2 = (_dot(xcat[:, :(t + 2) * LANES], taps)
              + _dot(xprev[...], vd_s[:, t * LANES:(t + 2) * LANES]))
        for k in range(2):
            y = y2[:, k * LANES:(k + 1) * LANES]
            for b in range(nb):
                u = x_ref[b, pl.ds(t + k, nch, stride=L), :]
                o_ref[b, pl.ds(t + k, nch, stride=L), :] = _gelu_tanh(y[b * nch:(b + 1) * nch, :] + dskip * u)


def _ssm(x3, params, dskip):
    nb, seq, _ = x3.shape
    nch = seq // SSM_CHUNK
    rows = nb * nch
    ns = SSM_BLOCK_STATE
    blk = pl.BlockSpec((nb, seq, LANES), lambda g: (0, 0, g))
    per = lambda shape: pl.BlockSpec((1,) + shape, lambda g: (g, 0, 0))
    return pl.pallas_call(
        functools.partial(_ssm_kernel, nch),
        grid=(SSM_BLOCKS,),
        in_specs=[blk, per((3, ns)), per((ns, 3)), per((LANES, ns)), per((LANES, ns)),
                  per((ns, LANES)), per((ns, LANES)), pl.BlockSpec((1, LANES), lambda g: (0, g))],
        out_specs=blk,
        out_shape=jax.ShapeDtypeStruct(x3.shape, F32),
        scratch_shapes=[pltpu.VMEM(((SSM_CHUNK + 1) * LANES, LANES), BF16),
                        pltpu.VMEM((SSM_CHUNK * LANES, 2 * ns), BF16),
                        pltpu.VMEM((2 * ns, SSM_CHUNK * LANES), BF16),
                        pltpu.VMEM((rows, SSM_CHUNK * LANES), BF16),
                        pltpu.VMEM((rows, 2 * ns), F32),
                        pltpu.VMEM((rows, 2 * ns), BF16)],
        compiler_params=_params("parallel"),
        name="ssm",
    )(x3, *params, dskip)


def _ssm_params(lambda_re, lambda_im, log_step, b_re, b_im, c_re, c_im):
    nb, ng, ns = SSM_BLOCKS, SSM_LANE_GROUPS, SSM_BLOCK_STATE
    lam = jnp.stack([lambda_re.astype(F32), lambda_im.astype(F32),
                     jnp.broadcast_to(log_step.astype(F32)[:, None], lambda_re.shape)])
    lam_rows = jnp.transpose(lam.reshape(3, nb, ns), (1, 0, 2))
    lam_cols = jnp.transpose(lam_rows, (0, 2, 1))
    eye = jnp.eye(ng, dtype=F32)

    def blockdiag(m):
        a, b = m.shape[-2:]
        out = jnp.einsum('ngab,gh->ngahb', m.astype(F32).reshape(nb, ng, a, b), eye)
        return out.reshape(nb, ng * a, ng * b)

    bt = lambda m: blockdiag(jnp.swapaxes(m, -1, -2))
    return lam_rows, lam_cols, bt(b_re), bt(b_im), bt(c_re), bt(c_im)


def _router_tables(w_group, b_group, w_expert, b_expert):
    w = jnp.concatenate([w_group.astype(F32),
                         jnp.transpose(w_expert.astype(F32), (1, 0, 2)).reshape(D_MODEL, N_EXPERTS)], axis=1)
    b = jnp.concatenate([b_group.astype(F32), b_expert.astype(F32).reshape(N_EXPERTS)])
    pad = ROUTER_LANES - w.shape[1]
    return jnp.pad(w, ((0, 0), (0, pad))), jnp.pad(b, (0, pad)).reshape(1, ROUTER_LANES)


def kernel(x, norm_mix, norm_ffn, norm_final, attn_w_qkv, attn_w_o, attn_rel_bias, ssm_lambda_re, ssm_lambda_im, ssm_log_step, ssm_b_re, ssm_b_im, ssm_c_re, ssm_c_im, ssm_d, ssm_w_out, moe_w_group_router, moe_b_group_router, moe_w_expert_router, moe_b_expert_router, moe_w_gate, moe_w_up, moe_w_down):
    batch, seq, d = x.shape
    t = batch * seq
    row = lambda v: v.astype(F32).reshape(1, d)
    h = x.astype(F32).reshape(t, d)

    qkv = _qkv(h, row(norm_mix[0]), attn_w_qkv[0].astype(F32))
    o = _attention(qkv, _rel_bias_rows(attn_rel_bias[0]), batch, seq)
    wr, br = _router_tables(moe_w_group_router[0], moe_b_group_router[0],
                            moe_w_expert_router[0], moe_b_expert_router[0])
    h, hn_tt, route, counts = _post(o, h, attn_w_o[0].astype(F32), row(norm_ffn[0]), wr, br, glu=False)
    h, hn32 = _moe(0, h, hn_tt, route, counts, wr, br, moe_w_gate, moe_w_up, moe_w_down, row(norm_mix[1]),
                   emit_h=True)

    ssm_params = _ssm_params(ssm_lambda_re[0], ssm_lambda_im[0], ssm_log_step[0],
                             ssm_b_re[0], ssm_b_im[0], ssm_c_re[0], ssm_c_im[0])
    y = _ssm(hn32.reshape(batch, seq, d), ssm_params, row(ssm_d[0]))
    wr, br = _router_tables(moe_w_group_router[1], moe_b_group_router[1],
                            moe_w_expert_router[1], moe_b_expert_router[1])
    h, hn_tt, route, counts = _post(y.reshape(t, d), h, ssm_w_out[0].astype(F32), row(norm_ffn[1]), wr, br, glu=True)
    _, out = _moe(1, h, hn_tt, route, counts, wr, br, moe_w_gate, moe_w_up, moe_w_down, row(norm_final),
                 emit_h=False)
    return out.reshape(batch, seq, d).astype(x.dtype)
```

```python
import functools

import jax
import jax.numpy as jnp
import numpy as np
from jax import lax
from jax.experimental import pallas as pl
from jax.experimental.pallas import tpu as pltpu

F32 = jnp.float32
BF16 = jnp.bfloat16

D_MODEL = 1024
CHUNK = 64
LOOKBACK_CHUNKS = 8
BAND = (LOOKBACK_CHUNKS + 1) * CHUNK
N_HEADS = 16
HEAD_DIM = D_MODEL // N_HEADS
MAX_REL = 128
SSM_GROUP = 16
SSM_GROUPS = D_MODEL // SSM_GROUP
SSM_STATE = 64
N_EXPERT_GROUPS = 4
EXPERTS_PER_GROUP = 8
N_EXPERTS = N_EXPERT_GROUPS * EXPERTS_PER_GROUP
D_EXPERT = D_MODEL // 4
RMS_EPS = 1e-6
NEG_BIG = -1e30

LANES = 128
SUBLANES = 8
MXU_DIM = 256
VMEM_LIMIT = 60 * 1024 * 1024

HEADS_PER_GROUP = MXU_DIM // HEAD_DIM
N_HEAD_GROUPS = N_HEADS // HEADS_PER_GROUP
ATTN_QBLOCK = LOOKBACK_CHUNKS * CHUNK
ATTN_QROWS = 4 * ATTN_QBLOCK
REL_PAD = -(-(2 * MAX_REL + 1) // LANES) * LANES
BIAS_WIDTH = -(-(BAND + CHUNK - 1) // LANES) * LANES
ROUTER_LANES = LANES
ROUTER_EXPERT_LANE0 = N_EXPERT_GROUPS
SSM_CHUNK = 16
SSM_LANE_GROUPS = LANES // SSM_GROUP
SSM_BLOCKS = D_MODEL // LANES
SSM_BLOCK_STATE = SSM_LANE_GROUPS * SSM_STATE
TOKEN_ROWS = D_MODEL // LANES
MOE_TILE = 256
R_GROUP, R_RANK = range(2)
QKV_ROWS = 512
POST_ROWS = 512
DISPATCH_ROWS = 2048
COMBINE_ROWS = 512
DMA_ISSUE_UNROLL = 4


def _dot(a, b):
    return jnp.dot(a, b, preferred_element_type=F32)


def _rms(x, g):
    return x * lax.rsqrt(jnp.mean(x * x, axis=-1, keepdims=True) + RMS_EPS) * g


def _sigmoid(x):
    return 1.0 / (1.0 + jnp.exp(-x))


def _gelu_tanh(x):
    c = np.float32(np.sqrt(2.0 / np.pi))
    return 0.5 * x * (1.0 + jnp.tanh(c * (x + np.float32(0.044715) * (x * x * x))))


def _params(*sem):
    return pltpu.CompilerParams(dimension_semantics=sem, vmem_limit_bytes=VMEM_LIMIT)


def _resident(shape):
    return pl.BlockSpec(shape, lambda *_: (0,) * len(shape), pipeline_mode=pl.Buffered(1))


def _cast_once(w_ref, wb):
    @pl.when(pl.program_id(0) == 0)
    def _():
        wb[...] = w_ref[...].astype(BF16)


def _qkv_kernel(x_ref, g_ref, w_ref, o_ref, wb):
    _cast_once(w_ref, wb)
    xn = _rms(x_ref[...], g_ref[...]).astype(BF16)
    for c in range(3):
        acc = _dot(xn, wb[:, c * D_MODEL:(c + 1) * D_MODEL])
        if c == 0:
            acc = acc * np.float32(HEAD_DIM ** -0.5)
        o_ref[:, c * D_MODEL:(c + 1) * D_MODEL] = acc.astype(BF16)


def _qkv(x, g, w, tm=QKV_ROWS):
    t = x.shape[0]
    return pl.pallas_call(
        _qkv_kernel,
        grid=(t // tm,),
        in_specs=[pl.BlockSpec((tm, D_MODEL), lambda i: (i, 0)),
                  pl.BlockSpec((1, D_MODEL), lambda i: (0, 0)),
                  _resident(w.shape)],
        out_specs=pl.BlockSpec((tm, 3 * D_MODEL), lambda i: (i, 0)),
        out_shape=jax.ShapeDtypeStruct((t, 3 * D_MODEL), BF16),
        scratch_shapes=[pltpu.VMEM(w.shape, BF16)],
        compiler_params=_params("arbitrary"),
        name="qkv",
    )(x, g, w)


def _expand_rel_bias(rel_ref, bias):
    rel = rel_ref[0]
    r1 = rel.astype(BF16)
    r2 = (rel - r1.astype(F32)).astype(BF16)
    r3 = (rel - r1.astype(F32) - r2.astype(F32)).astype(BF16)
    j = lax.broadcasted_iota(jnp.int32, (REL_PAD, BIAS_WIDTH), 1)
    src = jnp.clip(BAND - 1 - j, -MAX_REL, MAX_REL) + MAX_REL
    sel = jnp.where(lax.broadcasted_iota(jnp.int32, (REL_PAD, BIAS_WIDTH), 0) == src, 1.0, 0.0).astype(BF16)
    u = (_dot(r1, sel) + _dot(r2, sel)) + _dot(r3, sel)
    for h in range(HEADS_PER_GROUP):
        rows = jnp.broadcast_to(u[h:h + 1, :], (CHUNK, BIAS_WIDTH))
        rows = pltpu.roll(rows, BIAS_WIDTH - (CHUNK - 1), 1, stride=1, stride_axis=0)
        bias[h * CHUNK:(h + 1) * CHUNK, :] = rows[:, :BAND]


def _attn_kernel(q_ref, kp_ref, kc_ref, vp_ref, vc_ref, rel_ref, o_ref, kk, vv, bias):
    qb = pl.program_id(2)

    @pl.when(qb == 0)
    def _():
        _expand_rel_bias(rel_ref, bias)

    kk[0:ATTN_QBLOCK, :] = kp_ref[...]
    kk[ATTN_QBLOCK:, :] = kc_ref[...]
    vv[0:ATTN_QBLOCK, :] = vp_ref[...]
    vv[ATTN_QBLOCK:, :] = vc_ref[...]
    lane_head = lax.broadcasted_iota(jnp.int32, (CHUNK, MXU_DIM), 1) // HEAD_DIM
    col = lax.broadcasted_iota(jnp.int32, (1, BAND), 1)

    def chunks(first_block):
        for j in range(ATTN_QROWS // CHUNK):
            qj = q_ref[j * CHUNK:(j + 1) * CHUNK, :]
            lhs = jnp.concatenate(
                [jnp.where(lane_head == h, qj, jnp.zeros_like(qj)) for h in range(HEADS_PER_GROUP)], axis=0)
            kwin = kk[j * CHUNK:j * CHUNK + BAND, :]
            vwin = vv[j * CHUNK:j * CHUNK + BAND, :]
            s = lax.dot_general(lhs, kwin, (((1,), (1,)), ((), ())), preferred_element_type=F32) + bias[...]
            if first_block:
                s = jnp.where(col + j * CHUNK >= ATTN_QBLOCK, s, NEG_BIG)
            m = jnp.max(s, axis=-1, keepdims=True)
            p = jnp.exp(s - m)
            l = jnp.sum(p, axis=-1, keepdims=True)
            o_all = _dot(p.astype(BF16), vwin) * (1.0 / l)
            o = jnp.zeros((CHUNK, MXU_DIM), F32)
            for h in range(HEADS_PER_GROUP):
                o = o + jnp.where(lane_head == h, o_all[h * CHUNK:(h + 1) * CHUNK, :], 0.0)
            o_ref[j * CHUNK:(j + 1) * CHUNK, :] = o.astype(BF16)

    @pl.when(qb == 0)
    def _():
        chunks(True)

    @pl.when(qb > 0)
    def _():
        chunks(False)


def _attention(qkv, rel, batch, seq):
    nqb = seq // ATTN_QROWS
    per = ATTN_QROWS // ATTN_QBLOCK
    kcol = D_MODEL // MXU_DIM
    blk = (ATTN_QROWS, MXU_DIM)
    back = (ATTN_QBLOCK, MXU_DIM)
    cur = lambda off: (lambda b, g, i: (b * nqb + i, off + g))
    prev = lambda off: (lambda b, g, i: ((b * nqb + i) * per - jnp.minimum(i, 1), off + g))
    return pl.pallas_call(
        _attn_kernel,
        grid=(batch, N_HEAD_GROUPS, nqb),
        in_specs=[pl.BlockSpec(blk, cur(0)),
                  pl.BlockSpec(back, prev(kcol)), pl.BlockSpec(blk, cur(kcol)),
                  pl.BlockSpec(back, prev(2 * kcol)), pl.BlockSpec(blk, cur(2 * kcol)),
                  pl.BlockSpec((1, SUBLANES, REL_PAD), lambda b, g, i: (g, 0, 0))],
        out_specs=pl.BlockSpec(blk, cur(0)),
        out_shape=jax.ShapeDtypeStruct((batch * seq, D_MODEL), BF16),
        scratch_shapes=[pltpu.VMEM((ATTN_QBLOCK + ATTN_QROWS, MXU_DIM), BF16),
                        pltpu.VMEM((ATTN_QBLOCK + ATTN_QROWS, MXU_DIM), BF16),
                        pltpu.VMEM((HEADS_PER_GROUP * CHUNK, BAND), F32)],
        compiler_params=_params("parallel", "parallel", "arbitrary"),
        name="attn",
    )(qkv, qkv, qkv, qkv, qkv, rel)


def _rel_bias_rows(rel_bias):
    rel = rel_bias.astype(F32).reshape(N_HEAD_GROUPS, HEADS_PER_GROUP, 2 * MAX_REL + 1)
    return jnp.pad(rel, ((0, 0), (0, SUBLANES - HEADS_PER_GROUP), (0, REL_PAD - (2 * MAX_REL + 1))))


def _router_logits(hn, wr_ref, br_ref, one_matmul=False):
    a_hi = hn.astype(BF16)
    a_lo = (hn - a_hi.astype(F32)).astype(BF16)
    w = wr_ref[...]
    w_hi = w.astype(BF16)
    w_lo = (w - w_hi.astype(F32)).astype(BF16)
    if not one_matmul:
        return _dot(a_hi, w_hi) + (_dot(a_hi, w_lo) + _dot(a_lo, w_hi)) + br_ref[...]
    w2 = jnp.concatenate([jnp.concatenate([w_hi, w_lo], axis=1),
                          jnp.concatenate([w_hi, jnp.zeros_like(w_lo)], axis=1)], axis=0)
    out = _dot(jnp.concatenate([a_hi, a_lo], axis=1), w2)
    return out[:, :ROUTER_LANES] + out[:, ROUTER_LANES:] + br_ref[...]


def _route_group(lg, base):
    rows = lg.shape[0]
    lane = lax.broadcasted_iota(jnp.int32, lg.shape, 1)
    gl = jnp.where(lane < N_EXPERT_GROUPS, lg, np.float32(-np.inf))
    gmax = jnp.max(gl, axis=-1, keepdims=True)
    gidx = jnp.min(jnp.where(gl == gmax, lane, np.int32(ROUTER_LANES)), axis=-1, keepdims=True)
    og = lane == gidx
    tri = (lax.broadcasted_iota(jnp.int32, (rows, rows), 0)
           > lax.broadcasted_iota(jnp.int32, (rows, rows), 1))
    tri = jnp.where(tri, 1.0, 0.0).astype(BF16)
    ogf = jnp.where(og, 1.0, 0.0)
    rank = jnp.sum(jnp.where(og, base + _dot(tri, ogf.astype(BF16)), 0.0), axis=-1, keepdims=True)
    rec = jnp.where(lane == R_GROUP, gidx.astype(F32), jnp.where(lane == R_RANK, rank, 0.0))
    return rec, base + jnp.sum(ogf, axis=0, keepdims=True)


def _route_experts(lg, g):
    lane = lax.broadcasted_iota(jnp.int32, lg.shape, 1)
    ninf = np.float32(-np.inf)
    big = np.int32(ROUTER_LANES)
    gmask = lane < N_EXPERT_GROUPS
    gmax = jnp.max(jnp.where(gmask, lg, ninf), axis=-1, keepdims=True)
    ge = jnp.where(gmask, jnp.exp(lg - gmax), 0.0)
    gprob = jnp.sum(jnp.where(lane == g, ge, 0.0), axis=-1, keepdims=True) / jnp.sum(ge, axis=-1, keepdims=True)
    lo = ROUTER_EXPERT_LANE0 + EXPERTS_PER_GROUP * g
    el = jnp.where((lane >= lo) & (lane < lo + EXPERTS_PER_GROUP), lg, ninf)
    l1 = jnp.max(el, axis=-1, keepdims=True)
    i1 = jnp.min(jnp.where(el == l1, lane, big), axis=-1, keepdims=True)
    el2 = jnp.where(lane == i1, ninf, el)
    l2 = jnp.max(el2, axis=-1, keepdims=True)
    i2 = jnp.min(jnp.where(el2 == l2, lane, big), axis=-1, keepdims=True)
    t = jnp.exp(l2 - l1)
    w1 = gprob / (1.0 + t)
    return i1, w1, i2, w1 * t


def _to_token_tiles(ref, x):
    rows = x.shape[0]
    for j in range(TOKEN_ROWS):
        ref[pl.ds(j, rows, stride=TOKEN_ROWS), :] = x[:, j * LANES:(j + 1) * LANES]


def _from_token_tiles(ref, rows):
    return jnp.concatenate([ref[pl.ds(j, rows, stride=TOKEN_ROWS), :] for j in range(TOKEN_ROWS)], axis=1)


def _post_kernel(glu, y_ref, h_ref, w_ref, g_ref, wr_ref, br_ref, hout_ref, hntt_ref, route_ref, cnt_ref, wb):
    @pl.when(pl.program_id(0) == 0)
    def _():
        cnt_ref[...] = jnp.zeros_like(cnt_ref)

    _cast_once(w_ref, wb)
    y = y_ref[...].astype(BF16)
    if glu:
        mix = _dot(y, wb[:, :D_MODEL]) * _sigmoid(_dot(y, wb[:, D_MODEL:]))
    else:
        mix = _dot(y, wb[...])
    h = h_ref[...] + mix
    hout_ref[...] = h
    hn = _rms(h, g_ref[...])
    _to_token_tiles(hntt_ref, hn)
    route_ref[...], cnt_ref[...] = _route_group(_router_logits(hn, wr_ref, br_ref), cnt_ref[...])


def _post(y, h, w, g, wr, br, glu, tm=POST_ROWS):
    t = h.shape[0]
    row = lambda i: (i, 0)
    fixed = lambda i: (0, 0)
    return pl.pallas_call(
        functools.partial(_post_kernel, glu),
        grid=(t // tm,),
        in_specs=[pl.BlockSpec((tm, D_MODEL), row), pl.BlockSpec((tm, D_MODEL), row),
                  _resident(w.shape), pl.BlockSpec((1, D_MODEL), fixed),
                  pl.BlockSpec((D_MODEL, ROUTER_LANES), fixed), pl.BlockSpec((1, ROUTER_LANES), fixed)],
        out_specs=[pl.BlockSpec((tm, D_MODEL), row), pl.BlockSpec((tm * TOKEN_ROWS, LANES), row),
                   pl.BlockSpec((tm, ROUTER_LANES), row), pl.BlockSpec((1, ROUTER_LANES), fixed)],
        out_shape=[jax.ShapeDtypeStruct((t, D_MODEL), F32), jax.ShapeDtypeStruct((t * TOKEN_ROWS, LANES), F32),
                   jax.ShapeDtypeStruct((t, ROUTER_LANES), F32), jax.ShapeDtypeStruct((1, ROUTER_LANES), F32)],
        scratch_shapes=[pltpu.VMEM(w.shape, BF16)],
        compiler_params=_params("arbitrary"),
        name="post_glu" if glu else "post_attn",
    )(y, h, w, g, wr, br)


def _moe_tiles(t):
    return t // MOE_TILE + N_EXPERT_GROUPS


def _dispatch_plan(route, counts, t):
    g = route[:, R_GROUP].astype(jnp.int32)
    rank = route[:, R_RANK].astype(jnp.int32)
    cnt = counts[0, :N_EXPERT_GROUPS].astype(jnp.int32)
    padded = ((cnt + (MOE_TILE - 1)) // MOE_TILE) * MOE_TILE
    ends = jnp.cumsum(padded)
    off = ends - padded
    onehot = g[:, None] == jnp.arange(N_EXPERT_GROUPS, dtype=jnp.int32)
    pos = jnp.sum(jnp.where(onehot, off, 0), axis=-1) + rank
    ntiles = _moe_tiles(t)
    tile_start = jnp.arange(ntiles, dtype=jnp.int32) * MOE_TILE
    tile_group = jnp.minimum(jnp.sum((tile_start[:, None] >= ends[None, :]).astype(jnp.int32), axis=1),
                             N_EXPERT_GROUPS - 1)
    n_active = (ends[-1] // MOE_TILE).reshape(1)
    plan = jnp.concatenate([off + cnt, padded - cnt, n_active])
    groups = jnp.arange(N_EXPERT_GROUPS, dtype=jnp.int32)
    later = (groups[None, :] > groups[:, None]) & (cnt[None, :] > 0)
    following = jnp.min(jnp.where(later, groups[None, :], N_EXPERT_GROUPS), axis=1)
    following = jnp.where(following == N_EXPERT_GROUPS, -1, following)
    next_group = jnp.sum(jnp.where(tile_group[:, None] == groups[None, :], following[None, :], 0), axis=1)
    return pos, tile_group, n_active, next_group, plan


def _token_tile(ref, i):
    return ref.at[pl.ds(pl.multiple_of(i * TOKEN_ROWS, TOKEN_ROWS), TOKEN_ROWS), :]


def _zero_fill(plan_ref, xs_ref, zbuf, zsem, ntiles):
    nseg = N_EXPERT_GROUPS
    zbuf[...] = jnp.zeros_like(zbuf)
    half = MOE_TILE // 2
    sizes = [1 << b for b in reversed(range(half.bit_length()))]

    def chunk(first, size):
        return pltpu.make_async_copy(zbuf.at[pl.ds(0, size * TOKEN_ROWS), :],
                                     xs_ref.at[pl.ds(pl.multiple_of(first * TOKEN_ROWS, TOKEN_ROWS),
                                                     size * TOKEN_ROWS), :], zsem)

    def pads(wait):
        def body(e, c):
            first = plan_ref[e]
            n = plan_ref[nseg + e]
            for size in sizes:
                hit = (n & size) != 0

                @pl.when(hit)
                def _():
                    cp = chunk(first, size)
                    cp.wait() if wait else cp.start()

                first = first + jnp.where(hit, size, 0)
            return c
        lax.fori_loop(0, nseg, body, 0)

    def tail(wait):
        def body(i, c):
            for k in range(2):
                cp = chunk(i * MOE_TILE + k * half, half)
                cp.wait() if wait else cp.start()
            return c
        lax.fori_loop(plan_ref[2 * nseg], ntiles, body, 0)

    pads(False)
    tail(False)
    pads(True)
    tail(True)


def _dispatch_kernel(ntiles, plan_ref, pos_ref, hn_ref, xs_ref, sem, zbuf, zsem):
    rows = hn_ref.shape[0] // TOKEN_ROWS

    @pl.when(pl.program_id(0) == 0)
    def _():
        _zero_fill(plan_ref, xs_ref, zbuf, zsem, ntiles)

    def copy(r):
        return pltpu.make_async_copy(_token_tile(hn_ref, r), _token_tile(xs_ref, pos_ref[0, 0, r]), sem)

    def issue(r, c):
        copy(2 * r).start(priority=0)
        copy(2 * r + 1).start(priority=1)
        return c

    lax.fori_loop(0, rows // 2, issue, 0, unroll=DMA_ISSUE_UNROLL)
    pltpu.make_async_copy(hn_ref, xs_ref.at[pl.ds(0, rows * TOKEN_ROWS), :], sem).wait()


def _pos_blocks(pos, tm):
    return pos.reshape(pos.shape[0] // tm, 1, tm)


def _dispatch(plan, pos, hn_tt, t, tm=DISPATCH_ROWS):
    ntiles = _moe_tiles(t)
    nrows = ntiles * MOE_TILE * TOKEN_ROWS
    return pl.pallas_call(
        functools.partial(_dispatch_kernel, ntiles),
        grid_spec=pltpu.PrefetchScalarGridSpec(
            num_scalar_prefetch=1,
            grid=(t // tm,),
            in_specs=[pl.BlockSpec((1, 1, tm), lambda i, plan: (i, 0, 0), memory_space=pltpu.SMEM),
                      pl.BlockSpec((tm * TOKEN_ROWS, LANES), lambda i, plan: (i, 0))],
            out_specs=pl.BlockSpec(memory_space=pl.ANY),
            scratch_shapes=[pltpu.SemaphoreType.DMA(()),
                            pltpu.VMEM((MOE_TILE // 2 * TOKEN_ROWS, LANES), F32),
                            pltpu.SemaphoreType.DMA(())]),
        out_shape=jax.ShapeDtypeStruct((nrows, LANES), F32),
        compiler_params=_params("arbitrary"),
        name="moe_dispatch",
    )(plan, _pos_blocks(pos, tm), hn_tt)


def _expert_kernel(layer, tg_ref, na_ref, ng_ref, xs_ref, wr_ref, br_ref, wg_hbm, wu_hbm, wd_hbm, ys_ref,
                   wg_f, wu_f, wd_f, wgu_s, wd_s, sems):
    i = pl.program_id(0)
    g = tg_ref[i]
    new_group = (i == 0) | (tg_ref[jnp.maximum(i - 1, 0)] != g)
    width = EXPERTS_PER_GROUP * D_EXPERT

    def fetch(group):
        return [pltpu.make_async_copy(w.at[layer, group], stage, sems.at[k])
                for k, (w, stage) in enumerate(((wg_hbm, wg_f), (wu_hbm, wu_f), (wd_hbm, wd_f)))]

    @pl.when(i < na_ref[0])
    def _():
        @pl.when(new_group)
        def _():
            @pl.when(i == 0)
            def _():
                for cp in fetch(g):
                    cp.start()

            for cp in fetch(g):
                cp.wait()
            for e in range(EXPERTS_PER_GROUP):
                cols = slice(e * D_EXPERT, (e + 1) * D_EXPERT)
                wgu_s[:, cols] = wg_f[e].astype(BF16)
                wgu_s[:, width + e * D_EXPERT:width + (e + 1) * D_EXPERT] = wu_f[e].astype(BF16)
                wd_s[cols, :] = wd_f[e].astype(BF16)
            nxt = ng_ref[i]

            @pl.when(nxt >= 0)
            def _():
                for cp in fetch(nxt):
                    cp.start()

        x = _from_token_tiles(xs_ref, MOE_TILE)
        i1, w1, i2, w2 = _route_experts(_router_logits(x, wr_ref, br_ref, one_matmul=True), g)
        xb = x.astype(BF16)
        first = ROUTER_EXPERT_LANE0 + g * EXPERTS_PER_GROUP
        hid = []
        for e in range(EXPERTS_PER_GROUP):
            a = _dot(xb, wgu_s[:, e * D_EXPERT:(e + 1) * D_EXPERT])
            u = _dot(xb, wgu_s[:, width + e * D_EXPERT:width + (e + 1) * D_EXPERT])
            gate = jnp.where(i1 == first + e, w1, 0.0) + jnp.where(i2 == first + e, w2, 0.0)
            hid.append(((a * _sigmoid(a)) * u * gate).astype(BF16))
        _to_token_tiles(ys_ref, _dot(jnp.concatenate(hid, axis=1), wd_s[...]))


def _experts(layer, tile_group, n_active, next_group, xs, wr, br, w_gate, w_up, w_down):
    ntiles = tile_group.shape[0]
    last = lambda i, na: jnp.minimum(i, na[0] - 1)
    tile = pl.BlockSpec((MOE_TILE * TOKEN_ROWS, LANES), lambda i, tg, na, ng: (last(i, na), 0))
    fixed = lambda a: pl.BlockSpec(a.shape, lambda i, tg, na, ng: (0, 0))
    grouped = lambda w: w.reshape((w.shape[0], N_EXPERT_GROUPS, EXPERTS_PER_GROUP) + w.shape[2:])
    stage = lambda w: pltpu.VMEM((EXPERTS_PER_GROUP,) + w.shape[2:], F32)
    hbm = pl.BlockSpec(memory_space=pl.ANY)
    width = EXPERTS_PER_GROUP * D_EXPERT
    return pl.pallas_call(
        functools.partial(_expert_kernel, layer),
        grid_spec=pltpu.PrefetchScalarGridSpec(
            num_scalar_prefetch=3,
            grid=(ntiles,),
            in_specs=[tile, fixed(wr), fixed(br), hbm, hbm, hbm],
            out_specs=tile,
            scratch_shapes=[stage(w_gate), stage(w_up), stage(w_down),
                            pltpu.VMEM((D_MODEL, 2 * width), BF16), pltpu.VMEM((width, D_MODEL), BF16),
                            pltpu.SemaphoreType.DMA((3,))]),
        out_shape=jax.ShapeDtypeStruct(xs.shape, F32),
        input_output_aliases={3: 0},
        compiler_params=_params("arbitrary"),
        name="moe_experts",
    )(tile_group, n_active, next_group, xs, wr, br, grouped(w_gate), grouped(w_up), grouped(w_down))


def _combine_kernel(emit_h, pos_ref, nxt_ref, h_ref, gn_ref, ys_ref, *refs):
    outs, (buf0, buf1, sem0, sem1) = refs[:-4], refs[-4:]
    bufs, sems = (buf0, buf1), (sem0, sem1)
    rows = h_ref.shape[0]
    i = pl.program_id(0)

    def gather(idx_ref, slot):
        def copy(r):
            return pltpu.make_async_copy(_token_tile(ys_ref, idx_ref[0, 0, r]), _token_tile(bufs[slot], r),
                                         sems[slot])

        def issue(r, c):
            copy(2 * r).start(priority=0)
            copy(2 * r + 1).start(priority=1)
            return c

        lax.fori_loop(0, rows // 2, issue, 0, unroll=DMA_ISSUE_UNROLL)

    @pl.when(i == 0)
    def _():
        gather(pos_ref, 0)

    for slot in range(2):
        @pl.when(i % 2 == slot)
        def _():
            @pl.when(i + 1 < pl.num_programs(0))
            def _():
                gather(nxt_ref, 1 - slot)

            pltpu.make_async_copy(ys_ref.at[pl.ds(0, rows * TOKEN_ROWS), :], bufs[slot], sems[slot]).wait()
            h = h_ref[...] + _from_token_tiles(bufs[slot], rows)
            if emit_h:
                outs[0][...] = h
            outs[-1][...] = _rms(h, gn_ref[...])


def _combine(pos, h, gnext, ys, emit_h, tm=COMBINE_ROWS):
    t = h.shape[0]
    nblk = t // tm
    row = lambda i: (i, 0)
    blocks = _pos_blocks(pos, tm)
    smem = lambda imap: pl.BlockSpec((1, 1, tm), imap, memory_space=pltpu.SMEM)
    n_out = 2 if emit_h else 1
    out = pl.pallas_call(
        functools.partial(_combine_kernel, emit_h),
        grid=(nblk,),
        in_specs=[smem(lambda i: (i, 0, 0)), smem(lambda i: (jnp.minimum(i + 1, nblk - 1), 0, 0)),
                  pl.BlockSpec((tm, D_MODEL), row),
                  pl.BlockSpec((1, D_MODEL), lambda i: (0, 0)),
                  pl.BlockSpec(memory_space=pl.ANY)],
        out_specs=[pl.BlockSpec((tm, D_MODEL), row)] * n_out,
        out_shape=[jax.ShapeDtypeStruct((t, D_MODEL), F32)] * n_out,
        scratch_shapes=[pltpu.VMEM((tm * TOKEN_ROWS, LANES), F32), pltpu.VMEM((tm * TOKEN_ROWS, LANES), F32),
                        pltpu.SemaphoreType.DMA(()), pltpu.SemaphoreType.DMA(())],
        compiler_params=_params("arbitrary"),
        name="moe_combine",
    )(blocks, blocks, h, gnext, ys)
    return (out[0], out[1]) if emit_h else (None, out[0])


def _moe(layer, h, hn_tt, route, counts, wr, br, w_gate, w_up, w_down, gnext, emit_h):
    t = h.shape[0]
    pos, tile_group, n_active, next_group, plan = _dispatch_plan(route, counts, t)
    xs = _dispatch(plan, pos, hn_tt, t)
    ys = _experts(layer, tile_group, n_active, next_group, xs, wr, br, w_gate, w_up, w_down)
    return _combine(pos, h, gnext, ys, emit_h)


def _dot3(a, b):
    a_hi = a.astype(BF16)
    a_lo = (a - a_hi.astype(F32)).astype(BF16)
    b_hi = b.astype(BF16)
    b_lo = (b - b_hi.astype(F32)).astype(BF16)
    return _dot(a_hi, b_hi) + (_dot(a_hi, b_lo) + _dot(a_lo, b_hi))


def _ssm_powers(lam_re, lam_im, log_step, lags):
    lam_re = jnp.minimum(lam_re, np.float32(-1e-4))
    step = jnp.exp(log_step)
    dec = jnp.exp(lags * (lam_re * step))
    ang = lags * (lam_im * step)
    return dec * jnp.cos(ang), dec * jnp.sin(ang)


def _ssm_operators(lamr_ref, lamc_ref, bre_ref, bim_ref, cre_ref, cim_ref, r_s, wd_s, vd_s):
    L, ns = SSM_CHUNK, SSM_BLOCK_STATE
    lam = lamr_ref[0]
    lre, lim, lst = lam[0:1], lam[1:2], lam[2:3]
    lag_rev = (L - 1 - lax.broadcasted_iota(jnp.int32, (L, 1), 0)).astype(F32)
    pr_re, pr_im = _ssm_powers(lre, lim, lst, lag_rev)
    ab_re, ab_im = _ssm_powers(lre, lim, lst, np.float32(1.0))
    lam_re = jnp.minimum(lre, np.float32(-1e-4))
    denom = lam_re * lam_re + lim * lim
    num_re = ab_re - 1.0
    f_re = (num_re * lam_re + ab_im * lim) / denom
    f_im = (ab_im * lam_re - num_re * lim) / denom
    bre, bim = bre_ref[0], bim_ref[0]
    bb_re = f_re * bre - f_im * bim
    bb_im = f_re * bim + f_im * bre
    cre, cim = cre_ref[0], cim_ref[0]
    for s in range(L):
        a_re, a_im = pr_re[s:s + 1], pr_im[s:s + 1]
        w_re = a_re * bb_re - a_im * bb_im
        w_im = a_re * bb_im + a_im * bb_re
        rows = slice(s * LANES, (s + 1) * LANES)
        wd_s[rows, 0:ns] = w_re.astype(BF16)
        wd_s[rows, ns:2 * ns] = w_im.astype(BF16)
        r_s[rows, :] = (_dot3(w_re, cre) - _dot3(w_im, cim)).astype(BF16)
    r_s[L * LANES:, :] = jnp.zeros((LANES, LANES), BF16)
    lamc = lamc_ref[0]
    lag1 = (1 + lax.broadcasted_iota(jnp.int32, (1, L), 1)).astype(F32)
    p1_re, p1_im = _ssm_powers(lamc[:, 0:1], lamc[:, 1:2], lamc[:, 2:3], lag1)
    for t in range(L):
        a_re, a_im = p1_re[:, t:t + 1], p1_im[:, t:t + 1]
        cols = slice(t * LANES, (t + 1) * LANES)
        vd_s[0:ns, cols] = (cre * a_re - cim * a_im).astype(BF16)
        vd_s[ns:2 * ns, cols] = (-(cre * a_im + cim * a_re)).astype(BF16)
    return _ssm_powers(lre, lim, lst, np.float32(L))


def _ssm_kernel(nch, x_ref, lamr_ref, lamc_ref, bre_ref, bim_ref, cre_ref, cim_ref, d_ref, o_ref,
                r_s, wd_s, vd_s, xcat, st, xprev):
    nb = x_ref.shape[0]
    ns = SSM_BLOCK_STATE
    ar, ai = _ssm_operators(lamr_ref, lamc_ref, bre_ref, bim_ref, cre_ref, cim_ref, r_s, wd_s, vd_s)
    for s in range(SSM_CHUNK):
        for b in range(nb):
            xs = x_ref[b, pl.ds(s, nch, stride=SSM_CHUNK), :]
            xcat[b * nch:(b + 1) * nch, s * LANES:(s + 1) * LANES] = xs.astype(BF16)
    st[...] = _dot(xcat[...], wd_s[...])

    def step(k, carry):
        out = []
        for b in range(nb):
            xr, xi = carry[b]
            row = b * nch + k
            sr = st[pl.ds(row, 1), 0:ns]
            si = st[pl.ds(row, 1), ns:2 * ns]
            st[pl.ds(row, 1), 0:ns] = xr
            st[pl.ds(row, 1), ns:2 * ns] = xi
            out.append((ar * xr - ai * xi + sr, ar * xi + ai * xr + si))
        return tuple(out)

    zero = jnp.zeros((1, ns), F32)
    lax.fori_loop(0, nch, step, tuple((zero, zero) for _ in range(nb)))
    xprev[...] = st[...].astype(BF16)
    dskip = d_ref[...]
    L = SSM_CHUNK
    for t in range(0, L, 2):
        taps = jnp.concatenate([r_s[(L - 1 - t) * LANES:, :], r_s[(L - 2 - t) * LANES:L * LANES, :]], axis=1)
        y2 = (_dot(xcat[:, :(t + 2) * LANES], taps)
              + _dot(xprev[...], vd_s[:, t * LANES:(t + 2) * LANES]))
        for k in range(2):
            y = y2[:, k * LANES:(k + 1) * LANES]
            for b in range(nb):
                u = x_ref[b, pl.ds(t + k, nch, stride=L), :]
                o_ref[b, pl.ds(t + k, nch, stride=L), :] = _gelu_tanh(y[b * nch:(b + 1) * nch, :] + dskip * u)


def _ssm(x3, params, dskip):
    nb, seq, _ = x3.shape
    nch = seq // SSM_CHUNK
    rows = nb * nch
    ns = SSM_BLOCK_STATE
    blk = pl.BlockSpec((nb, seq, LANES), lambda g: (0, 0, g))
    per = lambda shape: pl.BlockSpec((1,) + shape, lambda g: (g, 0, 0))
    return pl.pallas_call(
        functools.partial(_ssm_kernel, nch),
        grid=(SSM_BLOCKS,),
        in_specs=[blk, per((3, ns)), per((ns, 3)), per((LANES, ns)), per((LANES, ns)),
                  per((ns, LANES)), per((ns, LANES)), pl.BlockSpec((1, LANES), lambda g: (0, g))],
        out_specs=blk,
        out_shape=jax.ShapeDtypeStruct(x3.shape, F32),
        scratch_shapes=[pltpu.VMEM(((SSM_CHUNK + 1) * LANES, LANES), BF16),
                        pltpu.VMEM((SSM_CHUNK * LANES, 2 * ns), BF16),
                        pltpu.VMEM((2 * ns, SSM_CHUNK * LANES), BF16),
                        pltpu.VMEM((rows, SSM_CHUNK * LANES), BF16),
                        pltpu.VMEM((rows, 2 * ns), F32),
                        pltpu.VMEM((rows, 2 * ns), BF16)],
        compiler_params=_params("parallel"),
        name="ssm",
    )(x3, *params, dskip)


def _ssm_params(lambda_re, lambda_im, log_step, b_re, b_im, c_re, c_im):
    nb, ng, ns = SSM_BLOCKS, SSM_LANE_GROUPS, SSM_BLOCK_STATE
    lam = jnp.stack([lambda_re.astype(F32), lambda_im.astype(F32),
                     jnp.broadcast_to(log_step.astype(F32)[:, None], lambda_re.shape)])
    lam_rows = jnp.transpose(lam.reshape(3, nb, ns), (1, 0, 2))
    lam_cols = jnp.transpose(lam_rows, (0, 2, 1))
    eye = jnp.eye(ng, dtype=F32)

    def blockdiag(m):
        a, b = m.shape[-2:]
        out = jnp.einsum('ngab,gh->ngahb', m.astype(F32).reshape(nb, ng, a, b), eye)
        return out.reshape(nb, ng * a, ng * b)

    bt = lambda m: blockdiag(jnp.swapaxes(m, -1, -2))
    return lam_rows, lam_cols, bt(b_re), bt(b_im), bt(c_re), bt(c_im)


def _router_tables(w_group, b_group, w_expert, b_expert):
    w = jnp.concatenate([w_group.astype(F32),
                         jnp.transpose(w_expert.astype(F32), (1, 0, 2)).reshape(D_MODEL, N_EXPERTS)], axis=1)
    b = jnp.concatenate([b_group.astype(F32), b_expert.astype(F32).reshape(N_EXPERTS)])
    pad = ROUTER_LANES - w.shape[1]
    return jnp.pad(w, ((0, 0), (0, pad))), jnp.pad(b, (0, pad)).reshape(1, ROUTER_LANES)


def kernel(x, norm_mix, norm_ffn, norm_final, attn_w_qkv, attn_w_o, attn_rel_bias, ssm_lambda_re, ssm_lambda_im, ssm_log_step, ssm_b_re, ssm_b_im, ssm_c_re, ssm_c_im, ssm_d, ssm_w_out, moe_w_group_router, moe_b_group_router, moe_w_expert_router, moe_b_expert_router, moe_w_gate, moe_w_up, moe_w_down):
    batch, seq, d = x.shape
    t = batch * seq
    row = lambda v: v.astype(F32).reshape(1, d)
    h = x.astype(F32).reshape(t, d)

    qkv = _qkv(h, row(norm_mix[0]), attn_w_qkv[0].astype(F32))
    o = _attention(qkv, _rel_bias_rows(attn_rel_bias[0]), batch, seq)
    wr, br = _router_tables(moe_w_group_router[0], moe_b_group_router[0],
                            moe_w_expert_router[0], moe_b_expert_router[0])
    h, hn_tt, route, counts = _post(o, h, attn_w_o[0].astype(F32), row(norm_ffn[0]), wr, br, glu=False)
    h, hn32 = _moe(0, h, hn_tt, route, counts, wr, br, moe_w_gate, moe_w_up, moe_w_down, row(norm_mix[1]),
                   emit_h=True)

    ssm_params = _ssm_params(ssm_lambda_re[0], ssm_lambda_im[0], ssm_log_step[0],
                             ssm_b_re[0], ssm_b_im[0], ssm_c_re[0], ssm_c_im[0])
    y = _ssm(hn32.reshape(batch, seq, d), ssm_params, row(ssm_d[0]))
    wr, br = _router_tables(moe_w_group_router[1], moe_b_group_router[1],
                            moe_w_expert_router[1], moe_b_expert_router[1])
    h, hn_tt, route, counts = _post(y.reshape(t, d), h, ssm_w_out[0].astype(F32), row(norm_ffn[1]), wr, br, glu=True)
    _, out = _moe(1, h, hn_tt, route, counts, wr, br, moe_w_gate, moe_w_up, moe_w_down, row(norm_final),
                 emit_h=False)
    return out.reshape(batch, seq, d).astype(x.dtype)
```

```python
import functools

import jax
import jax.numpy as jnp
import numpy as np
from jax import lax
from jax.experimental import pallas as pl
from jax.experimental.pallas import tpu as pltpu

F32 = jnp.float32
BF16 = jnp.bfloat16

D_MODEL = 1024
CHUNK = 64
LOOKBACK_CHUNKS = 8
BAND = (LOOKBACK_CHUNKS + 1) * CHUNK
N_HEADS = 16
HEAD_DIM = D_MODEL // N_HEADS
MAX_REL = 128
SSM_GROUP = 16
SSM_GROUPS = D_MODEL // SSM_GROUP
SSM_STATE = 64
N_EXPERT_GROUPS = 4
EXPERTS_PER_GROUP = 8
N_EXPERTS = N_EXPERT_GROUPS * EXPERTS_PER_GROUP
D_EXPERT = D_MODEL // 4
RMS_EPS = 1e-6
NEG_BIG = -1e30

LANES = 128
SUBLANES = 8
MXU_DIM = 256
VMEM_LIMIT = 60 * 1024 * 1024

HEADS_PER_GROUP = MXU_DIM // HEAD_DIM
N_HEAD_GROUPS = N_HEADS // HEADS_PER_GROUP
ATTN_QBLOCK = LOOKBACK_CHUNKS * CHUNK
ATTN_QROWS = 4 * ATTN_QBLOCK
REL_PAD = -(-(2 * MAX_REL + 1) // LANES) * LANES
BIAS_WIDTH = -(-(BAND + CHUNK - 1) // LANES) * LANES
ROUTER_LANES = LANES
ROUTER_EXPERT_LANE0 = N_EXPERT_GROUPS
SSM_CHUNK = 16
SSM_LANE_GROUPS = LANES // SSM_GROUP
SSM_BLOCKS = D_MODEL // LANES
SSM_BLOCK_STATE = SSM_LANE_GROUPS * SSM_STATE
TOKEN_ROWS = D_MODEL // LANES
MOE_TILE = 256
R_GROUP, R_RANK = range(2)
QKV_ROWS = 512
POST_ROWS = 512
DISPATCH_ROWS = 2048
COMBINE_ROWS = 512
DMA_ISSUE_UNROLL = 4


def _dot(a, b):
    return jnp.dot(a, b, preferred_element_type=F32)


def _rms(x, g):
    return x * lax.rsqrt(jnp.mean(x * x, axis=-1, keepdims=True) + RMS_EPS) * g


def _sigmoid(x):
    return 1.0 / (1.0 + jnp.exp(-x))


def _gelu_tanh(x):
    c = np.float32(np.sqrt(2.0 / np.pi))
    return 0.5 * x * (1.0 + jnp.tanh(c * (x + np.float32(0.044715) * (x * x * x))))


def _params(*sem):
    return pltpu.CompilerParams(dimension_semantics=sem, vmem_limit_bytes=VMEM_LIMIT)


def _resident(shape):
    return pl.BlockSpec(shape, lambda *_: (0,) * len(shape), pipeline_mode=pl.Buffered(1))


def _cast_once(w_ref, wb):
    @pl.when(pl.program_id(0) == 0)
    def _():
        wb[...] = w_ref[...].astype(BF16)


def _qkv_kernel(x_ref, g_ref, w_ref, o_ref, wb):
    _cast_once(w_ref, wb)
    xn = _rms(x_ref[...], g_ref[...]).astype(BF16)
    for c in range(3):
        acc = _dot(xn, wb[:, c * D_MODEL:(c + 1) * D_MODEL])
        if c == 0:
            acc = acc * np.float32(HEAD_DIM ** -0.5)
        o_ref[:, c * D_MODEL:(c + 1) * D_MODEL] = acc.astype(BF16)


def _qkv(x, g, w, tm=QKV_ROWS):
    t = x.shape[0]
    return pl.pallas_call(
        _qkv_kernel,
        grid=(t // tm,),
        in_specs=[pl.BlockSpec((tm, D_MODEL), lambda i: (i, 0)),
                  pl.BlockSpec((1, D_MODEL), lambda i: (0, 0)),
                  _resident(w.shape)],
        out_specs=pl.BlockSpec((tm, 3 * D_MODEL), lambda i: (i, 0)),
        out_shape=jax.ShapeDtypeStruct((t, 3 * D_MODEL), BF16),
        scratch_shapes=[pltpu.VMEM(w.shape, BF16)],
        compiler_params=_params("arbitrary"),
        name="qkv",
    )(x, g, w)


def _expand_rel_bias(rel_ref, bias):
    rel = rel_ref[0]
    r1 = rel.astype(BF16)
    r2 = (rel - r1.astype(F32)).astype(BF16)
    r3 = (rel - r1.astype(F32) - r2.astype(F32)).astype(BF16)
    j = lax.broadcasted_iota(jnp.int32, (REL_PAD, BIAS_WIDTH), 1)
    src = jnp.clip(BAND - 1 - j, -MAX_REL, MAX_REL) + MAX_REL
    sel = jnp.where(lax.broadcasted_iota(jnp.int32, (REL_PAD, BIAS_WIDTH), 0) == src, 1.0, 0.0).astype(BF16)
    u = (_dot(r1, sel) + _dot(r2, sel)) + _dot(r3, sel)
    for h in range(HEADS_PER_GROUP):
        rows = jnp.broadcast_to(u[h:h + 1, :], (CHUNK, BIAS_WIDTH))
        rows = pltpu.roll(rows, BIAS_WIDTH - (CHUNK - 1), 1, stride=1, stride_axis=0)
        bias[h * CHUNK:(h + 1) * CHUNK, :] = rows[:, :BAND]


def _attn_kernel(q_ref, kp_ref, kc_ref, vp_ref, vc_ref, rel_ref, o_ref, kk, vv, bias):
    qb = pl.program_id(2)

    @pl.when(qb == 0)
    def _():
        _expand_rel_bias(rel_ref, bias)

    kk[0:ATTN_QBLOCK, :] = kp_ref[...]
    kk[ATTN_QBLOCK:, :] = kc_ref[...]
    vv[0:ATTN_QBLOCK, :] = vp_ref[...]
    vv[ATTN_QBLOCK:, :] = vc_ref[...]
    lane_head = lax.broadcasted_iota(jnp.int32, (CHUNK, MXU_DIM), 1) // HEAD_DIM
    col = lax.broadcasted_iota(jnp.int32, (1, BAND), 1)

    def chunks(first_block):
        for j in range(ATTN_QROWS // CHUNK):
            qj = q_ref[j * CHUNK:(j + 1) * CHUNK, :]
            lhs = jnp.concatenate(
                [jnp.where(lane_head == h, qj, jnp.zeros_like(qj)) for h in range(HEADS_PER_GROUP)], axis=0)
            kwin = kk[j * CHUNK:j * CHUNK + BAND, :]
            vwin = vv[j * CHUNK:j * CHUNK + BAND, :]
            s = lax.dot_general(lhs, kwin, (((1,), (1,)), ((), ())), preferred_element_type=F32) + bias[...]
            if first_block:
                s = jnp.where(col + j * CHUNK >= ATTN_QBLOCK, s, NEG_BIG)
            m = jnp.max(s, axis=-1, keepdims=True)
            p = jnp.exp(s - m)
            l = jnp.sum(p, axis=-1, keepdims=True)
            o_all = _dot(p.astype(BF16), vwin) * (1.0 / l)
            o = jnp.zeros((CHUNK, MXU_DIM), F32)
            for h in range(HEADS_PER_GROUP):
                o = o + jnp.where(lane_head == h, o_all[h * CHUNK:(h + 1) * CHUNK, :], 0.0)
            o_ref[j * CHUNK:(j + 1) * CHUNK, :] = o.astype(BF16)

    @pl.when(qb == 0)
    def _():
        chunks(True)

    @pl.when(qb > 0)
    def _():
        chunks(False)


def _attention(qkv, rel, batch, seq):
    nqb = seq // ATTN_QROWS
    per = ATTN_QROWS // ATTN_QBLOCK
    kcol = D_MODEL // MXU_DIM
    blk = (ATTN_QROWS, MXU_DIM)
    back = (ATTN_QBLOCK, MXU_DIM)
    cur = lambda off: (lambda b, g, i: (b * nqb + i, off + g))
    prev = lambda off: (lambda b, g, i: ((b * nqb + i) * per - jnp.minimum(i, 1), off + g))
    return pl.pallas_call(
        _attn_kernel,
        grid=(batch, N_HEAD_GROUPS, nqb),
        in_specs=[pl.BlockSpec(blk, cur(0)),
                  pl.BlockSpec(back, prev(kcol)), pl.BlockSpec(blk, cur(kcol)),
                  pl.BlockSpec(back, prev(2 * kcol)), pl.BlockSpec(blk, cur(2 * kcol)),
                  pl.BlockSpec((1, SUBLANES, REL_PAD), lambda b, g, i: (g, 0, 0))],
        out_specs=pl.BlockSpec(blk, cur(0)),
        out_shape=jax.ShapeDtypeStruct((batch * seq, D_MODEL), BF16),
        scratch_shapes=[pltpu.VMEM((ATTN_QBLOCK + ATTN_QROWS, MXU_DIM), BF16),
                        pltpu.VMEM((ATTN_QBLOCK + ATTN_QROWS, MXU_DIM), BF16),
                        pltpu.VMEM((HEADS_PER_GROUP * CHUNK, BAND), F32)],
        compiler_params=_params("parallel", "parallel", "arbitrary"),
        name="attn",
    )(qkv, qkv, qkv, qkv, qkv, rel)


def _rel_bias_rows(rel_bias):
    rel = rel_bias.astype(F32).reshape(N_HEAD_GROUPS, HEADS_PER_GROUP, 2 * MAX_REL + 1)
    return jnp.pad(rel, ((0, 0), (0, SUBLANES - HEADS_PER_GROUP), (0, REL_PAD - (2 * MAX_REL + 1))))


def _router_logits(hn, wr_ref, br_ref, one_matmul=False):
    a_hi = hn.astype(BF16)
    a_lo = (hn - a_hi.astype(F32)).astype(BF16)
    w = wr_ref[...]
    w_hi = w.astype(BF16)
    w_lo = (w - w_hi.astype(F32)).astype(BF16)
    if not one_matmul:
        return _dot(a_hi, w_hi) + (_dot(a_hi, w_lo) + _dot(a_lo, w_hi)) + br_ref[...]
    w2 = jnp.concatenate([jnp.concatenate([w_hi, w_lo], axis=1),
                          jnp.concatenate([w_hi, jnp.zeros_like(w_lo)], axis=1)], axis=0)
    out = _dot(jnp.concatenate([a_hi, a_lo], axis=1), w2)
    return out[:, :ROUTER_LANES] + out[:, ROUTER_LANES:] + br_ref[...]


def _route_group(lg, base):
    rows = lg.shape[0]
    lane = lax.broadcasted_iota(jnp.int32, lg.shape, 1)
    gl = jnp.where(lane < N_EXPERT_GROUPS, lg, np.float32(-np.inf))
    gmax = jnp.max(gl, axis=-1, keepdims=True)
    gidx = jnp.min(jnp.where(gl == gmax, lane, np.int32(ROUTER_LANES)), axis=-1, keepdims=True)
    og = lane == gidx
    tri = (lax.broadcasted_iota(jnp.int32, (rows, rows), 0)
           > lax.broadcasted_iota(jnp.int32, (rows, rows), 1))
    tri = jnp.where(tri, 1.0, 0.0).astype(BF16)
    ogf = jnp.where(og, 1.0, 0.0)
    rank = jnp.sum(jnp.where(og, base + _dot(tri, ogf.astype(BF16)), 0.0), axis=-1, keepdims=True)
    rec = jnp.where(lane == R_GROUP, gidx.astype(F32), jnp.where(lane == R_RANK, rank, 0.0))
    return rec, base + jnp.sum(ogf, axis=0, keepdims=True)


def _route_experts(lg, g):
    lane = lax.broadcasted_iota(jnp.int32, lg.shape, 1)
    ninf = np.float32(-np.inf)
    big = np.int32(ROUTER_LANES)
    gmask = lane < N_EXPERT_GROUPS
    gmax = jnp.max(jnp.where(gmask, lg, ninf), axis=-1, keepdims=True)
    ge = jnp.where(gmask, jnp.exp(lg - gmax), 0.0)
    gprob = jnp.sum(jnp.where(lane == g, ge, 0.0), axis=-1, keepdims=True) / jnp.sum(ge, axis=-1, keepdims=True)
    lo = ROUTER_EXPERT_LANE0 + EXPERTS_PER_GROUP * g
    el = jnp.where((lane >= lo) & (lane < lo + EXPERTS_PER_GROUP), lg, ninf)
    l1 = jnp.max(el, axis=-1, keepdims=True)
    i1 = jnp.min(jnp.where(el == l1, lane, big), axis=-1, keepdims=True)
    el2 = jnp.where(lane == i1, ninf, el)
    l2 = jnp.max(el2, axis=-1, keepdims=True)
    i2 = jnp.min(jnp.where(el2 == l2, lane, big), axis=-1, keepdims=True)
    t = jnp.exp(l2 - l1)
    w1 = gprob / (1.0 + t)
    return i1, w1, i2, w1 * t


def _to_token_tiles(ref, x):
    rows = x.shape[0]
    for j in range(TOKEN_ROWS):
        ref[pl.ds(j, rows, stride=TOKEN_ROWS), :] = x[:, j * LANES:(j + 1) * LANES]


def _from_token_tiles(ref, rows):
    return jnp.concatenate([ref[pl.ds(j, rows, stride=TOKEN_ROWS), :] for j in range(TOKEN_ROWS)], axis=1)


def _post_kernel(glu, y_ref, h_ref, w_ref, g_ref, wr_ref, br_ref, hout_ref, hntt_ref, route_ref, cnt_ref, wb):
    @pl.when(pl.program_id(0) == 0)
    def _():
        cnt_ref[...] = jnp.zeros_like(cnt_ref)

    _cast_once(w_ref, wb)
    y = y_ref[...].astype(BF16)
    if glu:
        mix = _dot(y, wb[:, :D_MODEL]) * _sigmoid(_dot(y, wb[:, D_MODEL:]))
    else:
        mix = _dot(y, wb[...])
    h = h_ref[...] + mix
    hout_ref[...] = h
    hn = _rms(h, g_ref[...])
    _to_token_tiles(hntt_ref, hn)
    route_ref[...], cnt_ref[...] = _route_group(_router_logits(hn, wr_ref, br_ref), cnt_ref[...])


def _post(y, h, w, g, wr, br, glu, tm=POST_ROWS):
    t = h.shape[0]
    row = lambda i: (i, 0)
    fixed = lambda i: (0, 0)
    return pl.pallas_call(
        functools.partial(_post_kernel, glu),
        grid=(t // tm,),
        in_specs=[pl.BlockSpec((tm, D_MODEL), row), pl.BlockSpec((tm, D_MODEL), row),
                  _resident(w.shape), pl.BlockSpec((1, D_MODEL), fixed),
                  pl.BlockSpec((D_MODEL, ROUTER_LANES), fixed), pl.BlockSpec((1, ROUTER_LANES), fixed)],
        out_specs=[pl.BlockSpec((tm, D_MODEL), row), pl.BlockSpec((tm * TOKEN_ROWS, LANES), row),
                   pl.BlockSpec((tm, ROUTER_LANES), row), pl.BlockSpec((1, ROUTER_LANES), fixed)],
        out_shape=[jax.ShapeDtypeStruct((t, D_MODEL), F32), jax.ShapeDtypeStruct((t * TOKEN_ROWS, LANES), F32),
                   jax.ShapeDtypeStruct((t, ROUTER_LANES), F32), jax.ShapeDtypeStruct((1, ROUTER_LANES), F32)],
        scratch_shapes=[pltpu.VMEM(w.shape, BF16)],
        compiler_params=_params("arbitrary"),
        name="post_glu" if glu else "post_attn",
    )(y, h, w, g, wr, br)


def _moe_tiles(t):
    return t // MOE_TILE + N_EXPERT_GROUPS


def _dispatch_plan(route, counts, t):
    g = route[:, R_GROUP].astype(jnp.int32)
    rank = route[:, R_RANK].astype(jnp.int32)
    cnt = counts[0, :N_EXPERT_GROUPS].astype(jnp.int32)
    padded = ((cnt + (MOE_TILE - 1)) // MOE_TILE) * MOE_TILE
    ends = jnp.cumsum(padded)
    off = ends - padded
    onehot = g[:, None] == jnp.arange(N_EXPERT_GROUPS, dtype=jnp.int32)
    pos = jnp.sum(jnp.where(onehot, off, 0), axis=-1) + rank
    ntiles = _moe_tiles(t)
    tile_start = jnp.arange(ntiles, dtype=jnp.int32) * MOE_TILE
    tile_group = jnp.minimum(jnp.sum((tile_start[:, None] >= ends[None, :]).astype(jnp.int32), axis=1),
                             N_EXPERT_GROUPS - 1)
    n_active = (ends[-1] // MOE_TILE).reshape(1)
    plan = jnp.concatenate([off + cnt, padded - cnt, n_active])
    groups = jnp.arange(N_EXPERT_GROUPS, dtype=jnp.int32)
    later = (groups[None, :] > groups[:, None]) & (cnt[None, :] > 0)
    following = jnp.min(jnp.where(later, groups[None, :], N_EXPERT_GROUPS), axis=1)
    following = jnp.where(following == N_EXPERT_GROUPS, -1, following)
    next_group = jnp.sum(jnp.where(tile_group[:, None] == groups[None, :], following[None, :], 0), axis=1)
    return pos, tile_group, n_active, next_group, plan


def _token_tile(ref, i):
    return ref.at[pl.ds(pl.multiple_of(i * TOKEN_ROWS, TOKEN_ROWS), TOKEN_ROWS), :]


def _zero_fill(plan_ref, xs_ref, zbuf, zsem, ntiles):
    nseg = N_EXPERT_GROUPS
    zbuf[...] = jnp.zeros_like(zbuf)
    half = MOE_TILE // 2
    sizes = [1 << b for b in reversed(range(half.bit_length()))]

    def chunk(first, size):
        return pltpu.make_async_copy(zbuf.at[pl.ds(0, size * TOKEN_ROWS), :],
                                     xs_ref.at[pl.ds(pl.multiple_of(first * TOKEN_ROWS, TOKEN_ROWS),
                                                     size * TOKEN_ROWS), :], zsem)

    def pads(wait):
        def body(e, c):
            first = plan_ref[e]
            n = plan_ref[nseg + e]
            for size in sizes:
                hit = (n & size) != 0

                @pl.when(hit)
                def _():
                    cp = chunk(first, size)
                    cp.wait() if wait else cp.start()

                first = first + jnp.where(hit, size, 0)
            return c
        lax.fori_loop(0, nseg, body, 0)

    def tail(wait):
        def body(i, c):
            for k in range(2):
                cp = chunk(i * MOE_TILE + k * half, half)
                cp.wait() if wait else cp.start()
            return c
        lax.fori_loop(plan_ref[2 * nseg], ntiles, body, 0)

    pads(False)
    tail(False)
    pads(True)
    tail(True)


def _dispatch_kernel(ntiles, plan_ref, pos_ref, hn_ref, xs_ref, sem, zbuf, zsem):
    rows = hn_ref.shape[0] // TOKEN_ROWS

    @pl.when(pl.program_id(0) == 0)
    def _():
        _zero_fill(plan_ref, xs_ref, zbuf, zsem, ntiles)

    def copy(r):
        return pltpu.make_async_copy(_token_tile(hn_ref, r), _token_tile(xs_ref, pos_ref[0, 0, r]), sem)

    def issue(r, c):
        copy(2 * r).start(priority=0)
        copy(2 * r + 1).start(priority=1)
        return c

    lax.fori_loop(0, rows // 2, issue, 0, unroll=DMA_ISSUE_UNROLL)
    pltpu.make_async_copy(hn_ref, xs_ref.at[pl.ds(0, rows * TOKEN_ROWS), :], sem).wait()


def _pos_blocks(pos, tm):
    return pos.reshape(pos.shape[0] // tm, 1, tm)


def _dispatch(plan, pos, hn_tt, t, tm=DISPATCH_ROWS):
    ntiles = _moe_tiles(t)
    nrows = ntiles * MOE_TILE * TOKEN_ROWS
    return pl.pallas_call(
        functools.partial(_dispatch_kernel, ntiles),
        grid_spec=pltpu.PrefetchScalarGridSpec(
            num_scalar_prefetch=1,
            grid=(t // tm,),
            in_specs=[pl.BlockSpec((1, 1, tm), lambda i, plan: (i, 0, 0), memory_space=pltpu.SMEM),
                      pl.BlockSpec((tm * TOKEN_ROWS, LANES), lambda i, plan: (i, 0))],
            out_specs=pl.BlockSpec(memory_space=pl.ANY),
            scratch_shapes=[pltpu.SemaphoreType.DMA(()),
                            pltpu.VMEM((MOE_TILE // 2 * TOKEN_ROWS, LANES), F32),
                            pltpu.SemaphoreType.DMA(())]),
        out_shape=jax.ShapeDtypeStruct((nrows, LANES), F32),
        compiler_params=_params("arbitrary"),
        name="moe_dispatch",
    )(plan, _pos_blocks(pos, tm), hn_tt)


def _expert_kernel(layer, tg_ref, na_ref, ng_ref, xs_ref, wr_ref, br_ref, wg_hbm, wu_hbm, wd_hbm, ys_ref,
                   wg_f, wu_f, wd_f, wgu_s, wd_s, sems):
    i = pl.program_id(0)
    g = tg_ref[i]
    new_group = (i == 0) | (tg_ref[jnp.maximum(i - 1, 0)] != g)
    width = EXPERTS_PER_GROUP * D_EXPERT

    def fetch(group):
        return [pltpu.make_async_copy(w.at[layer, group], stage, sems.at[k])
                for k, (w, stage) in enumerate(((wg_hbm, wg_f), (wu_hbm, wu_f), (wd_hbm, wd_f)))]

    @pl.when(i < na_ref[0])
    def _():
        @pl.when(new_group)
        def _():
            @pl.when(i == 0)
            def _():
                for cp in fetch(g):
                    cp.start()

            for cp in fetch(g):
                cp.wait()
            for e in range(EXPERTS_PER_GROUP):
                cols = slice(e * D_EXPERT, (e + 1) * D_EXPERT)
                wgu_s[:, cols] = wg_f[e].astype(BF16)
                wgu_s[:, width + e * D_EXPERT:width + (e + 1) * D_EXPERT] = wu_f[e].astype(BF16)
                wd_s[cols, :] = wd_f[e].astype(BF16)
            nxt = ng_ref[i]

            @pl.when(nxt >= 0)
            def _():
                for cp in fetch(nxt):
                    cp.start()

        x = _from_token_tiles(xs_ref, MOE_TILE)
        i1, w1, i2, w2 = _route_experts(_router_logits(x, wr_ref, br_ref, one_matmul=True), g)
        xb = x.astype(BF16)
        first = ROUTER_EXPERT_LANE0 + g * EXPERTS_PER_GROUP
        hid = []
        for e in range(EXPERTS_PER_GROUP):
            a = _dot(xb, wgu_s[:, e * D_EXPERT:(e + 1) * D_EXPERT])
            u = _dot(xb, wgu_s[:, width + e * D_EXPERT:width + (e + 1) * D_EXPERT])
            gate = jnp.where(i1 == first + e, w1, 0.0) + jnp.where(i2 == first + e, w2, 0.0)
            hid.append(((a * _sigmoid(a)) * u * gate).astype(BF16))
        _to_token_tiles(ys_ref, _dot(jnp.concatenate(hid, axis=1), wd_s[...]))


def _experts(layer, tile_group, n_active, next_group, xs, wr, br, w_gate, w_up, w_down):
    ntiles = tile_group.shape[0]
    last = lambda i, na: jnp.minimum(i, na[0] - 1)
    tile = pl.BlockSpec((MOE_TILE * TOKEN_ROWS, LANES), lambda i, tg, na, ng: (last(i, na), 0))
    fixed = lambda a: pl.BlockSpec(a.shape, lambda i, tg, na, ng: (0, 0))
    grouped = lambda w: w.reshape((w.shape[0], N_EXPERT_GROUPS, EXPERTS_PER_GROUP) + w.shape[2:])
    stage = lambda w: pltpu.VMEM((EXPERTS_PER_GROUP,) + w.shape[2:], F32)
    hbm = pl.BlockSpec(memory_space=pl.ANY)
    width = EXPERTS_PER_GROUP * D_EXPERT
    return pl.pallas_call(
        functools.partial(_expert_kernel, layer),
        grid_spec=pltpu.PrefetchScalarGridSpec(
            num_scalar_prefetch=3,
            grid=(ntiles,),
            in_specs=[tile, fixed(wr), fixed(br), hbm, hbm, hbm],
            out_specs=tile,
            scratch_shapes=[stage(w_gate), stage(w_up), stage(w_down),
                            pltpu.VMEM((D_MODEL, 2 * width), BF16), pltpu.VMEM((width, D_MODEL), BF16),
                            pltpu.SemaphoreType.DMA((3,))]),
        out_shape=jax.ShapeDtypeStruct(xs.shape, F32),
        input_output_aliases={3: 0},
        compiler_params=_params("arbitrary"),
        name="moe_experts",
    )(tile_group, n_active, next_group, xs, wr, br, grouped(w_gate), grouped(w_up), grouped(w_down))


def _combine_kernel(emit_h, pos_ref, nxt_ref, h_ref, gn_ref, ys_ref, *refs):
    outs, (buf0, buf1, sem0, sem1) = refs[:-4], refs[-4:]
    bufs, sems = (buf0, buf1), (sem0, sem1)
    rows = h_ref.shape[0]
    i = pl.program_id(0)

    def gather(idx_ref, slot):
        def copy(r):
            return pltpu.make_async_copy(_token_tile(ys_ref, idx_ref[0, 0, r]), _token_tile(bufs[slot], r),
                                         sems[slot])

        def issue(r, c):
            copy(2 * r).start(priority=0)
            copy(2 * r + 1).start(priority=1)
            return c

        lax.fori_loop(0, rows // 2, issue, 0, unroll=DMA_ISSUE_UNROLL)

    @pl.when(i == 0)
    def _():
        gather(pos_ref, 0)

    for slot in range(2):
        @pl.when(i % 2 == slot)
        def _():
            @pl.when(i + 1 < pl.num_programs(0))
            def _():
                gather(nxt_ref, 1 - slot)

            pltpu.make_async_copy(ys_ref.at[pl.ds(0, rows * TOKEN_ROWS), :], bufs[slot], sems[slot]).wait()
            h = h_ref[...] + _from_token_tiles(bufs[slot], rows)
            if emit_h:
                outs[0][...] = h
            outs[-1][...] = _rms(h, gn_ref[...])


def _combine(pos, h, gnext, ys, emit_h, tm=COMBINE_ROWS):
    t = h.shape[0]
    nblk = t // tm
    row = lambda i: (i, 0)
    blocks = _pos_blocks(pos, tm)
    smem = lambda imap: pl.BlockSpec((1, 1, tm), imap, memory_space=pltpu.SMEM)
    n_out = 2 if emit_h else 1
    out = pl.pallas_call(
        functools.partial(_combine_kernel, emit_h),
        grid=(nblk,),
        in_specs=[smem(lambda i: (i, 0, 0)), smem(lambda i: (jnp.minimum(i + 1, nblk - 1), 0, 0)),
                  pl.BlockSpec((tm, D_MODEL), row),
                  pl.BlockSpec((1, D_MODEL), lambda i: (0, 0)),
                  pl.BlockSpec(memory_space=pl.ANY)],
        out_specs=[pl.BlockSpec((tm, D_MODEL), row)] * n_out,
        out_shape=[jax.ShapeDtypeStruct((t, D_MODEL), F32)] * n_out,
        scratch_shapes=[pltpu.VMEM((tm * TOKEN_ROWS, LANES), F32), pltpu.VMEM((tm * TOKEN_ROWS, LANES), F32),
                        pltpu.SemaphoreType.DMA(()), pltpu.SemaphoreType.DMA(())],
        compiler_params=_params("arbitrary"),
        name="moe_combine",
    )(blocks, blocks, h, gnext, ys)
    return (out[0], out[1]) if emit_h else (None, out[0])


def _moe(layer, h, hn_tt, route, counts, wr, br, w_gate, w_up, w_down, gnext, emit_h):
    t = h.shape[0]
    pos, tile_group, n_active, next_group, plan = _dispatch_plan(route, counts, t)
    xs = _dispatch(plan, pos, hn_tt, t)
    ys = _experts(layer, tile_group, n_active, next_group, xs, wr, br, w_gate, w_up, w_down)
    return _combine(pos, h, gnext, ys, emit_h)


def _dot3(a, b):
    a_hi = a.astype(BF16)
    a_lo = (a - a_hi.astype(F32)).astype(BF16)
    b_hi = b.astype(BF16)
    b_lo = (b - b_hi.astype(F32)).astype(BF16)
    return _dot(a_hi, b_hi) + (_dot(a_hi, b_lo) + _dot(a_lo, b_hi))


def _ssm_powers(lam_re, lam_im, log_step, lags):
    lam_re = jnp.minimum(lam_re, np.float32(-1e-4))
    step = jnp.exp(log_step)
    dec = jnp.exp(lags * (lam_re * step))
    ang = lags * (lam_im * step)
    return dec * jnp.cos(ang), dec * jnp.sin(ang)


def _ssm_operators(lamr_ref, lamc_ref, bre_ref, bim_ref, cre_ref, cim_ref, r_s, wd_s, vd_s):
    L, ns = SSM_CHUNK, SSM_BLOCK_STATE
    lam = lamr_ref[0]
    lre, lim, lst = lam[0:1], lam[1:2], lam[2:3]
    lag_rev = (L - 1 - lax.broadcasted_iota(jnp.int32, (L, 1), 0)).astype(F32)
    pr_re, pr_im = _ssm_powers(lre, lim, lst, lag_rev)
    ab_re, ab_im = _ssm_powers(lre, lim, lst, np.float32(1.0))
    lam_re = jnp.minimum(lre, np.float32(-1e-4))
    denom = lam_re * lam_re + lim * lim
    num_re = ab_re - 1.0
    f_re = (num_re * lam_re + ab_im * lim) / denom
    f_im = (ab_im * lam_re - num_re * lim) / denom
    bre, bim = bre_ref[0], bim_ref[0]
    bb_re = f_re * bre - f_im * bim
    bb_im = f_re * bim + f_im * bre
    cre, cim = cre_ref[0], cim_ref[0]
    for s in range(L):
        a_re, a_im = pr_re[s:s + 1], pr_im[s:s + 1]
        w_re = a_re * bb_re - a_im * bb_im
        w_im = a_re * bb_im + a_im * bb_re
        rows = slice(s * LANES, (s + 1) * LANES)
        wd_s[rows, 0:ns] = w_re.astype(BF16)
        wd_s[rows, ns:2 * ns] = w_im.astype(BF16)
        r_s[rows, :] = (_dot3(w_re, cre) - _dot3(w_im, cim)).astype(BF16)
    r_s[L * LANES:, :] = jnp.zeros((LANES, LANES), BF16)
    lamc = lamc_ref[0]
    lag1 = (1 + lax.broadcasted_iota(jnp.int32, (1, L), 1)).astype(F32)
    p1_re, p1_im = _ssm_powers(lamc[:, 0:1], lamc[:, 1:2], lamc[:, 2:3], lag1)
    for t in range(L):
        a_re, a_im = p1_re[:, t:t + 1], p1_im[:, t:t + 1]
        cols = slice(t * LANES, (t + 1) * LANES)
        vd_s[0:ns, cols] = (cre * a_re - cim * a_im).astype(BF16)
        vd_s[ns:2 * ns, cols] = (-(cre * a_im + cim * a_re)).astype(BF16)
    return _ssm_powers(lre, lim, lst, np.float32(L))


def _ssm_kernel(nch, x_ref, lamr_ref, lamc_ref, bre_ref, bim_ref, cre_ref, cim_ref, d_ref, o_ref,
                r_s, wd_s, vd_s, xcat, st, xprev):
    nb = x_ref.shape[0]
    ns = SSM_BLOCK_STATE
    ar, ai = _ssm_operators(lamr_ref, lamc_ref, bre_ref, bim_ref, cre_ref, cim_ref, r_s, wd_s, vd_s)
    for s in range(SSM_CHUNK):
        for b in range(nb):
            xs = x_ref[b, pl.ds(s, nch, stride=SSM_CHUNK), :]
            xcat[b * nch:(b + 1) * nch, s * LANES:(s + 1) * LANES] = xs.astype(BF16)
    nslab = st.shape[0]
    half = nslab // 2
    for b in range(nb):
        local = _dot(xcat[b * nch:(b + 1) * nch, :], wd_s[...])
        for j in range(nslab):
            st[j, pl.ds(b, nch, stride=nb), :] = local[:, j * LANES:(j + 1) * LANES]

    def step(k, carry):
        rows = pl.ds(pl.multiple_of(k * nb, nb), nb)
        local = [st[j, rows, :] for j in range(nslab)]
        for j in range(nslab):
            st[j, rows, :] = carry[j]
        new_re, new_im = [], []
        for j in range(half):
            xr, xi = carry[j], carry[half + j]
            a_r, a_i = ar[:, j * LANES:(j + 1) * LANES], ai[:, j * LANES:(j + 1) * LANES]
            new_re.append(a_r * xr - a_i * xi + local[j])
            new_im.append(a_r * xi + a_i * xr + local[half + j])
        return tuple(new_re + new_im)

    lax.fori_loop(0, nch, step, tuple(jnp.zeros((nb, LANES), F32) for _ in range(nslab)))
    for b in range(nb):
        for j in range(nslab):
            xprev[b * nch:(b + 1) * nch, j * LANES:(j + 1) * LANES] = (
                st[j, pl.ds(b, nch, stride=nb), :].astype(BF16))
    dskip = d_ref[...]
    L = SSM_CHUNK
    for t in range(0, L, 2):
        taps = jnp.concatenate([r_s[(L - 1 - t) * LANES:, :], r_s[(L - 2 - t) * LANES:L * LANES, :]], axis=1)
        y2 = (_dot(xcat[:, :(t + 2) * LANES], taps)
              + _dot(xprev[...], vd_s[:, t * LANES:(t + 2) * LANES]))
        for k in range(2):
            y = y2[:, k * LANES:(k + 1) * LANES]
            for b in range(nb):
                u = x_ref[b, pl.ds(t + k, nch, stride=L), :]
                o_ref[b, pl.ds(t + k, nch, stride=L), :] = _gelu_tanh(y[b * nch:(b + 1) * nch, :] + dskip * u)


def _ssm(x3, params, dskip):
    nb, seq, _ = x3.shape
    nch = seq // SSM_CHUNK
    rows = nb * nch
    ns = SSM_BLOCK_STATE
    blk = pl.BlockSpec((nb, seq, LANES), lambda g: (0, 0, g))
    per = lambda shape: pl.BlockSpec((1,) + shape, lambda g: (g, 0, 0))
    return pl.pallas_call(
        functools.partial(_ssm_kernel, nch),
        grid=(SSM_BLOCKS,),
        in_specs=[blk, per((3, ns)), per((ns, 3)), per((LANES, ns)), per((LANES, ns)),
                  per((ns, LANES)), per((ns, LANES)), pl.BlockSpec((1, LANES), lambda g: (0, g))],
        out_specs=blk,
        out_shape=jax.ShapeDtypeStruct(x3.shape, F32),
        scratch_shapes=[pltpu.VMEM(((SSM_CHUNK + 1) * LANES, LANES), BF16),
                        pltpu.VMEM((SSM_CHUNK * LANES, 2 * ns), BF16),
                        pltpu.VMEM((2 * ns, SSM_CHUNK * LANES), BF16),
                        pltpu.VMEM((rows, SSM_CHUNK * LANES), BF16),
                        pltpu.VMEM((2 * ns // LANES, rows, LANES), F32),
                        pltpu.VMEM((rows, 2 * ns), BF16)],
        compiler_params=_params("parallel"),
        name="ssm",
    )(x3, *params, dskip)


def _ssm_params(lambda_re, lambda_im, log_step, b_re, b_im, c_re, c_im):
    nb, ng, ns = SSM_BLOCKS, SSM_LANE_GROUPS, SSM_BLOCK_STATE
    lam = jnp.stack([lambda_re.astype(F32), lambda_im.astype(F32),
                     jnp.broadcast_to(log_step.astype(F32)[:, None], lambda_re.shape)])
    lam_rows = jnp.transpose(lam.reshape(3, nb, ns), (1, 0, 2))
    lam_cols = jnp.transpose(lam_rows, (0, 2, 1))
    eye = jnp.eye(ng, dtype=F32)

    def blockdiag(m):
        a, b = m.shape[-2:]
        out = jnp.einsum('ngab,gh->ngahb', m.astype(F32).reshape(nb, ng, a, b), eye)
        return out.reshape(nb, ng * a, ng * b)

    bt = lambda m: blockdiag(jnp.swapaxes(m, -1, -2))
    return lam_rows, lam_cols, bt(b_re), bt(b_im), bt(c_re), bt(c_im)


def _router_tables(w_group, b_group, w_expert, b_expert):
    w = jnp.concatenate([w_group.astype(F32),
                         jnp.transpose(w_expert.astype(F32), (1, 0, 2)).reshape(D_MODEL, N_EXPERTS)], axis=1)
    b = jnp.concatenate([b_group.astype(F32), b_expert.astype(F32).reshape(N_EXPERTS)])
    pad = ROUTER_LANES - w.shape[1]
    return jnp.pad(w, ((0, 0), (0, pad))), jnp.pad(b, (0, pad)).reshape(1, ROUTER_LANES)


def kernel(x, norm_mix, norm_ffn, norm_final, attn_w_qkv, attn_w_o, attn_rel_bias, ssm_lambda_re, ssm_lambda_im, ssm_log_step, ssm_b_re, ssm_b_im, ssm_c_re, ssm_c_im, ssm_d, ssm_w_out, moe_w_group_router, moe_b_group_router, moe_w_expert_router, moe_b_expert_router, moe_w_gate, moe_w_up, moe_w_down):
    batch, seq, d = x.shape
    t = batch * seq
    row = lambda v: v.astype(F32).reshape(1, d)
    h = x.astype(F32).reshape(t, d)

    qkv = _qkv(h, row(norm_mix[0]), attn_w_qkv[0].astype(F32))
    o = _attention(qkv, _rel_bias_rows(attn_rel_bias[0]), batch, seq)
    wr, br = _router_tables(moe_w_group_router[0], moe_b_group_router[0],
                            moe_w_expert_router[0], moe_b_expert_router[0])
    h, hn_tt, route, counts = _post(o, h, attn_w_o[0].astype(F32), row(norm_ffn[0]), wr, br, glu=False)
    h, hn32 = _moe(0, h, hn_tt, route, counts, wr, br, moe_w_gate, moe_w_up, moe_w_down, row(norm_mix[1]),
                   emit_h=True)

    ssm_params = _ssm_params(ssm_lambda_re[0], ssm_lambda_im[0], ssm_log_step[0],
                             ssm_b_re[0], ssm_b_im[0], ssm_c_re[0], ssm_c_im[0])
    y = _ssm(hn32.reshape(batch, seq, d), ssm_params, row(ssm_d[0]))
    wr, br = _router_tables(moe_w_group_router[1], moe_b_group_router[1],
                            moe_w_expert_router[1], moe_b_expert_router[1])
    h, hn_tt, route, counts = _post(y.reshape(t, d), h, ssm_w_out[0].astype(F32), row(norm_ffn[1]), wr, br, glu=True)
    _, out = _moe(1, h, hn_tt, route, counts, wr, br, moe_w_gate, moe_w_up, moe_w_down, row(norm_final),
                 emit_h=False)
    return out.reshape(batch, seq, d).astype(x.dtype)
```

```python
import functools

import jax
import jax.numpy as jnp
import numpy as np
from jax import lax
from jax.experimental import pallas as pl
from jax.experimental.pallas import tpu as pltpu

F32 = jnp.float32
BF16 = jnp.bfloat16

D_MODEL = 1024
CHUNK = 64
LOOKBACK_CHUNKS = 8
BAND = (LOOKBACK_CHUNKS + 1) * CHUNK
N_HEADS = 16
HEAD_DIM = D_MODEL // N_HEADS
MAX_REL = 128
SSM_GROUP = 16
SSM_GROUPS = D_MODEL // SSM_GROUP
SSM_STATE = 64
N_EXPERT_GROUPS = 4
EXPERTS_PER_GROUP = 8
N_EXPERTS = N_EXPERT_GROUPS * EXPERTS_PER_GROUP
D_EXPERT = D_MODEL // 4
RMS_EPS = 1e-6
NEG_BIG = -1e30

LANES = 128
SUBLANES = 8
MXU_DIM = 256
VMEM_LIMIT = 60 * 1024 * 1024

HEADS_PER_GROUP = MXU_DIM // HEAD_DIM
N_HEAD_GROUPS = N_HEADS // HEADS_PER_GROUP
ATTN_QBLOCK = LOOKBACK_CHUNKS * CHUNK
ATTN_QROWS = 4 * ATTN_QBLOCK
REL_PAD = -(-(2 * MAX_REL + 1) // LANES) * LANES
BIAS_WIDTH = -(-(BAND + CHUNK - 1) // LANES) * LANES
ROUTER_LANES = LANES
ROUTER_EXPERT_LANE0 = N_EXPERT_GROUPS
SSM_CHUNK = 16
SSM_LANE_GROUPS = LANES // SSM_GROUP
SSM_BLOCKS = D_MODEL // LANES
SSM_BLOCK_STATE = SSM_LANE_GROUPS * SSM_STATE
TOKEN_ROWS = D_MODEL // LANES
MOE_TILE = 256
R_GROUP, R_RANK = range(2)
QKV_ROWS = 512
POST_ROWS = 512
DISPATCH_ROWS = 2048
COMBINE_ROWS = 512
DMA_ISSUE_UNROLL = 4


def _dot(a, b):
    return jnp.dot(a, b, preferred_element_type=F32)


def _rms(x, g):
    return x * lax.rsqrt(jnp.mean(x * x, axis=-1, keepdims=True) + RMS_EPS) * g


def _sigmoid(x):
    return 1.0 / (1.0 + jnp.exp(-x))


def _gelu_tanh(x):
    c = np.float32(np.sqrt(2.0 / np.pi))
    return 0.5 * x * (1.0 + jnp.tanh(c * (x + np.float32(0.044715) * (x * x * x))))


def _params(*sem):
    return pltpu.CompilerParams(dimension_semantics=sem, vmem_limit_bytes=VMEM_LIMIT)


def _resident(shape):
    return pl.BlockSpec(shape, lambda *_: (0,) * len(shape), pipeline_mode=pl.Buffered(1))


def _cast_once(w_ref, wb):
    @pl.when(pl.program_id(0) == 0)
    def _():
        wb[...] = w_ref[...].astype(BF16)


def _qkv_kernel(x_ref, g_ref, w_ref, o_ref, wb):
    _cast_once(w_ref, wb)
    xn = _rms(x_ref[...], g_ref[...]).astype(BF16)
    for c in range(3):
        acc = _dot(xn, wb[:, c * D_MODEL:(c + 1) * D_MODEL])
        if c == 0:
            acc = acc * np.float32(HEAD_DIM ** -0.5)
        o_ref[:, c * D_MODEL:(c + 1) * D_MODEL] = acc.astype(BF16)


def _qkv(x, g, w, tm=QKV_ROWS):
    t = x.shape[0]
    return pl.pallas_call(
        _qkv_kernel,
        grid=(t // tm,),
        in_specs=[pl.BlockSpec((tm, D_MODEL), lambda i: (i, 0)),
                  pl.BlockSpec((1, D_MODEL), lambda i: (0, 0)),
                  _resident(w.shape)],
        out_specs=pl.BlockSpec((tm, 3 * D_MODEL), lambda i: (i, 0)),
        out_shape=jax.ShapeDtypeStruct((t, 3 * D_MODEL), BF16),
        scratch_shapes=[pltpu.VMEM(w.shape, BF16)],
        compiler_params=_params("arbitrary"),
        name="qkv",
    )(x, g, w)


def _expand_rel_bias(rel_ref, bias):
    rel = rel_ref[0]
    r1 = rel.astype(BF16)
    r2 = (rel - r1.astype(F32)).astype(BF16)
    r3 = (rel - r1.astype(F32) - r2.astype(F32)).astype(BF16)
    j = lax.broadcasted_iota(jnp.int32, (REL_PAD, BIAS_WIDTH), 1)
    src = jnp.clip(BAND - 1 - j, -MAX_REL, MAX_REL) + MAX_REL
    sel = jnp.where(lax.broadcasted_iota(jnp.int32, (REL_PAD, BIAS_WIDTH), 0) == src, 1.0, 0.0).astype(BF16)
    u = (_dot(r1, sel) + _dot(r2, sel)) + _dot(r3, sel)
    for h in range(HEADS_PER_GROUP):
        rows = jnp.broadcast_to(u[h:h + 1, :], (CHUNK, BIAS_WIDTH))
        rows = pltpu.roll(rows, BIAS_WIDTH - (CHUNK - 1), 1, stride=1, stride_axis=0)
        bias[h * CHUNK:(h + 1) * CHUNK, :] = rows[:, :BAND]


def _attn_kernel(q_ref, kp_ref, kc_ref, vp_ref, vc_ref, rel_ref, o_ref, kk, vv, bias):
    qb = pl.program_id(2)

    @pl.when(qb == 0)
    def _():
        _expand_rel_bias(rel_ref, bias)

    kk[0:ATTN_QBLOCK, :] = kp_ref[...]
    kk[ATTN_QBLOCK:, :] = kc_ref[...]
    vv[0:ATTN_QBLOCK, :] = vp_ref[...]
    vv[ATTN_QBLOCK:, :] = vc_ref[...]
    lane_head = lax.broadcasted_iota(jnp.int32, (CHUNK, MXU_DIM), 1) // HEAD_DIM
    col = lax.broadcasted_iota(jnp.int32, (1, BAND), 1)

    def chunks(first_block):
        for j in range(ATTN_QROWS // CHUNK):
            qj = q_ref[j * CHUNK:(j + 1) * CHUNK, :]
            lhs = jnp.concatenate(
                [jnp.where(lane_head == h, qj, jnp.zeros_like(qj)) for h in range(HEADS_PER_GROUP)], axis=0)
            kwin = kk[j * CHUNK:j * CHUNK + BAND, :]
            vwin = vv[j * CHUNK:j * CHUNK + BAND, :]
            s = lax.dot_general(lhs, kwin, (((1,), (1,)), ((), ())), preferred_element_type=F32) + bias[...]
            if first_block:
                s = jnp.where(col + j * CHUNK >= ATTN_QBLOCK, s, NEG_BIG)
            m = jnp.max(s, axis=-1, keepdims=True)
            p = jnp.exp(s - m)
            l = jnp.sum(p, axis=-1, keepdims=True)
            o_all = _dot(p.astype(BF16), vwin) * (1.0 / l)
            o = jnp.zeros((CHUNK, MXU_DIM), F32)
            for h in range(HEADS_PER_GROUP):
                o = o + jnp.where(lane_head == h, o_all[h * CHUNK:(h + 1) * CHUNK, :], 0.0)
            o_ref[j * CHUNK:(j + 1) * CHUNK, :] = o.astype(BF16)

    @pl.when(qb == 0)
    def _():
        chunks(True)

    @pl.when(qb > 0)
    def _():
        chunks(False)


def _attention(qkv, rel, batch, seq):
    nqb = seq // ATTN_QROWS
    per = ATTN_QROWS // ATTN_QBLOCK
    kcol = D_MODEL // MXU_DIM
    blk = (ATTN_QROWS, MXU_DIM)
    back = (ATTN_QBLOCK, MXU_DIM)
    cur = lambda off: (lambda b, g, i: (b * nqb + i, off + g))
    prev = lambda off: (lambda b, g, i: ((b * nqb + i) * per - jnp.minimum(i, 1), off + g))
    return pl.pallas_call(
        _attn_kernel,
        grid=(batch, N_HEAD_GROUPS, nqb),
        in_specs=[pl.BlockSpec(blk, cur(0)),
                  pl.BlockSpec(back, prev(kcol)), pl.BlockSpec(blk, cur(kcol)),
                  pl.BlockSpec(back, prev(2 * kcol)), pl.BlockSpec(blk, cur(2 * kcol)),
                  pl.BlockSpec((1, SUBLANES, REL_PAD), lambda b, g, i: (g, 0, 0))],
        out_specs=pl.BlockSpec(blk, cur(0)),
        out_shape=jax.ShapeDtypeStruct((batch * seq, D_MODEL), BF16),
        scratch_shapes=[pltpu.VMEM((ATTN_QBLOCK + ATTN_QROWS, MXU_DIM), BF16),
                        pltpu.VMEM((ATTN_QBLOCK + ATTN_QROWS, MXU_DIM), BF16),
                        pltpu.VMEM((HEADS_PER_GROUP * CHUNK, BAND), F32)],
        compiler_params=_params("parallel", "parallel", "arbitrary"),
        name="attn",
    )(qkv, qkv, qkv, qkv, qkv, rel)


def _rel_bias_rows(rel_bias):
    rel = rel_bias.astype(F32).reshape(N_HEAD_GROUPS, HEADS_PER_GROUP, 2 * MAX_REL + 1)
    return jnp.pad(rel, ((0, 0), (0, SUBLANES - HEADS_PER_GROUP), (0, REL_PAD - (2 * MAX_REL + 1))))


def _router_logits(hn, wr_ref, br_ref, one_matmul=False):
    a_hi = hn.astype(BF16)
    a_lo = (hn - a_hi.astype(F32)).astype(BF16)
    w = wr_ref[...]
    w_hi = w.astype(BF16)
    w_lo = (w - w_hi.astype(F32)).astype(BF16)
    if not one_matmul:
        return _dot(a_hi, w_hi) + (_dot(a_hi, w_lo) + _dot(a_lo, w_hi)) + br_ref[...]
    w2 = jnp.concatenate([jnp.concatenate([w_hi, w_lo], axis=1),
                          jnp.concatenate([w_hi, jnp.zeros_like(w_lo)], axis=1)], axis=0)
    out = _dot(jnp.concatenate([a_hi, a_lo], axis=1), w2)
    return out[:, :ROUTER_LANES] + out[:, ROUTER_LANES:] + br_ref[...]


def _route_group(lg, base):
    rows = lg.shape[0]
    lane = lax.broadcasted_iota(jnp.int32, lg.shape, 1)
    gl = jnp.where(lane < N_EXPERT_GROUPS, lg, np.float32(-np.inf))
    gmax = jnp.max(gl, axis=-1, keepdims=True)
    gidx = jnp.min(jnp.where(gl == gmax, lane, np.int32(ROUTER_LANES)), axis=-1, keepdims=True)
    og = lane == gidx
    tri = (lax.broadcasted_iota(jnp.int32, (rows, rows), 0)
           > lax.broadcasted_iota(jnp.int32, (rows, rows), 1))
    tri = jnp.where(tri, 1.0, 0.0).astype(BF16)
    ogf = jnp.where(og, 1.0, 0.0)
    rank = jnp.sum(jnp.where(og, base + _dot(tri, ogf.astype(BF16)), 0.0), axis=-1, keepdims=True)
    rec = jnp.where(lane == R_GROUP, gidx.astype(F32), jnp.where(lane == R_RANK, rank, 0.0))
    return rec, base + jnp.sum(ogf, axis=0, keepdims=True)


def _route_experts(lg, g):
    lane = lax.broadcasted_iota(jnp.int32, lg.shape, 1)
    ninf = np.float32(-np.inf)
    big = np.int32(ROUTER_LANES)
    gmask = lane < N_EXPERT_GROUPS
    gmax = jnp.max(jnp.where(gmask, lg, ninf), axis=-1, keepdims=True)
    ge = jnp.where(gmask, jnp.exp(lg - gmax), 0.0)
    gprob = jnp.sum(jnp.where(lane == g, ge, 0.0), axis=-1, keepdims=True) / jnp.sum(ge, axis=-1, keepdims=True)
    lo = ROUTER_EXPERT_LANE0 + EXPERTS_PER_GROUP * g
    el = jnp.where((lane >= lo) & (lane < lo + EXPERTS_PER_GROUP), lg, ninf)
    l1 = jnp.max(el, axis=-1, keepdims=True)
    i1 = jnp.min(jnp.where(el == l1, lane, big), axis=-1, keepdims=True)
    el2 = jnp.where(lane == i1, ninf, el)
    l2 = jnp.max(el2, axis=-1, keepdims=True)
    i2 = jnp.min(jnp.where(el2 == l2, lane, big), axis=-1, keepdims=True)
    t = jnp.exp(l2 - l1)
    w1 = gprob / (1.0 + t)
    return i1, w1, i2, w1 * t


def _to_token_tiles(ref, x):
    rows = x.shape[0]
    for j in range(TOKEN_ROWS):
        ref[pl.ds(j, rows, stride=TOKEN_ROWS), :] = x[:, j * LANES:(j + 1) * LANES]


def _from_token_tiles(ref, rows):
    return jnp.concatenate([ref[pl.ds(j, rows, stride=TOKEN_ROWS), :] for j in range(TOKEN_ROWS)], axis=1)


def _post_kernel(glu, y_ref, h_ref, w_ref, g_ref, wr_ref, br_ref, hout_ref, hntt_ref, route_ref, cnt_ref, wb):
    @pl.when(pl.program_id(0) == 0)
    def _():
        cnt_ref[...] = jnp.zeros_like(cnt_ref)

    _cast_once(w_ref, wb)
    y = y_ref[...].astype(BF16)
    if glu:
        mix = _dot(y, wb[:, :D_MODEL]) * _sigmoid(_dot(y, wb[:, D_MODEL:]))
    else:
        mix = _dot(y, wb[...])
    h = h_ref[...] + mix
    hout_ref[...] = h
    hn = _rms(h, g_ref[...])
    _to_token_tiles(hntt_ref, hn)
    route_ref[...], cnt_ref[...] = _route_group(_router_logits(hn, wr_ref, br_ref), cnt_ref[...])


def _post(y, h, w, g, wr, br, glu, tm=POST_ROWS):
    t = h.shape[0]
    row = lambda i: (i, 0)
    fixed = lambda i: (0, 0)
    return pl.pallas_call(
        functools.partial(_post_kernel, glu),
        grid=(t // tm,),
        in_specs=[pl.BlockSpec((tm, D_MODEL), row), pl.BlockSpec((tm, D_MODEL), row),
                  _resident(w.shape), pl.BlockSpec((1, D_MODEL), fixed),
                  pl.BlockSpec((D_MODEL, ROUTER_LANES), fixed), pl.BlockSpec((1, ROUTER_LANES), fixed)],
        out_specs=[pl.BlockSpec((tm, D_MODEL), row), pl.BlockSpec((tm * TOKEN_ROWS, LANES), row),
                   pl.BlockSpec((tm, ROUTER_LANES), row), pl.BlockSpec((1, ROUTER_LANES), fixed)],
        out_shape=[jax.ShapeDtypeStruct((t, D_MODEL), F32), jax.ShapeDtypeStruct((t * TOKEN_ROWS, LANES), F32),
                   jax.ShapeDtypeStruct((t, ROUTER_LANES), F32), jax.ShapeDtypeStruct((1, ROUTER_LANES), F32)],
        scratch_shapes=[pltpu.VMEM(w.shape, BF16)],
        compiler_params=_params("arbitrary"),
        name="post_glu" if glu else "post_attn",
    )(y, h, w, g, wr, br)


def _moe_tiles(t):
    return t // MOE_TILE + N_EXPERT_GROUPS


def _dispatch_plan(route, counts, t):
    g = route[:, R_GROUP].astype(jnp.int32)
    rank = route[:, R_RANK].astype(jnp.int32)
    cnt = counts[0, :N_EXPERT_GROUPS].astype(jnp.int32)
    padded = ((cnt + (MOE_TILE - 1)) // MOE_TILE) * MOE_TILE
    ends = jnp.cumsum(padded)
    off = ends - padded
    onehot = g[:, None] == jnp.arange(N_EXPERT_GROUPS, dtype=jnp.int32)
    pos = jnp.sum(jnp.where(onehot, off, 0), axis=-1) + rank
    ntiles = _moe_tiles(t)
    tile_start = jnp.arange(ntiles, dtype=jnp.int32) * MOE_TILE
    tile_group = jnp.minimum(jnp.sum((tile_start[:, None] >= ends[None, :]).astype(jnp.int32), axis=1),
                             N_EXPERT_GROUPS - 1)
    n_active = (ends[-1] // MOE_TILE).reshape(1)
    plan = jnp.concatenate([off + cnt, padded - cnt, n_active])
    groups = jnp.arange(N_EXPERT_GROUPS, dtype=jnp.int32)
    later = (groups[None, :] > groups[:, None]) & (cnt[None, :] > 0)
    following = jnp.min(jnp.where(later, groups[None, :], N_EXPERT_GROUPS), axis=1)
    following = jnp.where(following == N_EXPERT_GROUPS, -1, following)
    next_group = jnp.sum(jnp.where(tile_group[:, None] == groups[None, :], following[None, :], 0), axis=1)
    return pos, tile_group, n_active, next_group, plan


def _token_tile(ref, i):
    return ref.at[pl.ds(pl.multiple_of(i * TOKEN_ROWS, TOKEN_ROWS), TOKEN_ROWS), :]


def _zero_fill(plan_ref, xs_ref, zbuf, zsem, ntiles):
    nseg = N_EXPERT_GROUPS
    zbuf[...] = jnp.zeros_like(zbuf)
    half = MOE_TILE // 2
    sizes = [1 << b for b in reversed(range(half.bit_length()))]

    def chunk(first, size):
        return pltpu.make_async_copy(zbuf.at[pl.ds(0, size * TOKEN_ROWS), :],
                                     xs_ref.at[pl.ds(pl.multiple_of(first * TOKEN_ROWS, TOKEN_ROWS),
                                                     size * TOKEN_ROWS), :], zsem)

    def pads(wait):
        def body(e, c):
            first = plan_ref[e]
            n = plan_ref[nseg + e]
            for size in sizes:
                hit = (n & size) != 0

                @pl.when(hit)
                def _():
                    cp = chunk(first, size)
                    cp.wait() if wait else cp.start()

                first = first + jnp.where(hit, size, 0)
            return c
        lax.fori_loop(0, nseg, body, 0)

    def tail(wait):
        def body(i, c):
            for k in range(2):
                cp = chunk(i * MOE_TILE + k * half, half)
                cp.wait() if wait else cp.start()
            return c
        lax.fori_loop(plan_ref[2 * nseg], ntiles, body, 0)

    pads(False)
    tail(False)
    pads(True)
    tail(True)


def _dispatch_kernel(ntiles, plan_ref, pos_ref, hn_ref, xs_ref, sem, zbuf, zsem):
    rows = hn_ref.shape[0] // TOKEN_ROWS

    @pl.when(pl.program_id(0) == 0)
    def _():
        _zero_fill(plan_ref, xs_ref, zbuf, zsem, ntiles)

    def copy(r):
        return pltpu.make_async_copy(_token_tile(hn_ref, r), _token_tile(xs_ref, pos_ref[0, 0, r]), sem)

    def issue(r, c):
        copy(2 * r).start(priority=0)
        copy(2 * r + 1).start(priority=1)
        return c

    lax.fori_loop(0, rows // 2, issue, 0, unroll=DMA_ISSUE_UNROLL)
    pltpu.make_async_copy(hn_ref, xs_ref.at[pl.ds(0, rows * TOKEN_ROWS), :], sem).wait()


def _pos_blocks(pos, tm):
    return pos.reshape(pos.shape[0] // tm, 1, tm)


def _dispatch(plan, pos, hn_tt, t, tm=DISPATCH_ROWS):
    ntiles = _moe_tiles(t)
    nrows = ntiles * MOE_TILE * TOKEN_ROWS
    return pl.pallas_call(
        functools.partial(_dispatch_kernel, ntiles),
        grid_spec=pltpu.PrefetchScalarGridSpec(
            num_scalar_prefetch=1,
            grid=(t // tm,),
            in_specs=[pl.BlockSpec((1, 1, tm), lambda i, plan: (i, 0, 0), memory_space=pltpu.SMEM),
                      pl.BlockSpec((tm * TOKEN_ROWS, LANES), lambda i, plan: (i, 0))],
            out_specs=pl.BlockSpec(memory_space=pl.ANY),
            scratch_shapes=[pltpu.SemaphoreType.DMA(()),
                            pltpu.VMEM((MOE_TILE // 2 * TOKEN_ROWS, LANES), F32),
                            pltpu.SemaphoreType.DMA(())]),
        out_shape=jax.ShapeDtypeStruct((nrows, LANES), F32),
        compiler_params=_params("arbitrary"),
        name="moe_dispatch",
    )(plan, _pos_blocks(pos, tm), hn_tt)


def _expert_kernel(layer, tg_ref, na_ref, ng_ref, xs_ref, wr_ref, br_ref, wg_hbm, wu_hbm, wd_hbm, ys_ref,
                   wg_f, wu_f, wd_f, wgu_s, wd_s, sems):
    i = pl.program_id(0)
    g = tg_ref[i]
    new_group = (i == 0) | (tg_ref[jnp.maximum(i - 1, 0)] != g)
    width = EXPERTS_PER_GROUP * D_EXPERT

    def fetch(group):
        return [pltpu.make_async_copy(w.at[layer, group], stage, sems.at[k])
                for k, (w, stage) in enumerate(((wg_hbm, wg_f), (wu_hbm, wu_f), (wd_hbm, wd_f)))]

    @pl.when(i < na_ref[0])
    def _():
        @pl.when(new_group)
        def _():
            @pl.when(i == 0)
            def _():
                for cp in fetch(g):
                    cp.start()

            for cp in fetch(g):
                cp.wait()
            for e in range(EXPERTS_PER_GROUP):
                cols = slice(e * D_EXPERT, (e + 1) * D_EXPERT)
                wgu_s[:, cols] = wg_f[e].astype(BF16)
                wgu_s[:, width + e * D_EXPERT:width + (e + 1) * D_EXPERT] = wu_f[e].astype(BF16)
                wd_s[cols, :] = wd_f[e].astype(BF16)
            nxt = ng_ref[i]

            @pl.when(nxt >= 0)
            def _():
                for cp in fetch(nxt):
                    cp.start()

        x = _from_token_tiles(xs_ref, MOE_TILE)
        i1, w1, i2, w2 = _route_experts(_router_logits(x, wr_ref, br_ref, one_matmul=True), g)
        xb = x.astype(BF16)
        first = ROUTER_EXPERT_LANE0 + g * EXPERTS_PER_GROUP
        hid = []
        for e in range(EXPERTS_PER_GROUP):
            a = _dot(xb, wgu_s[:, e * D_EXPERT:(e + 1) * D_EXPERT])
            u = _dot(xb, wgu_s[:, width + e * D_EXPERT:width + (e + 1) * D_EXPERT])
            gate = jnp.where(i1 == first + e, w1, 0.0) + jnp.where(i2 == first + e, w2, 0.0)
            hid.append(((a * _sigmoid(a)) * u * gate).astype(BF16))
        _to_token_tiles(ys_ref, _dot(jnp.concatenate(hid, axis=1), wd_s[...]))


def _experts(layer, tile_group, n_active, next_group, xs, wr, br, w_gate, w_up, w_down):
    ntiles = tile_group.shape[0]
    last = lambda i, na: jnp.minimum(i, na[0] - 1)
    tile = pl.BlockSpec((MOE_TILE * TOKEN_ROWS, LANES), lambda i, tg, na, ng: (last(i, na), 0))
    fixed = lambda a: pl.BlockSpec(a.shape, lambda i, tg, na, ng: (0, 0))
    grouped = lambda w: w.reshape((w.shape[0], N_EXPERT_GROUPS, EXPERTS_PER_GROUP) + w.shape[2:])
    stage = lambda w: pltpu.VMEM((EXPERTS_PER_GROUP,) + w.shape[2:], F32)
    hbm = pl.BlockSpec(memory_space=pl.ANY)
    width = EXPERTS_PER_GROUP * D_EXPERT
    return pl.pallas_call(
        functools.partial(_expert_kernel, layer),
        grid_spec=pltpu.PrefetchScalarGridSpec(
            num_scalar_prefetch=3,
            grid=(ntiles,),
            in_specs=[tile, fixed(wr), fixed(br), hbm, hbm, hbm],
            out_specs=tile,
            scratch_shapes=[stage(w_gate), stage(w_up), stage(w_down),
                            pltpu.VMEM((D_MODEL, 2 * width), BF16), pltpu.VMEM((width, D_MODEL), BF16),
                            pltpu.SemaphoreType.DMA((3,))]),
        out_shape=jax.ShapeDtypeStruct(xs.shape, F32),
        input_output_aliases={3: 0},
        compiler_params=_params("arbitrary"),
        name="moe_experts",
    )(tile_group, n_active, next_group, xs, wr, br, grouped(w_gate), grouped(w_up), grouped(w_down))


def _combine_kernel(emit_h, pos_ref, nxt_ref, h_ref, gn_ref, ys_ref, *refs):
    outs, (buf0, buf1, sem0, sem1) = refs[:-4], refs[-4:]
    bufs, sems = (buf0, buf1), (sem0, sem1)
    rows = h_ref.shape[0]
    i = pl.program_id(0)

    def gather(idx_ref, slot):
        def copy(r):
            return pltpu.make_async_copy(_token_tile(ys_ref, idx_ref[0, 0, r]), _token_tile(bufs[slot], r),
                                         sems[slot])

        def issue(r, c):
            copy(2 * r).start(priority=0)
            copy(2 * r + 1).start(priority=1)
            return c

        lax.fori_loop(0, rows // 2, issue, 0, unroll=DMA_ISSUE_UNROLL)

    @pl.when(i == 0)
    def _():
        gather(pos_ref, 0)

    for slot in range(2):
        @pl.when(i % 2 == slot)
        def _():
            @pl.when(i + 1 < pl.num_programs(0))
            def _():
                gather(nxt_ref, 1 - slot)

            pltpu.make_async_copy(ys_ref.at[pl.ds(0, rows * TOKEN_ROWS), :], bufs[slot], sems[slot]).wait()
            h = h_ref[...] + _from_token_tiles(bufs[slot], rows)
            if emit_h:
                outs[0][...] = h
            outs[-1][...] = _rms(h, gn_ref[...])


def _combine(pos, h, gnext, ys, emit_h, tm=COMBINE_ROWS):
    t = h.shape[0]
    nblk = t // tm
    row = lambda i: (i, 0)
    blocks = _pos_blocks(pos, tm)
    smem = lambda imap: pl.BlockSpec((1, 1, tm), imap, memory_space=pltpu.SMEM)
    n_out = 2 if emit_h else 1
    out = pl.pallas_call(
        functools.partial(_combine_kernel, emit_h),
        grid=(nblk,),
        in_specs=[smem(lambda i: (i, 0, 0)), smem(lambda i: (jnp.minimum(i + 1, nblk - 1), 0, 0)),
                  pl.BlockSpec((tm, D_MODEL), row),
                  pl.BlockSpec((1, D_MODEL), lambda i: (0, 0)),
                  pl.BlockSpec(memory_space=pl.ANY)],
        out_specs=[pl.BlockSpec((tm, D_MODEL), row)] * n_out,
        out_shape=[jax.ShapeDtypeStruct((t, D_MODEL), F32)] * n_out,
        scratch_shapes=[pltpu.VMEM((tm * TOKEN_ROWS, LANES), F32), pltpu.VMEM((tm * TOKEN_ROWS, LANES), F32),
                        pltpu.SemaphoreType.DMA(()), pltpu.SemaphoreType.DMA(())],
        compiler_params=_params("arbitrary"),
        name="moe_combine",
    )(blocks, blocks, h, gnext, ys)
    return (out[0], out[1]) if emit_h else (None, out[0])


def _moe(layer, h, hn_tt, route, counts, wr, br, w_gate, w_up, w_down, gnext, emit_h):
    t = h.shape[0]
    pos, tile_group, n_active, next_group, plan = _dispatch_plan(route, counts, t)
    xs = _dispatch(plan, pos, hn_tt, t)
    ys = _experts(layer, tile_group, n_active, next_group, xs, wr, br, w_gate, w_up, w_down)
    return _combine(pos, h, gnext, ys, emit_h)


def _dot3(a, b):
    a_hi = a.astype(BF16)
    a_lo = (a - a_hi.astype(F32)).astype(BF16)
    b_hi = b.astype(BF16)
    b_lo = (b - b_hi.astype(F32)).astype(BF16)
    return _dot(a_hi, b_hi) + (_dot(a_hi, b_lo) + _dot(a_lo, b_hi))


def _ssm_powers(lam_re, lam_im, log_step, lags):
    lam_re = jnp.minimum(lam_re, np.float32(-1e-4))
    step = jnp.exp(log_step)
    dec = jnp.exp(lags * (lam_re * step))
    ang = lags * (lam_im * step)
    return dec * jnp.cos(ang), dec * jnp.sin(ang)


def _ssm_operators(lamr_ref, lamc_ref, bre_ref, bim_ref, cre_ref, cim_ref, r_s, wd_s, vd_s):
    L, ns = SSM_CHUNK, SSM_BLOCK_STATE
    lam = lamr_ref[0]
    lre, lim, lst = lam[0:1], lam[1:2], lam[2:3]
    lag_rev = (L - 1 - lax.broadcasted_iota(jnp.int32, (L, 1), 0)).astype(F32)
    pr_re, pr_im = _ssm_powers(lre, lim, lst, lag_rev)
    ab_re, ab_im = _ssm_powers(lre, lim, lst, np.float32(1.0))
    lam_re = jnp.minimum(lre, np.float32(-1e-4))
    denom = lam_re * lam_re + lim * lim
    num_re = ab_re - 1.0
    f_re = (num_re * lam_re + ab_im * lim) / denom
    f_im = (ab_im * lam_re - num_re * lim) / denom
    bre, bim = bre_ref[0], bim_ref[0]
    bb_re = f_re * bre - f_im * bim
    bb_im = f_re * bim + f_im * bre
    cre, cim = cre_ref[0], cim_ref[0]
    for s in range(L):
        a_re, a_im = pr_re[s:s + 1], pr_im[s:s + 1]
        w_re = a_re * bb_re - a_im * bb_im
        w_im = a_re * bb_im + a_im * bb_re
        rows = slice(s * LANES, (s + 1) * LANES)
        wd_s[rows, 0:ns] = w_re.astype(BF16)
        wd_s[rows, ns:2 * ns] = w_im.astype(BF16)
        r_s[rows, :] = (_dot3(w_re, cre) - _dot3(w_im, cim)).astype(BF16)
    r_s[L * LANES:, :] = jnp.zeros((LANES, LANES), BF16)
    lamc = lamc_ref[0]
    lag1 = (1 + lax.broadcasted_iota(jnp.int32, (1, L), 1)).astype(F32)
    p1_re, p1_im = _ssm_powers(lamc[:, 0:1], lamc[:, 1:2], lamc[:, 2:3], lag1)
    for t in range(L):
        a_re, a_im = p1_re[:, t:t + 1], p1_im[:, t:t + 1]
        cols = slice(t * LANES, (t + 1) * LANES)
        vd_s[0:ns, cols] = (cre * a_re - cim * a_im).astype(BF16)
        vd_s[ns:2 * ns, cols] = (-(cre * a_im + cim * a_re)).astype(BF16)
    return _ssm_powers(lre, lim, lst, np.float32(L))


def _ssm_kernel(nch, x_ref, lamr_ref, lamc_ref, bre_ref, bim_ref, cre_ref, cim_ref, d_ref, o_ref,
                r_s, wd_s, vd_s, xcat, st, xprev):
    nb = x_ref.shape[0]
    ns = SSM_BLOCK_STATE
    ar, ai = _ssm_operators(lamr_ref, lamc_ref, bre_ref, bim_ref, cre_ref, cim_ref, r_s, wd_s, vd_s)
    for s in range(SSM_CHUNK):
        for b in range(nb):
            xs = x_ref[b, pl.ds(s, nch, stride=SSM_CHUNK), :]
            xcat[b * nch:(b + 1) * nch, s * LANES:(s + 1) * LANES] = xs.astype(BF16)
    nslab = st.shape[0]
    half = nslab // 2
    for b in range(nb):
        local = _dot(xcat[b * nch:(b + 1) * nch, :], wd_s[...])
        for j in range(nslab):
            st[j, pl.ds(b, nch, stride=nb), :] = local[:, j * LANES:(j + 1) * LANES]

    def step(k, carry):
        rows = pl.ds(pl.multiple_of(k * nb, nb), nb)
        local = [st[j, rows, :] for j in range(nslab)]
        for j in range(nslab):
            st[j, rows, :] = carry[j]
        new_re, new_im = [], []
        for j in range(half):
            xr, xi = carry[j], carry[half + j]
            a_r, a_i = ar[:, j * LANES:(j + 1) * LANES], ai[:, j * LANES:(j + 1) * LANES]
            new_re.append(a_r * xr - a_i * xi + local[j])
            new_im.append(a_r * xi + a_i * xr + local[half + j])
        return tuple(new_re + new_im)

    lax.fori_loop(0, nch, step, tuple(jnp.zeros((nb, LANES), F32) for _ in range(nslab)), unroll=4)
    for b in range(nb):
        for j in range(nslab):
            xprev[b * nch:(b + 1) * nch, j * LANES:(j + 1) * LANES] = (
                st[j, pl.ds(b, nch, stride=nb), :].astype(BF16))
    dskip = d_ref[...]
    L = SSM_CHUNK
    for t in range(0, L, 2):
        taps = jnp.concatenate([r_s[(L - 1 - t) * LANES:, :], r_s[(L - 2 - t) * LANES:L * LANES, :]], axis=1)
        y2 = (_dot(xcat[:, :(t + 2) * LANES], taps)
              + _dot(xprev[...], vd_s[:, t * LANES:(t + 2) * LANES]))
        for k in range(2):
            y = y2[:, k * LANES:(k + 1) * LANES]
            for b in range(nb):
                u = x_ref[b, pl.ds(t + k, nch, stride=L), :]
                o_ref[b, pl.ds(t + k, nch, stride=L), :] = _gelu_tanh(y[b * nch:(b + 1) * nch, :] + dskip * u)


def _ssm(x3, params, dskip):
    nb, seq, _ = x3.shape
    nch = seq // SSM_CHUNK
    rows = nb * nch
    ns = SSM_BLOCK_STATE
    blk = pl.BlockSpec((nb, seq, LANES), lambda g: (0, 0, g))
    per = lambda shape: pl.BlockSpec((1,) + shape, lambda g: (g, 0, 0))
    return pl.pallas_call(
        functools.partial(_ssm_kernel, nch),
        grid=(SSM_BLOCKS,),
        in_specs=[blk, per((3, ns)), per((ns, 3)), per((LANES, ns)), per((LANES, ns)),
                  per((ns, LANES)), per((ns, LANES)), pl.BlockSpec((1, LANES), lambda g: (0, g))],
        out_specs=blk,
        out_shape=jax.ShapeDtypeStruct(x3.shape, F32),
        scratch_shapes=[pltpu.VMEM(((SSM_CHUNK + 1) * LANES, LANES), BF16),
                        pltpu.VMEM((SSM_CHUNK * LANES, 2 * ns), BF16),
                        pltpu.VMEM((2 * ns, SSM_CHUNK * LANES), BF16),
                        pltpu.VMEM((rows, SSM_CHUNK * LANES), BF16),
                        pltpu.VMEM((2 * ns // LANES, rows, LANES), F32),
                        pltpu.VMEM((rows, 2 * ns), BF16)],
        compiler_params=_params("parallel"),
        name="ssm",
    )(x3, *params, dskip)


def _ssm_params(lambda_re, lambda_im, log_step, b_re, b_im, c_re, c_im):
    nb, ng, ns = SSM_BLOCKS, SSM_LANE_GROUPS, SSM_BLOCK_STATE
    lam = jnp.stack([lambda_re.astype(F32), lambda_im.astype(F32),
                     jnp.broadcast_to(log_step.astype(F32)[:, None], lambda_re.shape)])
    lam_rows = jnp.transpose(lam.reshape(3, nb, ns), (1, 0, 2))
    lam_cols = jnp.transpose(lam_rows, (0, 2, 1))
    eye = jnp.eye(ng, dtype=F32)

    def blockdiag(m):
        a, b = m.shape[-2:]
        out = jnp.einsum('ngab,gh->ngahb', m.astype(F32).reshape(nb, ng, a, b), eye)
        return out.reshape(nb, ng * a, ng * b)

    bt = lambda m: blockdiag(jnp.swapaxes(m, -1, -2))
    return lam_rows, lam_cols, bt(b_re), bt(b_im), bt(c_re), bt(c_im)


def _router_tables(w_group, b_group, w_expert, b_expert):
    w = jnp.concatenate([w_group.astype(F32),
                         jnp.transpose(w_expert.astype(F32), (1, 0, 2)).reshape(D_MODEL, N_EXPERTS)], axis=1)
    b = jnp.concatenate([b_group.astype(F32), b_expert.astype(F32).reshape(N_EXPERTS)])
    pad = ROUTER_LANES - w.shape[1]
    return jnp.pad(w, ((0, 0), (0, pad))), jnp.pad(b, (0, pad)).reshape(1, ROUTER_LANES)


def kernel(x, norm_mix, norm_ffn, norm_final, attn_w_qkv, attn_w_o, attn_rel_bias, ssm_lambda_re, ssm_lambda_im, ssm_log_step, ssm_b_re, ssm_b_im, ssm_c_re, ssm_c_im, ssm_d, ssm_w_out, moe_w_group_router, moe_b_group_router, moe_w_expert_router, moe_b_expert_router, moe_w_gate, moe_w_up, moe_w_down):
    batch, seq, d = x.shape
    t = batch * seq
    row = lambda v: v.astype(F32).reshape(1, d)
    h = x.astype(F32).reshape(t, d)

    qkv = _qkv(h, row(norm_mix[0]), attn_w_qkv[0].astype(F32))
    o = _attention(qkv, _rel_bias_rows(attn_rel_bias[0]), batch, seq)
    wr, br = _router_tables(moe_w_group_router[0], moe_b_group_router[0],
                            moe_w_expert_router[0], moe_b_expert_router[0])
    h, hn_tt, route, counts = _post(o, h, attn_w_o[0].astype(F32), row(norm_ffn[0]), wr, br, glu=False)
    h, hn32 = _moe(0, h, hn_tt, route, counts, wr, br, moe_w_gate, moe_w_up, moe_w_down, row(norm_mix[1]),
                   emit_h=True)

    ssm_params = _ssm_params(ssm_lambda_re[0], ssm_lambda_im[0], ssm_log_step[0],
                             ssm_b_re[0], ssm_b_im[0], ssm_c_re[0], ssm_c_im[0])
    y = _ssm(hn32.reshape(batch, seq, d), ssm_params, row(ssm_d[0]))
    wr, br = _router_tables(moe_w_group_router[1], moe_b_group_router[1],
                            moe_w_expert_router[1], moe_b_expert_router[1])
    h, hn_tt, route, counts = _post(y.reshape(t, d), h, ssm_w_out[0].astype(F32), row(norm_ffn[1]), wr, br, glu=True)
    _, out = _moe(1, h, hn_tt, route, counts, wr, br, moe_w_gate, moe_w_up, moe_w_down, row(norm_final),
                 emit_h=False)
    return out.reshape(batch, seq, d).astype(x.dtype)
```
